```python
import math, functools
import jax, jax.numpy as jnp
from jax import lax
import numpy as np

D_MODEL = 2048
BATCH = 32
SEQ = 256
DEPTH = 2
DEC_BATCH = 2
DEC_SEQ = 1024
PAST_LEN = 512

GRID_W = 64
MIX_W = D_MODEL
ATT_W = MIX_W // 2
LRU_W = MIX_W // 4
HY_W = MIX_W // 4
HEAD_DIM = 64
N_HEADS = ATT_W // (2 * HEAD_DIM)
LRU_BLOCKS = 8
LRU_BW = LRU_W // LRU_BLOCKS
LRU_C = 8.0
LRU_CONV = 4
HY_CONV = 3
HY_BANDS = 16
HY_POS = 1 + 2 * HY_BANDS
HY_HIDDEN = 64
HY_DECAY_FAST = 0.3
HY_DECAY_SLOW = 1.5
HY_DECAY_TARGET = 1e-2
ROPE_BASE = 10000.0
Q_BLOCK = 128
EPS = 1e-6
IN_W = 4 * ATT_W + 2 * LRU_W + 4 * HY_W
SPLITS = [ATT_W, 2 * ATT_W, 3 * ATT_W, 4 * ATT_W, 4 * ATT_W + LRU_W, 4 * ATT_W + 2 * LRU_W, 4 * ATT_W + 2 * LRU_W + HY_W, 4 * ATT_W + 2 * LRU_W + 2 * HY_W, 4 * ATT_W + 2 * LRU_W + 3 * HY_W]

kernel_name = 'hymba_diffattn_rglru_hyena_diffusion_step'


def rmsnorm(x, g):
    xf = x.astype(jnp.float32)
    xf = xf * lax.rsqrt(jnp.mean(xf * xf, axis=-1, keepdims=True) + EPS)
    return (xf * g.astype(jnp.float32)).astype(x.dtype)


def dwconv(x, w, b, left, right):
    L = x.shape[1]
    xp = jnp.pad(x, ((0, 0), (left, right), (0, 0)))
    out = xp[:, 0:L] * w[0]
    for t in range(1, w.shape[0]):
        out = out + xp[:, t:t + L] * w[t]
    return out + b


def axial_rope(n_tok):
    rows = n_tok // GRID_W
    r, col = jnp.meshgrid(jnp.arange(rows, dtype=jnp.float32), jnp.arange(GRID_W, dtype=jnp.float32), indexing='ij')
    r = r.reshape(-1)
    col = col.reshape(-1)
    nf = HEAD_DIM // 4
    inv = ROPE_BASE ** (-jnp.arange(nf, dtype=jnp.float32) / nf)
    ang = jnp.stack([r[:, None] * inv, col[:, None] * inv], axis=1)
    return jnp.cos(ang), jnp.sin(ang)


def apply_axial_rope(x, cos, sin):
    nf = HEAD_DIM // 4
    xr = x.reshape(x.shape[:-1] + (2, HEAD_DIM // 2))
    a, b = xr[..., :nf], xr[..., nf:]
    cs = cos[None, :, None, None]
    sn = sin[None, :, None, None]
    out = jnp.concatenate([a * cs - b * sn, b * cs + a * sn], axis=-1)
    return out.reshape(x.shape).astype(x.dtype)


def diff_attention(q, k, v, lam):
    B, Lq = q.shape[0], q.shape[1]
    nblk = Lq // Q_BLOCK
    qb = jnp.moveaxis(q.reshape(B, nblk, Q_BLOCK, N_HEADS, 2, HEAD_DIM), 1, 0)
    scale = HEAD_DIM ** -0.5

    def block(qblk):
        s = jnp.einsum('bqhcd,bkhcd->bchqk', qblk, k, preferred_element_type=jnp.float32) * scale
        p = jax.nn.softmax(s, axis=-1)
        w = p[:, 0] - lam * p[:, 1]
        return jnp.einsum('bhqk,bkhe->bqhe', w.astype(v.dtype), v)

    out = lax.map(block, qb)
    return jnp.moveaxis(out, 0, 1).reshape(B, Lq, N_HEADS, 2 * HEAD_DIM)


def _combine(left, right):
    a_l, b_l = left
    a_r, b_r = right
    return a_l * a_r, a_r * b_l + b_r


def rglru_dir(xc, wa, ba, wi, bi, lam, h0, reverse):
    B, L, W = xc.shape
    xb = xc.reshape(B, L, LRU_BLOCKS, LRU_BW)
    r = jax.nn.sigmoid(jnp.einsum('blnc,ncd->blnd', xb, wa.astype(jnp.float32)).reshape(B, L, W) + ba.astype(jnp.float32))
    i = jax.nn.sigmoid(jnp.einsum('blnc,ncd->blnd', xb, wi.astype(jnp.float32)).reshape(B, L, W) + bi.astype(jnp.float32))
    log_a = -LRU_C * r * jax.nn.softplus(-lam.astype(jnp.float32))
    a = jnp.exp(log_a)
    b = jnp.sqrt(-jnp.expm1(2.0 * log_a)) * (i * xc)
    if reverse:
        a = jnp.flip(a, axis=1)
        b = jnp.flip(b, axis=1)
    b = b.at[:, 0].add(a[:, 0] * h0.astype(jnp.float32))
    _, h = lax.associative_scan(_combine, (a, b), axis=1)
    if reverse:
        h = jnp.flip(h, axis=1)
    return h


def hyena_filters(L, w1, b1, w2, b2, w3):
    pos = jnp.arange(L, dtype=jnp.float32)
    t = pos / float(max(L - 1, 1))
    bands = jnp.linspace(1e-4, HY_BANDS - 1, HY_BANDS, dtype=jnp.float32)
    ang = (2.0 * math.pi / L) * pos[:, None] * bands[None, :]
    z = jnp.concatenate([t[:, None], jnp.cos(ang), jnp.sin(ang)], axis=-1)
    hdn = jnp.sin(jnp.matmul(z, w1.astype(jnp.float32)) + b1.astype(jnp.float32))
    hdn = jnp.sin(jnp.matmul(hdn, w2.astype(jnp.float32)) + b2.astype(jnp.float32))
    filt = jnp.matmul(hdn, w3.astype(jnp.float32)).reshape(L, 2, HY_W)
    lo = abs(math.log(HY_DECAY_TARGET) / HY_DECAY_SLOW)
    hi = abs(math.log(HY_DECAY_TARGET) / HY_DECAY_FAST)
    deltas = jnp.linspace(lo, hi, HY_W, dtype=jnp.float32)
    decay = jnp.exp(-t[:, None] * deltas[None, :])
    filt = filt * decay[:, None, :]
    return filt[:, 0], filt[:, 1]


def bidir_long_conv(u, hf, hb, d):
    L, C = u.shape[1], u.shape[2]
    n = 2 * L
    g = jnp.concatenate([hf, jnp.zeros((1, C), jnp.float32), jnp.flip(hb[1:], axis=0)], axis=0)
    uf = u.astype(jnp.float32)
    y = jnp.fft.irfft(jnp.fft.rfft(uf, n=n, axis=1) * jnp.fft.rfft(g, n=n, axis=0)[None], n=n, axis=1)[:, :L]
    return (y + uf * d.astype(jnp.float32)).astype(u.dtype)


def run_layer(x, cond, rope, ctx_k, ctx_v, h0, lam, lam_init, norm_g, w_ada, b_ada, w_in, w_out, subln_g, lru_conv_w, lru_conv_b, lru_wa, lru_ba, lru_wi, lru_bi, lru_lam, hy_conv_w, hy_conv_b, hy_w1, hy_b1, hy_w2, hy_b2, hy_w3, hy_d):
    B, L, _ = x.shape
    mod = jnp.matmul(jax.nn.silu(cond), w_ada) + b_ada
    shift, scale, gate = jnp.split(mod[:, None, :], 3, axis=-1)
    h = rmsnorm(x, norm_g) * (1 + scale) + shift
    proj = jnp.matmul(h, w_in)
    q, k, v, ag, lx, lg, hv, hx1, hx0, hg = jnp.split(proj, SPLITS, axis=-1)

    q = q.reshape(B, L, N_HEADS, 2, HEAD_DIM)
    k = k.reshape(B, L, N_HEADS, 2, HEAD_DIM)
    v = v.reshape(B, L, N_HEADS, 2 * HEAD_DIM)
    k_store = k.reshape(B, L, N_HEADS, 2 * HEAD_DIM)
    v_store = v
    if rope is not None:
        q = apply_axial_rope(q, rope[0], rope[1])
        k = apply_axial_rope(k, rope[0], rope[1])
    if ctx_k is not None:
        P = ctx_k.shape[1]
        k = jnp.concatenate([ctx_k.reshape(B, P, N_HEADS, 2, HEAD_DIM).astype(k.dtype), k], axis=1)
        v = jnp.concatenate([ctx_v.astype(v.dtype), v], axis=1)
    att = diff_attention(q, k, v, lam)
    att = rmsnorm(att, subln_g) * (1.0 - lam_init)
    att = att.reshape(B, L, ATT_W) * jax.nn.silu(ag)

    xc = dwconv(lx, lru_conv_w, lru_conv_b, 2, 1).astype(jnp.float32)
    if h0 is None:
        h0 = jnp.zeros((B, 2, LRU_W), jnp.float32)
    h_f = rglru_dir(xc, lru_wa[0], lru_ba[0], lru_wi[0], lru_bi[0], lru_lam[0], h0[:, 0], False)
    h_b = rglru_dir(xc, lru_wa[1], lru_ba[1], lru_wi[1], lru_bi[1], lru_lam[1], h0[:, 1], True)
    h_last = jnp.stack([h_f[:, -1], h_b[:, 0]], axis=1)
    lru = (h_f + h_b).astype(x.dtype) * jax.nn.silu(lg)

    u = dwconv(jnp.concatenate([hv, hx1, hx0], axis=-1), hy_conv_w, hy_conv_b, 1, 1)
    hv, hx1, hx0 = jnp.split(u, 3, axis=-1)
    filt_f, filt_b = hyena_filters(L, hy_w1, hy_b1, hy_w2, hy_b2, hy_w3)
    hy = hx0 * bidir_long_conv(hx1 * hv, filt_f, filt_b, hy_d)
    hy = hy * jax.nn.silu(hg)

    out = jnp.matmul(jnp.concatenate([att, lru, hy], axis=-1), w_out)
    return x + gate * out, k_store, v_store, h_last


def setup_inputs(seed: int = 0) -> dict:
    key = jax.random.key(seed)
    ks = jax.random.split(key, 40)

    def nrm(k, shape, s):
        return s * jax.random.normal(k, shape, jnp.float32)

    u = jax.random.uniform(ks[23], (DEPTH, 2, LRU_W), jnp.float32, 0.9, 0.999)
    a = u ** (1.0 / LRU_C)
    return {
        'x_prompt': nrm(ks[0], (BATCH, SEQ, D_MODEL), 1.0),
        'x_sample': nrm(ks[1], (DEC_BATCH, DEC_SEQ, D_MODEL), 1.0),
        'cache_k': nrm(ks[2], (DEC_BATCH, DEPTH, PAST_LEN, N_HEADS, 2 * HEAD_DIM), 1.0),
        'cache_v': nrm(ks[3], (DEC_BATCH, DEPTH, PAST_LEN, N_HEADS, 2 * HEAD_DIM), 1.0),
        'state_lru': nrm(ks[4], (DEC_BATCH, DEPTH, 2, LRU_W), 0.5),
        'c': nrm(ks[5], (DEC_BATCH, D_MODEL), 1.0),
        'c_ctx': nrm(ks[6], (D_MODEL,), 1.0),
        'norm_g': 1.0 + nrm(ks[7], (DEPTH, D_MODEL), 0.02),
        'w_ada': nrm(ks[8], (DEPTH, D_MODEL, 3 * D_MODEL), 0.5 * D_MODEL ** -0.5),
        'b_ada': nrm(ks[9], (DEPTH, 3 * D_MODEL), 0.02),
        'w_in': nrm(ks[10], (DEPTH, D_MODEL, IN_W), D_MODEL ** -0.5),
        'w_out': nrm(ks[11], (DEPTH, MIX_W, D_MODEL), MIX_W ** -0.5),
        'lam_q1': nrm(ks[12], (DEPTH, HEAD_DIM), 0.1),
        'lam_k1': nrm(ks[13], (DEPTH, HEAD_DIM), 0.1),
        'lam_q2': nrm(ks[14], (DEPTH, HEAD_DIM), 0.1),
        'lam_k2': nrm(ks[15], (DEPTH, HEAD_DIM), 0.1),
        'attn_subln_g': 1.0 + nrm(ks[16], (DEPTH, 2 * HEAD_DIM), 0.02),
        'lru_conv_w': nrm(ks[17], (DEPTH, LRU_CONV, LRU_W), LRU_CONV ** -0.5),
        'lru_conv_b': nrm(ks[18], (DEPTH, LRU_W), 0.02),
        'lru_wa': nrm(ks[19], (DEPTH, 2, LRU_BLOCKS, LRU_BW, LRU_BW), LRU_BW ** -0.5),
        'lru_ba': nrm(ks[20], (DEPTH, 2, LRU_W), 0.02),
        'lru_wi': nrm(ks[21], (DEPTH, 2, LRU_BLOCKS, LRU_BW, LRU_BW), LRU_BW ** -0.5),
        'lru_bi': nrm(ks[22], (DEPTH, 2, LRU_W), 0.02),
        'lru_lam': jnp.log(a) - jnp.log1p(-a),
        'hy_conv_w': nrm(ks[24], (DEPTH, HY_CONV, 3 * HY_W), HY_CONV ** -0.5),
        'hy_conv_b': nrm(ks[25], (DEPTH, 3 * HY_W), 0.02),
        'hy_w1': nrm(ks[26], (DEPTH, HY_POS, HY_HIDDEN), HY_POS ** -0.5),
        'hy_b1': nrm(ks[27], (DEPTH, HY_HIDDEN), 0.1),
        'hy_w2': nrm(ks[28], (DEPTH, HY_HIDDEN, HY_HIDDEN), HY_HIDDEN ** -0.5),
        'hy_b2': nrm(ks[29], (DEPTH, HY_HIDDEN), 0.1),
        'hy_w3': nrm(ks[30], (DEPTH, HY_HIDDEN, 2 * HY_W), 0.05 * HY_HIDDEN ** -0.5),
        'hy_d': nrm(ks[31], (DEPTH, HY_W), 0.5),
        'final_g': 1.0 + nrm(ks[32], (D_MODEL,), 0.02),
    }


def reference(x_prompt, x_sample, cache_k, cache_v, state_lru, c, c_ctx, norm_g, w_ada, b_ada, w_in, w_out, lam_q1, lam_k1, lam_q2, lam_k2, attn_subln_g, lru_conv_w, lru_conv_b, lru_wa, lru_ba, lru_wi, lru_bi, lru_lam, hy_conv_w, hy_conv_b, hy_w1, hy_b1, hy_w2, hy_b2, hy_w3, hy_d, final_g):
    rope = axial_rope(x_sample.shape[1])
    ctx_cond = c_ctx[None, :]
    xp, xs = x_prompt, x_sample
    ks, vs, hs = [], [], []
    for l in range(DEPTH):
        lam_init = 0.8 - 0.6 * math.exp(-0.3 * l)
        lam = (jnp.exp(jnp.sum(lam_q1[l].astype(jnp.float32) * lam_k1[l].astype(jnp.float32)))
               - jnp.exp(jnp.sum(lam_q2[l].astype(jnp.float32) * lam_k2[l].astype(jnp.float32))) + lam_init)
        layer = functools.partial(
            run_layer, lam=lam, lam_init=lam_init, norm_g=norm_g[l], w_ada=w_ada[l], b_ada=b_ada[l],
            w_in=w_in[l], w_out=w_out[l], subln_g=attn_subln_g[l], lru_conv_w=lru_conv_w[l],
            lru_conv_b=lru_conv_b[l], lru_wa=lru_wa[l], lru_ba=lru_ba[l], lru_wi=lru_wi[l],
            lru_bi=lru_bi[l], lru_lam=lru_lam[l], hy_conv_w=hy_conv_w[l], hy_conv_b=hy_conv_b[l],
            hy_w1=hy_w1[l], hy_b1=hy_b1[l], hy_w2=hy_w2[l], hy_b2=hy_b2[l], hy_w3=hy_w3[l], hy_d=hy_d[l])
        xp, k_l, v_l, h_l = layer(xp, ctx_cond, None, None, None, None)
        ks.append(k_l)
        vs.append(v_l)
        hs.append(h_l)
        xs, _, _, _ = layer(xs, c, rope, cache_k[:, l], cache_v[:, l], state_lru[:, l])
    y_prompt = rmsnorm(xp, final_g)
    y_sample = rmsnorm(xs, final_g)
    new_cache_k = jnp.stack(ks, axis=1)
    new_cache_v = jnp.stack(vs, axis=1)
    new_state_lru = jnp.stack(hs, axis=1)
    return (y_prompt, y_sample, new_cache_k, new_cache_v, new_state_lru)
```

```python
import functools
import math

import numpy as np
import jax
import jax.numpy as jnp
from jax import lax
from jax.experimental import pallas as pl
from jax.experimental.pallas import tpu as pltpu

D_MODEL = 2048
BATCH = 32
SEQ = 256
DEPTH = 2
DEC_BATCH = 2
DEC_SEQ = 1024
PAST_LEN = 512
GRID_W = 64
ATT_W = 1024
LRU_W = 512
HY_W = 512
HEAD_DIM = 64
N_HEADS = 8
HEAD_W = 2 * HEAD_DIM
LRU_BLOCKS = 8
LRU_BW = LRU_W // LRU_BLOCKS
LRU_C = 8.0
HY_BANDS = 16
HY_POS = 1 + 2 * HY_BANDS
HY_HIDDEN = 64
HY_DECAY_FAST = 0.3
HY_DECAY_SLOW = 1.5
HY_DECAY_TARGET = 1e-2
ROPE_BASE = 10000.0
EPS = 1e-6
IN_W = 4 * ATT_W + 2 * LRU_W + 4 * HY_W

T_CTX = BATCH * SEQ
T_SMP = DEC_BATCH * DEC_SEQ
T_ALL = T_CTX + T_SMP
N_COND = 8

C_Q, C_K, C_V, C_AG = 0, ATT_W, 2 * ATT_W, 3 * ATT_W
C_LG = 4 * ATT_W
C_HG = C_LG + LRU_W
C_LX = C_HG + HY_W
C_HV = C_LX + LRU_W
C_HX1 = C_HV + HY_W
C_HX0 = C_HX1 + HY_W

F32 = jnp.float32
BF16 = jnp.bfloat16
VMEM_LIMIT = 56 * 1024 * 1024


def _sigmoid(x):
    return 1.0 / (1.0 + jnp.exp(-x))


def _silu(x):
    return x * _sigmoid(x)


def _dot(a, b):
    return jnp.dot(a, b, preferred_element_type=F32)


def _dot_nt(a, b):
    return lax.dot_general(a, b, (((1,), (1,)), ((), ())), preferred_element_type=F32)


def _dot_hi(a, b):
    return jnp.dot(a, b, precision=lax.Precision.HIGHEST, preferred_element_type=F32)


def _params(*sem):
    return pltpu.CompilerParams(dimension_semantics=sem, vmem_limit_bytes=VMEM_LIMIT)


def _rope_tables():
    t = np.arange(DEC_SEQ)
    pos = np.stack([t // GRID_W, t % GRID_W], axis=1).astype(np.float64)
    nf = HEAD_DIM // 4
    inv = ROPE_BASE ** (-np.arange(nf, dtype=np.float64) / nf)
    lane = np.arange(HEAD_W)
    j = lane % HEAD_DIM
    axis = j // (HEAD_DIM // 2)
    f = j % nf
    upper = (j % (HEAD_DIM // 2)) >= nf
    ang = pos[:, axis] * inv[f][None, :]
    cos = np.cos(ang)
    sin = np.sin(ang) * np.where(upper, 1.0, -1.0)[None, :]
    return jnp.asarray(cos, F32), jnp.asarray(sin, F32)


def _dft_tables(L):
    n = 2 * L
    k = np.arange(L)[:, None]
    t = np.arange(L)[None, :]
    ang = 2.0 * np.pi * ((k * t) % n).astype(np.float64) / n
    fwd = np.concatenate([np.cos(ang), -np.sin(ang)], axis=0)
    fwd[L, :] = (-1.0) ** np.arange(L)
    wk = np.where(np.arange(L) == 0, 1.0, 2.0)[None, :]
    ang_t = ang.T
    inv = np.concatenate([wk * np.cos(ang_t), -2.0 * np.sin(ang_t)], axis=1) / n
    inv[:, L] = ((-1.0) ** np.arange(L)) / n
    return fwd, inv


def _hyena_tables(L):
    pos = np.arange(L, dtype=np.float64)
    t = pos / float(max(L - 1, 1))
    bands = np.linspace(1e-4, HY_BANDS - 1, HY_BANDS)
    ang = (2.0 * math.pi / L) * pos[:, None] * bands[None, :]
    z = np.concatenate([t[:, None], np.cos(ang), np.sin(ang)], axis=-1)
    lo = abs(math.log(HY_DECAY_TARGET) / HY_DECAY_SLOW)
    hi = abs(math.log(HY_DECAY_TARGET) / HY_DECAY_FAST)
    deltas = np.linspace(lo, hi, HY_W)
    decay = np.exp(-t[:, None] * deltas[None, :])
    return jnp.asarray(z, F32), jnp.asarray(decay, F32)


def _ada_kernel(c_ref, w_ref, b_ref, o_ref):
    s = _silu(c_ref[...])
    o_ref[...] = _dot(s.astype(BF16), w_ref[...].astype(BF16)) + b_ref[...]


def _ada(cond, w_ada, b_ada):
    tn = 1024
    return pl.pallas_call(
        _ada_kernel,
        grid=(DEPTH, 3 * D_MODEL // tn),
        in_specs=[
            pl.BlockSpec((N_COND, D_MODEL), lambda l, j: (0, 0)),
            pl.BlockSpec((None, D_MODEL, tn), lambda l, j: (l, 0, j)),
            pl.BlockSpec((None, 1, tn), lambda l, j: (l, 0, j)),
        ],
        out_specs=pl.BlockSpec((None, N_COND, tn), lambda l, j: (l, 0, j)),
        out_shape=jax.ShapeDtypeStruct((DEPTH, N_COND, 3 * D_MODEL), F32),
        compiler_params=_params("arbitrary", "arbitrary"),
        name="ada",
    )(cond, w_ada, b_ada.reshape(DEPTH, 1, 3 * D_MODEL))


def _cond_of_tile(i, tm):
    n_ctx = T_CTX // tm
    return jnp.where(i < n_ctx, 0, 1 + (i - n_ctx) // (DEC_SEQ // tm))


IN_TM = 1024
IN_TN = 1024
NORM_ROWS = 32


def _inproj_kernel(x_ref, g_ref, shift_ref, scale_ref, w_ref, o_ref, h_ref):
    @pl.when(pl.program_id(1) == 0)
    def _():
        g = g_ref[...]
        sc = 1.0 + scale_ref[...]
        sh = shift_ref[...]

        def body(r, carry):
            rows = pl.ds(pl.multiple_of(r * NORM_ROWS, NORM_ROWS), NORM_ROWS)
            x = x_ref[rows, :]
            ms = jnp.mean(x * x, axis=-1, keepdims=True)
            xn = x * lax.rsqrt(ms + EPS)
            h_ref[rows, :] = ((xn * g) * sc + sh).astype(BF16)
            return carry

        lax.fori_loop(0, IN_TM // NORM_ROWS, body, 0)

    o_ref[...] = _dot(h_ref[...], w_ref[...])


def _inproj(x, norm_g, mod4, w_in_p, layer):
    tm, tn = IN_TM, IN_TN
    cond = functools.partial(_cond_of_tile, tm=tm)
    return pl.pallas_call(
        _inproj_kernel,
        grid=(T_ALL // tm, IN_W // tn),
        in_specs=[
            pl.BlockSpec((tm, D_MODEL), lambda i, j: (i, 0)),
            pl.BlockSpec((1, D_MODEL), lambda i, j: (0, 0)),
            pl.BlockSpec((None, None, 1, D_MODEL), lambda i, j: (layer, cond(i), 0, 0)),
            pl.BlockSpec((None, None, 1, D_MODEL), lambda i, j: (layer, cond(i), 0, 1)),
            pl.BlockSpec((D_MODEL, tn), lambda i, j: (0, j)),
        ],
        out_specs=pl.BlockSpec((tm, tn), lambda i, j: (i, j)),
        out_shape=jax.ShapeDtypeStruct((T_ALL, IN_W), F32),
        scratch_shapes=[pltpu.VMEM((tm, D_MODEL), BF16)],
        compiler_params=_params("arbitrary", "arbitrary"),
        name="inproj",
    )(x, norm_g, mod4, mod4, w_in_p)


def _lam_value(lamp_ref, lam_init):
    lp = lamp_ref[...]
    t1 = jnp.sum(lp[0:1] * lp[1:2], axis=-1, keepdims=True)
    t2 = jnp.sum(lp[2:3] * lp[3:4], axis=-1, keepdims=True)
    return jnp.exp(t1) - jnp.exp(t2) + lam_init


def _diff_attend(q, k_bf, v_bf, lam, g, lam_init):
    m = q.shape[0]
    lane = lax.broadcasted_iota(jnp.int32, q.shape, 1)
    q1 = jnp.where(lane < HEAD_DIM, q, 0.0).astype(BF16)
    q2 = jnp.where(lane < HEAD_DIM, 0.0, q).astype(BF16)
    s = _dot_nt(jnp.concatenate([q1, q2], axis=0), k_bf)
    s = s - jnp.max(s, axis=-1, keepdims=True)
    e = jnp.exp(s)
    p = e * (1.0 / jnp.sum(e, axis=-1, keepdims=True))
    w = p[:m] - lam * p[m:]
    o = _dot(w.astype(BF16), v_bf)
    ms = jnp.mean(o * o, axis=-1, keepdims=True)
    return (o * lax.rsqrt(ms + EPS) * g) * (1.0 - lam_init)


def _attn_ctx_kernel(lamp_ref, g_ref, q_ref, k_ref, v_ref, o_ref, *, lam_init):
    lam = _lam_value(lamp_ref, lam_init)
    g = g_ref[...]
    for h in range(N_HEADS):
        cols = slice(h * HEAD_W, (h + 1) * HEAD_W)
        q = q_ref[:, cols] * (HEAD_DIM ** -0.5)
        o_ref[:, cols] = _diff_attend(q, k_ref[:, cols].astype(BF16), v_ref[:, cols].astype(BF16), lam, g, lam_init)


def _attn_ctx(proj, lamp, subln_g, lam_init):
    blk = lambda c: pl.BlockSpec((SEQ, ATT_W), lambda b, c=c: (b, c))
    return pl.pallas_call(
        functools.partial(_attn_ctx_kernel, lam_init=lam_init),
        grid=(BATCH,),
        in_specs=[
            pl.BlockSpec((4, HEAD_DIM), lambda b: (0, 0)),
            pl.BlockSpec((1, HEAD_W), lambda b: (0, 0)),
            blk(C_Q // ATT_W), blk(C_K // ATT_W), blk(C_V // ATT_W),
        ],
        out_specs=pl.BlockSpec((SEQ, ATT_W), lambda b: (b, 0)),
        out_shape=jax.ShapeDtypeStruct((T_CTX, ATT_W), F32),
        compiler_params=_params("arbitrary"),
        name="attn_ctx",
    )(lamp, subln_g, proj, proj, proj)


Q_CHUNK = 256
N_QC = DEC_SEQ // Q_CHUNK


def _rope(x, cos, sin):
    lane = lax.broadcasted_iota(jnp.int32, x.shape, 1)
    lower = (lane % (HEAD_DIM // 2)) < (HEAD_DIM // 4)
    nf = HEAD_DIM // 4
    partner = jnp.where(lower, pltpu.roll(x, HEAD_W - nf, 1), pltpu.roll(x, nf, 1))
    return x * cos + partner * sin


def _attn_smp_kernel(lamp_ref, g_ref, cosq_ref, sinq_ref, cos_ref, sin_ref, q_ref, k_ref, v_ref,
                     ck_ref, cv_ref, o_ref, ks_ref, vs_ref, *, lam_init):
    @pl.when(pl.program_id(2) == 0)
    def _():
        ks_ref[0:PAST_LEN, :] = ck_ref[...].astype(BF16)
        vs_ref[0:PAST_LEN, :] = cv_ref[...].astype(BF16)
        ks_ref[PAST_LEN:, :] = _rope(k_ref[...], cos_ref[...], sin_ref[...]).astype(BF16)
        vs_ref[PAST_LEN:, :] = v_ref[...].astype(BF16)

    lam = _lam_value(lamp_ref, lam_init)
    q = _rope(q_ref[...], cosq_ref[...], sinq_ref[...]) * (HEAD_DIM ** -0.5)
    o_ref[...] = _diff_attend(q, ks_ref[...], vs_ref[...], lam, g_ref[...], lam_init)


def _attn_smp(proj, cache_k, cache_v, lamp, subln_g, cos, sin, layer, lam_init):
    row0 = T_CTX // Q_CHUNK
    seq0 = T_CTX // DEC_SEQ
    full = lambda shape: pl.BlockSpec(shape, lambda b, h, c: (0, 0))
    return pl.pallas_call(
        functools.partial(_attn_smp_kernel, lam_init=lam_init),
        grid=(DEC_BATCH, N_HEADS, N_QC),
        in_specs=[
            full((4, HEAD_DIM)),
            full((1, HEAD_W)),
            pl.BlockSpec((Q_CHUNK, HEAD_W), lambda b, h, c: (c, 0)),
            pl.BlockSpec((Q_CHUNK, HEAD_W), lambda b, h, c: (c, 0)),
            full((DEC_SEQ, HEAD_W)),
            full((DEC_SEQ, HEAD_W)),
            pl.BlockSpec((Q_CHUNK, HEAD_W), lambda b, h, c: (row0 + b * N_QC + c, C_Q // HEAD_W + h)),
            pl.BlockSpec((DEC_SEQ, HEAD_W), lambda b, h, c: (seq0 + b, C_K // HEAD_W + h)),
            pl.BlockSpec((DEC_SEQ, HEAD_W), lambda b, h, c: (seq0 + b, C_V // HEAD_W + h)),
            pl.BlockSpec((None, None, PAST_LEN, HEAD_W), lambda b, h, c: (b, layer, 0, h)),
            pl.BlockSpec((None, None, PAST_LEN, HEAD_W), lambda b, h, c: (b, layer, 0, h)),
        ],
        out_specs=pl.BlockSpec((Q_CHUNK, HEAD_W), lambda b, h, c: (b * N_QC + c, h)),
        out_shape=jax.ShapeDtypeStruct((T_SMP, ATT_W), F32),
        scratch_shapes=[
            pltpu.VMEM((PAST_LEN + DEC_SEQ, HEAD_W), BF16),
            pltpu.VMEM((PAST_LEN + DEC_SEQ, HEAD_W), BF16),
        ],
        compiler_params=_params("arbitrary", "arbitrary", "arbitrary"),
        name="attn_smp",
    )(lamp, subln_g, cos, sin, cos, sin, proj, proj, proj, cache_k, cache_v)


PAD = 8
GATE_ROWS = 256


def _lru_kernel(x_ref, cw_ref, cb_ref, w_ref, gb_ref, lam_ref, h0_ref, o_ref, hl_ref,
                xpad_ref, a_ref, b_ref, *, L):
    zeros = jnp.zeros((PAD, LRU_W), F32)
    xpad_ref[0:PAD, :] = zeros
    xpad_ref[PAD + L:, :] = zeros
    xpad_ref[PAD:PAD + L, :] = x_ref[...]
    nl = -lam_ref[...]
    sp = jnp.maximum(nl, 0.0) + jnp.log1p(jnp.exp(-jnp.abs(nl)))
    cw = cw_ref[...]
    cb = cb_ref[...]
    gb = gb_ref[...]

    for c in range(L // GATE_ROWS):
        r0 = c * GATE_ROWS
        xc = cb + cw[0:1] * xpad_ref[PAD - 2 + r0:PAD - 2 + r0 + GATE_ROWS, :]
        xc = xc + cw[1:2] * xpad_ref[PAD - 1 + r0:PAD - 1 + r0 + GATE_ROWS, :]
        xc = xc + cw[2:3] * xpad_ref[PAD + r0:PAD + r0 + GATE_ROWS, :]
        xc = xc + cw[3:4] * xpad_ref[PAD + 1 + r0:PAD + 1 + r0 + GATE_ROWS, :]
        gates = _dot(xc.astype(BF16), w_ref[...]) + gb
        for d in range(2):
            r = _sigmoid(gates[:, (2 * d) * LRU_W:(2 * d + 1) * LRU_W])
            i = _sigmoid(gates[:, (2 * d + 1) * LRU_W:(2 * d + 2) * LRU_W])
            log_a = (-LRU_C) * r * sp[d:d + 1]
            a = jnp.exp(log_a)
            b = jnp.sqrt(1.0 - a * a) * (i * xc)
            a_ref[d, r0:r0 + GATE_ROWS, :] = a
            b_ref[d, r0:r0 + GATE_ROWS, :] = b

    def step(t, carry):
        hf, hb = carry
        tb = L - 1 - t
        hf = a_ref[0, pl.ds(t, 1), :] * hf + b_ref[0, pl.ds(t, 1), :]
        b_ref[0, pl.ds(t, 1), :] = hf
        hb = a_ref[1, pl.ds(tb, 1), :] * hb + b_ref[1, pl.ds(tb, 1), :]
        b_ref[1, pl.ds(tb, 1), :] = hb
        return hf, hb

    hf, hb = lax.fori_loop(0, L, step, (h0_ref[0:1, :], h0_ref[1:2, :]))
    hl_ref[0:1, :] = hf
    hl_ref[1:2, :] = hb
    o_ref[...] = b_ref[0] + b_ref[1]


def _lru(proj, conv_w, conv_b, w_gates, b_gates, lam, h0, L, seq0):
    nseq = h0.shape[0]
    full = lambda shape: pl.BlockSpec(shape, lambda s: (0,) * len(shape))
    return pl.pallas_call(
        functools.partial(_lru_kernel, L=L),
        grid=(nseq,),
        in_specs=[
            pl.BlockSpec((L, LRU_W), lambda s: (seq0 + s, C_LX // LRU_W)),
            full((4, LRU_W)),
            full((1, LRU_W)),
            full((LRU_W, 4 * LRU_W)),
            full((1, 4 * LRU_W)),
            full((2, LRU_W)),
            pl.BlockSpec((None, 2, LRU_W), lambda s: (s, 0, 0)),
        ],
        out_specs=[
            pl.BlockSpec((L, LRU_W), lambda s: (s, 0)),
            pl.BlockSpec((None, 2, LRU_W), lambda s: (s, 0, 0)),
        ],
        out_shape=[
            jax.ShapeDtypeStruct((nseq * L, LRU_W), F32),
            jax.ShapeDtypeStruct((nseq, 2, LRU_W), F32),
        ],
        scratch_shapes=[
            pltpu.VMEM((L + 2 * PAD, LRU_W), F32),
            pltpu.VMEM((2, L, LRU_W), F32),
            pltpu.VMEM((2, L, LRU_W), F32),
        ],
        compiler_params=_params("arbitrary"),
        name=f"lru_{L}",
    )(proj, conv_w, conv_b, w_gates, b_gates, lam, h0)


def _lru_gate_weights(wa, wi, ba, bi):
    def dense(w):
        eye = jnp.eye(LRU_BLOCKS, dtype=w.dtype)
        return jnp.einsum("ncd,nm->ncmd", w, eye).reshape(LRU_W, LRU_W)

    w = jnp.concatenate([dense(wa[0]), dense(wi[0]), dense(wa[1]), dense(wi[1])], axis=1)
    b = jnp.concatenate([ba[0], bi[0], ba[1], bi[1]])[None, :]
    return w.astype(BF16), b


FILT_ROWS = 256


def _filt_kernel(z_ref, decay_ref, w1_ref, b1_ref, w2_ref, b2_ref, w3_ref, ft_ref, fb_ref, gt_ref, gx_ref, gd_ref,
                 taps_ref, *, L):
    i = pl.program_id(0)

    @pl.when(i == 0)
    def _():
        h = jnp.sin(_dot_hi(z_ref[...], w1_ref[...]) + b1_ref[...])
        h = jnp.sin(_dot_hi(h, w2_ref[...]) + b2_ref[...])
        filt = _dot_hi(h, w3_ref[...])
        decay = decay_ref[...]
        row = lax.broadcasted_iota(jnp.int32, (L, HY_W), 0)
        taps_ref[:, :HY_W] = filt[:, :HY_W] * decay
        taps_ref[:, HY_W:] = jnp.where(row == 0, 0.0, filt[:, HY_W:] * decay)

    taps = taps_ref[...]
    top = _dot_hi(ft_ref[...], taps)
    bot = _dot_hi(fb_ref[...], taps)
    first = (lax.broadcasted_iota(jnp.int32, (FILT_ROWS, HY_W), 0) + i * FILT_ROWS) == 0
    gt = top[:, :HY_W] + top[:, HY_W:]
    gt_ref[...] = gt
    gx_ref[...] = jnp.where(first, 0.0, bot[:, :HY_W] - bot[:, HY_W:])
    gd_ref[...] = jnp.where(first, bot[:, :HY_W] + bot[:, HY_W:], gt)


def _filters(z, decay, w1, b1, w2, b2, w3, fwd_f32, L):
    full = lambda shape: pl.BlockSpec(shape, lambda i: (0,) * len(shape))
    out = jax.ShapeDtypeStruct((L, HY_W), F32)
    nblk = L // FILT_ROWS
    return pl.pallas_call(
        functools.partial(_filt_kernel, L=L),
        grid=(nblk,),
        in_specs=[
            full((L, HY_POS)), full((L, HY_W)), full((HY_POS, HY_HIDDEN)), full((1, HY_HIDDEN)),
            full((HY_HIDDEN, HY_HIDDEN)), full((1, HY_HIDDEN)), full((HY_HIDDEN, 2 * HY_W)),
            pl.BlockSpec((FILT_ROWS, L), lambda i: (i, 0)),
            pl.BlockSpec((FILT_ROWS, L), lambda i: (nblk + i, 0)),
        ],
        out_specs=[pl.BlockSpec((FILT_ROWS, HY_W), lambda i: (i, 0))] * 3,
        out_shape=[out, out, out],
        scratch_shapes=[pltpu.VMEM((L, 2 * HY_W), F32)],
        compiler_params=_params("arbitrary"),
        name=f"hyena_filters_{L}",
    )(z, decay, w1, b1[None, :], w2, b2[None, :], w3, fwd_f32, fwd_f32)


HY_CH = 256
HY_NCH = HY_W // HY_CH


def _hyena_kernel(hv_ref, hx1_ref, hx0_ref, cwv_ref, cw1_ref, cw0_ref, cbv_ref, cb1_ref, cb0_ref, d_ref,
                  gt_ref, gx_ref, gd_ref, f_ref, fi_ref, o_ref, xpad_ref, *, L):
    zeros = jnp.zeros((PAD, HY_CH), F32)
    xpad_ref[0:PAD, :] = zeros
    xpad_ref[PAD + L:, :] = zeros

    def conv(ref, cw_ref, cb_ref):
        cw = cw_ref[...]
        xpad_ref[PAD:PAD + L, :] = ref[...]
        u = cb_ref[...] + cw[0:1] * xpad_ref[PAD - 1:PAD - 1 + L, :]
        u = u + cw[1:2] * xpad_ref[PAD:PAD + L, :]
        return u + cw[2:3] * xpad_ref[PAD + 1:PAD + 1 + L, :]

    z = conv(hx1_ref, cw1_ref, cb1_ref) * conv(hv_ref, cwv_ref, cbv_ref)
    spec = _dot(f_ref[...], z.astype(BF16))
    top, bot = spec[:L], spec[L:]
    gx = gx_ref[...]
    yt = top * gt_ref[...] - bot * gx
    yb = top * gx + bot * gd_ref[...]
    y = _dot(fi_ref[...], jnp.concatenate([yt, yb], axis=0).astype(BF16))
    o_ref[...] = conv(hx0_ref, cw0_ref, cb0_ref) * (y + z * d_ref[...])


def _hyena(proj, conv_w, conv_b, d, gt, gx, gd, fwd, inv, L, nseq, seq0):
    full = lambda shape: pl.BlockSpec(shape, lambda s, j: (0,) * len(shape))
    col = lambda c: pl.BlockSpec((L, HY_CH), lambda s, j, c=c: (seq0 + s, c // HY_CH + j))
    cwb = lambda rows, st: pl.BlockSpec((rows, HY_CH), lambda s, j, st=st: (0, st * HY_NCH + j))
    chan = lambda rows: pl.BlockSpec((rows, HY_CH), lambda s, j: (0, j))
    return pl.pallas_call(
        functools.partial(_hyena_kernel, L=L),
        grid=(nseq, HY_NCH),
        in_specs=[
            col(C_HV), col(C_HX1), col(C_HX0),
            cwb(3, 0), cwb(3, 1), cwb(3, 2), cwb(1, 0), cwb(1, 1), cwb(1, 2), chan(1),
            chan(L), chan(L), chan(L),
            full((2 * L, L)), full((L, 2 * L)),
        ],
        out_specs=pl.BlockSpec((L, HY_CH), lambda s, j: (s, j)),
        out_shape=jax.ShapeDtypeStruct((nseq * L, HY_W), F32),
        scratch_shapes=[pltpu.VMEM((L + 2 * PAD, HY_CH), F32)],
        compiler_params=_params("arbitrary", "arbitrary"),
        name=f"hyena_{L}",
    )(proj, proj, proj, conv_w, conv_w, conv_w, conv_b, conv_b, conv_b, d, gt, gx, gd, fwd, inv)


OUT_TM = 256


def _outproj_kernel(x_ref, gate_ref, att_ref, lru_ref, hy_ref, ag_ref, lg_ref, hg_ref, w_ref, fg_ref, o_ref, *, final):
    m_att = (att_ref[...] * _silu(ag_ref[...])).astype(BF16)
    m_lru = (lru_ref[...] * _silu(lg_ref[...])).astype(BF16)
    m_hy = (hy_ref[...] * _silu(hg_ref[...])).astype(BF16)
    acc = _dot(m_att, w_ref[0:ATT_W, :])
    acc = acc + _dot(m_lru, w_ref[ATT_W:ATT_W + LRU_W, :])
    acc = acc + _dot(m_hy, w_ref[ATT_W + LRU_W:, :])
    y = x_ref[...] + gate_ref[...] * acc
    if final:
        ms = jnp.mean(y * y, axis=-1, keepdims=True)
        y = (y * lax.rsqrt(ms + EPS)) * fg_ref[...]
    o_ref[...] = y


def _outproj(x, mod4, att, lru, hy, proj, w_out_bf, final_g, layer, final):
    tm = OUT_TM
    cond = functools.partial(_cond_of_tile, tm=tm)
    return pl.pallas_call(
        functools.partial(_outproj_kernel, final=final),
        grid=(T_ALL // tm,),
        in_specs=[
            pl.BlockSpec((tm, D_MODEL), lambda i: (i, 0)),
            pl.BlockSpec((None, None, 1, D_MODEL), lambda i: (layer, cond(i), 0, 2)),
            pl.BlockSpec((tm, ATT_W), lambda i: (i, 0)),
            pl.BlockSpec((tm, LRU_W), lambda i: (i, 0)),
            pl.BlockSpec((tm, HY_W), lambda i: (i, 0)),
            pl.BlockSpec((tm, ATT_W), lambda i: (i, C_AG // ATT_W)),
            pl.BlockSpec((tm, LRU_W), lambda i: (i, C_LG // LRU_W)),
            pl.BlockSpec((tm, HY_W), lambda i: (i, C_HG // HY_W)),
            pl.BlockSpec((D_MODEL, D_MODEL), lambda i: (0, 0)),
            pl.BlockSpec((1, D_MODEL), lambda i: (0, 0)),
        ],
        out_specs=pl.BlockSpec((tm, D_MODEL), lambda i: (i, 0)),
        out_shape=jax.ShapeDtypeStruct((T_ALL, D_MODEL), F32),
        compiler_params=_params("arbitrary"),
        name="outproj",
    )(x, mod4, att, lru, hy, proj, proj, proj, w_out_bf, final_g)


def _permute_w_in(w):
    a, l, h = ATT_W, LRU_W, HY_W
    o = 4 * a
    parts = [w[:, :o], w[:, o + l:o + 2 * l], w[:, o + 2 * l + 3 * h:], w[:, o:o + l], w[:, o + 2 * l:o + 2 * l + 3 * h]]
    return jnp.concatenate(parts, axis=1).astype(BF16)


def kernel(x_prompt, x_sample, cache_k, cache_v, state_lru, c, c_ctx, norm_g, w_ada, b_ada, w_in, w_out, lam_q1, lam_k1, lam_q2, lam_k2, attn_subln_g, lru_conv_w, lru_conv_b, lru_wa, lru_ba, lru_wi, lru_bi, lru_lam, hy_conv_w, hy_conv_b, hy_w1, hy_b1, hy_w2, hy_b2, hy_w3, hy_d, final_g):
    cos, sin = _rope_tables()
    tables = {}
    for L in (SEQ, DEC_SEQ):
        fwd, inv = _dft_tables(L)
        z, decay = _hyena_tables(L)
        fwd, inv = jnp.asarray(fwd, F32), jnp.asarray(inv, F32)
        tables[L] = (fwd, fwd.astype(BF16), inv.astype(BF16), z, decay)

    cond = jnp.concatenate([c_ctx[None, :], c, jnp.zeros((N_COND - 1 - DEC_BATCH, D_MODEL), F32)], axis=0)
    mod4 = _ada(cond, w_ada, b_ada).reshape(DEPTH, N_COND, 1, 3 * D_MODEL)

    x = jnp.concatenate([x_prompt.reshape(T_CTX, D_MODEL), x_sample.reshape(T_SMP, D_MODEL)], axis=0)
    ck = cache_k.reshape(DEC_BATCH, DEPTH, PAST_LEN, ATT_W)
    cv = cache_v.reshape(DEC_BATCH, DEPTH, PAST_LEN, ATT_W)
    h0_ctx = jnp.zeros((BATCH, 2, LRU_W), F32)

    ks, vs, hs = [], [], []
    for l in range(DEPTH):
        lam_init = 0.8 - 0.6 * math.exp(-0.3 * l)
        lamp = jnp.stack([lam_q1[l], lam_k1[l], lam_q2[l], lam_k2[l]], axis=0)
        subln = attn_subln_g[l][None, :]
        proj = _inproj(x, norm_g[l][None, :], mod4, _permute_w_in(w_in[l]), l)
        ks.append(proj[:T_CTX, C_K:C_K + ATT_W].reshape(BATCH, SEQ, N_HEADS, HEAD_W))
        vs.append(proj[:T_CTX, C_V:C_V + ATT_W].reshape(BATCH, SEQ, N_HEADS, HEAD_W))

        att = jnp.concatenate([
            _attn_ctx(proj, lamp, subln, lam_init),
            _attn_smp(proj, ck, cv, lamp, subln, cos, sin, l, lam_init),
        ], axis=0)

        wg, bg = _lru_gate_weights(lru_wa[l], lru_wi[l], lru_ba[l], lru_bi[l])
        lru_args = (lru_conv_w[l], lru_conv_b[l][None, :], wg, bg, lru_lam[l])
        lru_c, h_last = _lru(proj, *lru_args, h0_ctx, SEQ, 0)
        lru_s, _ = _lru(proj, *lru_args, state_lru[:, l], DEC_SEQ, T_CTX // DEC_SEQ)
        hs.append(h_last)
        lru = jnp.concatenate([lru_c, lru_s], axis=0)

        hy_parts = []
        for L, nseq, seq0 in ((SEQ, BATCH, 0), (DEC_SEQ, DEC_BATCH, T_CTX // DEC_SEQ)):
            fwd32, fwd16, inv16, z, decay = tables[L]
            gt, gx, gd = _filters(z, decay, hy_w1[l], hy_b1[l], hy_w2[l], hy_b2[l], hy_w3[l], fwd32, L)
            hy_parts.append(_hyena(proj, hy_conv_w[l], hy_conv_b[l][None, :], hy_d[l][None, :],
                                   gt, gx, gd, fwd16, inv16, L, nseq, seq0))
        hy = jnp.concatenate(hy_parts, axis=0)

        x = _outproj(x, mod4, att, lru, hy, proj, w_out[l].astype(BF16), final_g[None, :], l, l == DEPTH - 1)

    y_prompt = x[:T_CTX].reshape(BATCH, SEQ, D_MODEL)
    y_sample = x[T_CTX:].reshape(DEC_BATCH, DEC_SEQ, D_MODEL)
    return (y_prompt, y_sample, jnp.stack(ks, axis=1), jnp.stack(vs, axis=1), jnp.stack(hs, axis=1))
```

```python
import functools
import math

import numpy as np
import jax
import jax.numpy as jnp
from jax import lax
from jax.experimental import pallas as pl
from jax.experimental.pallas import tpu as pltpu

D_MODEL = 2048
BATCH = 32
SEQ = 256
DEPTH = 2
DEC_BATCH = 2
DEC_SEQ = 1024
PAST_LEN = 512
GRID_W = 64
ATT_W = 1024
LRU_W = 512
HY_W = 512
HEAD_DIM = 64
N_HEADS = 8
HEAD_W = 2 * HEAD_DIM
LRU_BLOCKS = 8
LRU_BW = LRU_W // LRU_BLOCKS
LRU_C = 8.0
HY_BANDS = 16
HY_POS = 1 + 2 * HY_BANDS
HY_HIDDEN = 64
HY_DECAY_FAST = 0.3
HY_DECAY_SLOW = 1.5
HY_DECAY_TARGET = 1e-2
ROPE_BASE = 10000.0
EPS = 1e-6
IN_W = 4 * ATT_W + 2 * LRU_W + 4 * HY_W

T_CTX = BATCH * SEQ
T_SMP = DEC_BATCH * DEC_SEQ
N_COND = 8

SLAB_W = 1024
N_SLABS = IN_W // SLAB_W
S_Q, S_K, S_V, S_AG, S_LGHG, S_LXHV, S_HX = range(N_SLABS)

LANES = 128
F32 = jnp.float32
BF16 = jnp.bfloat16
VMEM_LIMIT = 56 * 1024 * 1024


def _sigmoid(x):
    return 1.0 / (1.0 + jnp.exp(-x))


def _silu(x):
    return x * _sigmoid(x)


def _dot(a, b):
    return jnp.dot(a, b, preferred_element_type=F32)


def _dot_nt(a, b):
    return lax.dot_general(a, b, (((1,), (1,)), ((), ())), preferred_element_type=F32)


def _dot_hi(a, b):
    return jnp.dot(a, b, precision=lax.Precision.HIGHEST, preferred_element_type=F32)


def _params(*sem):
    return pltpu.CompilerParams(dimension_semantics=sem, vmem_limit_bytes=VMEM_LIMIT)


def _rope_tables():
    t = np.arange(DEC_SEQ)
    pos = np.stack([t // GRID_W, t % GRID_W], axis=1).astype(np.float64)
    nf = HEAD_DIM // 4
    inv = ROPE_BASE ** (-np.arange(nf, dtype=np.float64) / nf)
    lane = np.arange(HEAD_W)
    j = lane % HEAD_DIM
    axis = j // (HEAD_DIM // 2)
    f = j % nf
    upper = (j % (HEAD_DIM // 2)) >= nf
    ang = pos[:, axis] * inv[f][None, :]
    cos = np.cos(ang)
    sin = np.sin(ang) * np.where(upper, 1.0, -1.0)[None, :]
    return jnp.asarray(cos, F32), jnp.asarray(sin, F32)


def _dft_tables(L):
    n = 2 * L
    k = np.arange(L)[:, None]
    t = np.arange(L)[None, :]
    ang = 2.0 * np.pi * ((k * t) % n).astype(np.float64) / n
    fwd = np.concatenate([np.cos(ang), -np.sin(ang)], axis=0)
    fwd[L, :] = (-1.0) ** np.arange(L)
    wk = np.where(np.arange(L) == 0, 1.0, 2.0)[None, :]
    ang_t = ang.T
    inv = np.concatenate([wk * np.cos(ang_t), -2.0 * np.sin(ang_t)], axis=1) / n
    inv[:, L] = ((-1.0) ** np.arange(L)) / n
    return jnp.asarray(fwd, F32), jnp.asarray(inv, F32)


def _hyena_tables(L):
    pos = np.arange(L, dtype=np.float64)
    t = pos / float(max(L - 1, 1))
    bands = np.linspace(1e-4, HY_BANDS - 1, HY_BANDS)
    ang = (2.0 * math.pi / L) * pos[:, None] * bands[None, :]
    z = np.concatenate([t[:, None], np.cos(ang), np.sin(ang)], axis=-1)
    lo = abs(math.log(HY_DECAY_TARGET) / HY_DECAY_SLOW)
    hi = abs(math.log(HY_DECAY_TARGET) / HY_DECAY_FAST)
    deltas = np.linspace(lo, hi, HY_W)
    decay = np.exp(-t[:, None] * deltas[None, :])
    return jnp.asarray(z, F32), jnp.asarray(decay, F32)


def _ada_kernel(c_ref, w_ref, b_ref, o_ref):
    s = _silu(c_ref[...])
    o_ref[...] = _dot(s.astype(BF16), w_ref[...].astype(BF16)) + b_ref[...]


def _ada(cond, w_ada, b_ada):
    tn = 1024
    return pl.pallas_call(
        _ada_kernel,
        grid=(DEPTH, 3 * D_MODEL // tn),
        in_specs=[
            pl.BlockSpec((N_COND, D_MODEL), lambda l, j: (0, 0)),
            pl.BlockSpec((None, D_MODEL, tn), lambda l, j: (l, 0, j)),
            pl.BlockSpec((None, 1, tn), lambda l, j: (l, 0, j)),
        ],
        out_specs=pl.BlockSpec((None, N_COND, tn), lambda l, j: (l, 0, j)),
        out_shape=jax.ShapeDtypeStruct((DEPTH, N_COND, 3 * D_MODEL), F32),
        compiler_params=_params("arbitrary", "arbitrary"),
        name="ada",
    )(cond, w_ada, b_ada.reshape(DEPTH, 1, 3 * D_MODEL))


def _cond_row(seq_len, tm):
    if seq_len is None:
        return lambda i: 0
    return lambda i: 1 + i // (seq_len // tm)


IN_TM = 1024
NORM_ROWS = 32


def _inproj_kernel(x_ref, g_ref, shift_ref, scale_ref, w_ref, o_ref, h_ref):
    @pl.when(pl.program_id(1) == 0)
    def _():
        g = g_ref[...]
        sc = 1.0 + scale_ref[...]
        sh = shift_ref[...]

        def body(r, carry):
            rows = pl.ds(pl.multiple_of(r * NORM_ROWS, NORM_ROWS), NORM_ROWS)
            x = x_ref[rows, :]
            ms = jnp.mean(x * x, axis=-1, keepdims=True)
            xn = x * lax.rsqrt(ms + EPS)
            h_ref[rows, :] = ((xn * g) * sc + sh).astype(BF16)
            return carry

        lax.fori_loop(0, IN_TM // NORM_ROWS, body, 0)

    o_ref[...] = _dot(h_ref[...], w_ref[...])


def _inproj(x, norm_g, mod4, w_in_p, layer, seq_len):
    n_tok = x.shape[0]
    tm = IN_TM
    cond = _cond_row(seq_len, tm)
    return pl.pallas_call(
        _inproj_kernel,
        grid=(n_tok // tm, N_SLABS),
        in_specs=[
            pl.BlockSpec((tm, D_MODEL), lambda i, j: (i, 0)),
            pl.BlockSpec((1, D_MODEL), lambda i, j: (0, 0)),
            pl.BlockSpec((None, None, 1, D_MODEL), lambda i, j: (layer, cond(i), 0, 0)),
            pl.BlockSpec((None, None, 1, D_MODEL), lambda i, j: (layer, cond(i), 0, 1)),
            pl.BlockSpec((D_MODEL, SLAB_W), lambda i, j: (0, j)),
        ],
        out_specs=pl.BlockSpec((None, tm, SLAB_W), lambda i, j: (j, i, 0)),
        out_shape=jax.ShapeDtypeStruct((N_SLABS, n_tok, SLAB_W), F32),
        scratch_shapes=[pltpu.VMEM((tm, D_MODEL), BF16)],
        compiler_params=_params("arbitrary", "arbitrary"),
        name="inproj",
    )(x, norm_g, mod4, mod4, w_in_p)


def _lam_value(lamp_ref, lam_init):
    lp = lamp_ref[...]
    t1 = jnp.sum(lp[0:1] * lp[1:2], axis=-1, keepdims=True)
    t2 = jnp.sum(lp[2:3] * lp[3:4], axis=-1, keepdims=True)
    return jnp.exp(t1) - jnp.exp(t2) + lam_init


def _diff_attend(q, k_bf, v_bf, lam, g, lam_init):
    m = q.shape[0]
    lane = lax.broadcasted_iota(jnp.int32, q.shape, 1)
    q1 = jnp.where(lane < HEAD_DIM, q, 0.0).astype(BF16)
    q2 = jnp.where(lane < HEAD_DIM, 0.0, q).astype(BF16)
    s = _dot_nt(jnp.concatenate([q1, q2], axis=0), k_bf)
    s = s - jnp.max(s, axis=-1, keepdims=True)
    e = jnp.exp(s)
    p = e * (1.0 / jnp.sum(e, axis=-1, keepdims=True))
    w = p[:m] - lam * p[m:]
    o = _dot(w.astype(BF16), v_bf)
    ms = jnp.mean(o * o, axis=-1, keepdims=True)
    return (o * lax.rsqrt(ms + EPS) * g) * (1.0 - lam_init)


def _attn_ctx_kernel(lamp_ref, g_ref, q_ref, k_ref, v_ref, o_ref, *, lam_init):
    lam = _lam_value(lamp_ref, lam_init)
    g = g_ref[...]
    for h in range(N_HEADS):
        cols = slice(h * HEAD_W, (h + 1) * HEAD_W)
        q = q_ref[:, cols] * (HEAD_DIM ** -0.5)
        o_ref[:, cols] = _diff_attend(q, k_ref[:, cols].astype(BF16), v_ref[:, cols].astype(BF16), lam, g, lam_init)


def _attn_ctx(proj, lamp, subln_g, lam_init):
    slab = lambda s: pl.BlockSpec((None, SEQ, ATT_W), lambda b, s=s: (s, b, 0))
    return pl.pallas_call(
        functools.partial(_attn_ctx_kernel, lam_init=lam_init),
        grid=(BATCH,),
        in_specs=[
            pl.BlockSpec((4, HEAD_DIM), lambda b: (0, 0)),
            pl.BlockSpec((1, HEAD_W), lambda b: (0, 0)),
            slab(S_Q), slab(S_K), slab(S_V),
        ],
        out_specs=pl.BlockSpec((SEQ, ATT_W), lambda b: (b, 0)),
        out_shape=jax.ShapeDtypeStruct((T_CTX, ATT_W), F32),
        compiler_params=_params("arbitrary"),
        name="attn_ctx",
    )(lamp, subln_g, proj, proj, proj)


Q_CHUNK = 256
N_QC = DEC_SEQ // Q_CHUNK


def _rope(x, cos, sin):
    lane = lax.broadcasted_iota(jnp.int32, x.shape, 1)
    lower = (lane % (HEAD_DIM // 2)) < (HEAD_DIM // 4)
    nf = HEAD_DIM // 4
    partner = jnp.where(lower, pltpu.roll(x, HEAD_W - nf, 1), pltpu.roll(x, nf, 1))
    return x * cos + partner * sin


def _attn_smp_kernel(lamp_ref, g_ref, cosq_ref, sinq_ref, cos_ref, sin_ref, q_ref, k_ref, v_ref,
                     ck_ref, cv_ref, o_ref, ks_ref, vs_ref, *, lam_init):
    @pl.when(pl.program_id(2) == 0)
    def _():
        ks_ref[0:PAST_LEN, :] = ck_ref[...].astype(BF16)
        vs_ref[0:PAST_LEN, :] = cv_ref[...].astype(BF16)
        ks_ref[PAST_LEN:, :] = _rope(k_ref[...], cos_ref[...], sin_ref[...]).astype(BF16)
        vs_ref[PAST_LEN:, :] = v_ref[...].astype(BF16)

    lam = _lam_value(lamp_ref, lam_init)
    q = _rope(q_ref[...], cosq_ref[...], sinq_ref[...]) * (HEAD_DIM ** -0.5)
    o_ref[...] = _diff_attend(q, ks_ref[...], vs_ref[...], lam, g_ref[...], lam_init)


def _attn_smp(proj, cache_k, cache_v, lamp, subln_g, cos, sin, layer, lam_init):
    full = lambda shape: pl.BlockSpec(shape, lambda b, h, c: (0, 0))
    return pl.pallas_call(
        functools.partial(_attn_smp_kernel, lam_init=lam_init),
        grid=(DEC_BATCH, N_HEADS, N_QC),
        in_specs=[
            full((4, HEAD_DIM)),
            full((1, HEAD_W)),
            pl.BlockSpec((Q_CHUNK, HEAD_W), lambda b, h, c: (c, 0)),
            pl.BlockSpec((Q_CHUNK, HEAD_W), lambda b, h, c: (c, 0)),
            full((DEC_SEQ, HEAD_W)),
            full((DEC_SEQ, HEAD_W)),
            pl.BlockSpec((None, Q_CHUNK, HEAD_W), lambda b, h, c: (S_Q, b * N_QC + c, h)),
            pl.BlockSpec((None, DEC_SEQ, HEAD_W), lambda b, h, c: (S_K, b, h)),
            pl.BlockSpec((None, DEC_SEQ, HEAD_W), lambda b, h, c: (S_V, b, h)),
            pl.BlockSpec((None, None, PAST_LEN, HEAD_W), lambda b, h, c: (b, layer, 0, h)),
            pl.BlockSpec((None, None, PAST_LEN, HEAD_W), lambda b, h, c: (b, layer, 0, h)),
        ],
        out_specs=pl.BlockSpec((Q_CHUNK, HEAD_W), lambda b, h, c: (b * N_QC + c, h)),
        out_shape=jax.ShapeDtypeStruct((T_SMP, ATT_W), F32),
        scratch_shapes=[
            pltpu.VMEM((PAST_LEN + DEC_SEQ, HEAD_W), BF16),
            pltpu.VMEM((PAST_LEN + DEC_SEQ, HEAD_W), BF16),
        ],
        compiler_params=_params("arbitrary", "arbitrary", "arbitrary"),
        name="attn_smp",
    )(lamp, subln_g, cos, sin, cos, sin, proj, proj, proj, cache_k, cache_v)


PAD = 8
GATE_ROWS = 256
LRU_SLABS = LRU_W // LANES


def _lru_kernel(x_ref, cw_ref, cb_ref, w_ref, gb_ref, lam_ref, h0_ref, o_ref, hl_ref,
                xpad_ref, a_ref, b_ref, *, L, G):
    pitch = L + PAD
    zeros = jnp.zeros((PAD, LRU_W), F32)
    xpad_ref[0:PAD, :] = zeros
    xpad_ref[PAD + L:, :] = zeros
    nl = -lam_ref[...]
    sp = jnp.maximum(nl, 0.0) + jnp.log1p(jnp.exp(-jnp.abs(nl)))
    cw = cw_ref[...]
    cb = cb_ref[...]
    gb = gb_ref[...]

    def gates_of_sequence(g, carry):
        xpad_ref[PAD:PAD + L, :] = x_ref[pl.ds(pl.multiple_of(g * L, L), L), :]
        base = pl.multiple_of(g * pitch, PAD)
        for c in range(L // GATE_ROWS):
            r0 = c * GATE_ROWS
            xc = cb + cw[0:1] * xpad_ref[PAD - 2 + r0:PAD - 2 + r0 + GATE_ROWS, :]
            xc = xc + cw[1:2] * xpad_ref[PAD - 1 + r0:PAD - 1 + r0 + GATE_ROWS, :]
            xc = xc + cw[2:3] * xpad_ref[PAD + r0:PAD + r0 + GATE_ROWS, :]
            xc = xc + cw[3:4] * xpad_ref[PAD + 1 + r0:PAD + 1 + r0 + GATE_ROWS, :]
            gates = _dot(xc.astype(BF16), w_ref[...]) + gb
            for d in range(2):
                r = _sigmoid(gates[:, (2 * d) * LRU_W:(2 * d + 1) * LRU_W])
                i = _sigmoid(gates[:, (2 * d + 1) * LRU_W:(2 * d + 2) * LRU_W])
                a = jnp.exp((-LRU_C) * r * sp[d:d + 1])
                b = jnp.sqrt(1.0 - a * a) * (i * xc)
                rows = pl.ds(base + r0, GATE_ROWS)
                for s in range(LRU_SLABS):
                    a_ref[d * LRU_SLABS + s, rows, :] = a[:, s * LANES:(s + 1) * LANES]
                    b_ref[d * LRU_SLABS + s, rows, :] = b[:, s * LANES:(s + 1) * LANES]
        return carry

    lax.fori_loop(0, G, gates_of_sequence, 0)

    def step(t, hs):
        out = []
        for d in range(2):
            tt = t if d == 0 else L - 1 - t
            rows = pl.ds(tt, G, stride=pitch)
            for s in range(LRU_SLABS):
                k = d * LRU_SLABS + s
                h = a_ref[k, rows, :] * hs[k] + b_ref[k, rows, :]
                b_ref[k, rows, :] = h
                out.append(h)
        return tuple(out)

    h0 = tuple(h0_ref[d, :, s * LANES:(s + 1) * LANES] for d in range(2) for s in range(LRU_SLABS))
    hs = lax.fori_loop(0, L, step, h0, unroll=2)
    for d in range(2):
        for s in range(LRU_SLABS):
            hl_ref[d, :, s * LANES:(s + 1) * LANES] = hs[d * LRU_SLABS + s]

    def write_sequence(g, carry):
        rows = pl.ds(pl.multiple_of(g * pitch, PAD), L)
        orow = pl.ds(pl.multiple_of(g * L, L), L)
        for s in range(LRU_SLABS):
            o_ref[orow, s * LANES:(s + 1) * LANES] = b_ref[s, rows, :] + b_ref[LRU_SLABS + s, rows, :]
        return carry

    lax.fori_loop(0, G, write_sequence, 0)


def _lru(proj, conv_w, conv_b, w_gates, b_gates, lam, h0, L, G):
    nseq = h0.shape[1]
    pitch = L + PAD
    full = lambda shape: pl.BlockSpec(shape, lambda s: (0,) * len(shape))
    return pl.pallas_call(
        functools.partial(_lru_kernel, L=L, G=G),
        grid=(nseq // G,),
        in_specs=[
            pl.BlockSpec((None, G * L, LRU_W), lambda s: (S_LXHV, s, 0)),
            full((4, LRU_W)),
            full((1, LRU_W)),
            full((LRU_W, 4 * LRU_W)),
            full((1, 4 * LRU_W)),
            full((2, LRU_W)),
            pl.BlockSpec((2, G, LRU_W), lambda s: (0, s, 0)),
        ],
        out_specs=[
            pl.BlockSpec((G * L, LRU_W), lambda s: (s, 0)),
            pl.BlockSpec((2, G, LRU_W), lambda s: (0, s, 0)),
        ],
        out_shape=[
            jax.ShapeDtypeStruct((nseq * L, LRU_W), F32),
            jax.ShapeDtypeStruct((2, nseq, LRU_W), F32),
        ],
        scratch_shapes=[
            pltpu.VMEM((L + 2 * PAD, LRU_W), F32),
            pltpu.VMEM((2 * LRU_SLABS, G * pitch, LANES), F32),
            pltpu.VMEM((2 * LRU_SLABS, G * pitch, LANES), F32),
        ],
        compiler_params=_params("arbitrary"),
        name=f"lru_{L}",
    )(proj, conv_w, conv_b, w_gates, b_gates, lam, h0)


def _lru_gate_weights(wa, wi, ba, bi):
    def dense(w):
        eye = jnp.eye(LRU_BLOCKS, dtype=w.dtype)
        return jnp.einsum("ncd,nm->ncmd", w, eye).reshape(LRU_W, LRU_W)

    w = jnp.concatenate([dense(wa[0]), dense(wi[0]), dense(wa[1]), dense(wi[1])], axis=1)
    b = jnp.concatenate([ba[0], bi[0], ba[1], bi[1]])[None, :]
    return w.astype(BF16), b


FILT_ROWS = 256


def _filt_kernel(z_ref, decay_ref, w1_ref, b1_ref, w2_ref, b2_ref, w3_ref, ft_ref, fb_ref, gt_ref, gx_ref, gd_ref,
                 taps_ref, *, L):
    i = pl.program_id(0)

    @pl.when(i == 0)
    def _():
        h = jnp.sin(_dot_hi(z_ref[...], w1_ref[...]) + b1_ref[...])
        h = jnp.sin(_dot_hi(h, w2_ref[...]) + b2_ref[...])
        filt = _dot_hi(h, w3_ref[...])
        decay = decay_ref[...]
        row = lax.broadcasted_iota(jnp.int32, (L, HY_W), 0)
        taps_ref[:, :HY_W] = (filt[:, :HY_W] * decay).astype(BF16)
        taps_ref[:, HY_W:] = jnp.where(row == 0, 0.0, filt[:, HY_W:] * decay).astype(BF16)

    taps = taps_ref[...]
    top = _dot(ft_ref[...], taps)
    bot = _dot(fb_ref[...], taps)
    first = (lax.broadcasted_iota(jnp.int32, (FILT_ROWS, HY_W), 0) + i * FILT_ROWS) == 0
    gt = top[:, :HY_W] + top[:, HY_W:]
    gt_ref[...] = gt
    gx_ref[...] = jnp.where(first, 0.0, bot[:, :HY_W] - bot[:, HY_W:])
    gd_ref[...] = jnp.where(first, bot[:, :HY_W] + bot[:, HY_W:], gt)


def _filters(z, decay, w1, b1, w2, b2, w3, fwd_bf, L):
    full = lambda shape: pl.BlockSpec(shape, lambda i: (0,) * len(shape))
    out = jax.ShapeDtypeStruct((L, HY_W), F32)
    nblk = L // FILT_ROWS
    return pl.pallas_call(
        functools.partial(_filt_kernel, L=L),
        grid=(nblk,),
        in_specs=[
            full((L, HY_POS)), full((L, HY_W)), full((HY_POS, HY_HIDDEN)), full((1, HY_HIDDEN)),
            full((HY_HIDDEN, HY_HIDDEN)), full((1, HY_HIDDEN)), full((HY_HIDDEN, 2 * HY_W)),
            pl.BlockSpec((FILT_ROWS, L), lambda i: (i, 0)),
            pl.BlockSpec((FILT_ROWS, L), lambda i: (nblk + i, 0)),
        ],
        out_specs=[pl.BlockSpec((FILT_ROWS, HY_W), lambda i: (i, 0))] * 3,
        out_shape=[out, out, out],
        scratch_shapes=[pltpu.VMEM((L, 2 * HY_W), BF16)],
        compiler_params=_params("arbitrary"),
        name=f"hyena_filters_{L}",
    )(z, decay, w1, b1[None, :], w2, b2[None, :], w3, fwd_bf, fwd_bf)


HY_CH = 256
HY_NCH = HY_W // HY_CH


def _hyena_kernel(hv_ref, hx1_ref, hx0_ref, cwv_ref, cw1_ref, cw0_ref, cbv_ref, cb1_ref, cb0_ref, d_ref,
                  gt_ref, gx_ref, gd_ref, f_ref, fi_ref, o_ref, xpad_ref, *, L):
    zeros = jnp.zeros((PAD, HY_CH), F32)
    xpad_ref[0:PAD, :] = zeros
    xpad_ref[PAD + L:, :] = zeros

    def conv(ref, cw_ref, cb_ref):
        cw = cw_ref[...]
        xpad_ref[PAD:PAD + L, :] = ref[...]
        u = cb_ref[...] + cw[0:1] * xpad_ref[PAD - 1:PAD - 1 + L, :]
        u = u + cw[1:2] * xpad_ref[PAD:PAD + L, :]
        return u + cw[2:3] * xpad_ref[PAD + 1:PAD + 1 + L, :]

    z = conv(hx1_ref, cw1_ref, cb1_ref) * conv(hv_ref, cwv_ref, cbv_ref)
    spec = _dot(f_ref[...], z.astype(BF16))
    top, bot = spec[:L], spec[L:]
    gx = gx_ref[...]
    yt = top * gt_ref[...] - bot * gx
    yb = top * gx + bot * gd_ref[...]
    y = _dot(fi_ref[...], jnp.concatenate([yt, yb], axis=0).astype(BF16))
    o_ref[...] = conv(hx0_ref, cw0_ref, cb0_ref) * (y + z * d_ref[...])


def _hyena(proj, conv_w, conv_b, d, gt, gx, gd, fwd, inv, L, nseq):
    full = lambda shape: pl.BlockSpec(shape, lambda s, j: (0,) * len(shape))
    per_slab = SLAB_W // HY_CH
    col = lambda slab, off: pl.BlockSpec((None, L, HY_CH), lambda s, j, slab=slab, off=off: (slab, s, off * HY_NCH + j))
    cwb = lambda rows, st: pl.BlockSpec((rows, HY_CH), lambda s, j, st=st: (0, st * HY_NCH + j))
    chan = lambda rows: pl.BlockSpec((rows, HY_CH), lambda s, j: (0, j))
    assert per_slab == 2 * HY_NCH
    return pl.pallas_call(
        functools.partial(_hyena_kernel, L=L),
        grid=(nseq, HY_NCH),
        in_specs=[
            col(S_LXHV, 1), col(S_HX, 0), col(S_HX, 1),
            cwb(3, 0), cwb(3, 1), cwb(3, 2), cwb(1, 0), cwb(1, 1), cwb(1, 2), chan(1),
            chan(L), chan(L), chan(L),
            full((2 * L, L)), full((L, 2 * L)),
        ],
        out_specs=pl.BlockSpec((L, HY_CH), lambda s, j: (s, j)),
        out_shape=jax.ShapeDtypeStruct((nseq * L, HY_W), F32),
        scratch_shapes=[pltpu.VMEM((L + 2 * PAD, HY_CH), F32)],
        compiler_params=_params("arbitrary", "arbitrary"),
        name=f"hyena_{L}",
    )(proj, proj, proj, conv_w, conv_w, conv_w, conv_b, conv_b, conv_b, d, gt, gx, gd, fwd, inv)


OUT_TM = 256


def _outproj_kernel(x_ref, gate_ref, att_ref, lru_ref, hy_ref, ag_ref, lg_ref, hg_ref, w_ref, fg_ref, o_ref, *, final):
    m_att = (att_ref[...] * _silu(ag_ref[...])).astype(BF16)
    m_lru = (lru_ref[...] * _silu(lg_ref[...])).astype(BF16)
    m_hy = (hy_ref[...] * _silu(hg_ref[...])).astype(BF16)
    acc = _dot(m_att, w_ref[0:ATT_W, :])
    acc = acc + _dot(m_lru, w_ref[ATT_W:ATT_W + LRU_W, :])
    acc = acc + _dot(m_hy, w_ref[ATT_W + LRU_W:, :])
    y = x_ref[...] + gate_ref[...] * acc
    if final:
        ms = jnp.mean(y * y, axis=-1, keepdims=True)
        y = (y * lax.rsqrt(ms + EPS)) * fg_ref[...]
    o_ref[...] = y


def _outproj(x, mod4, att, lru, hy, proj, w_out_bf, final_g, layer, final, seq_len):
    n_tok = x.shape[0]
    tm = OUT_TM
    cond = _cond_row(seq_len, tm)
    return pl.pallas_call(
        functools.partial(_outproj_kernel, final=final),
        grid=(n_tok // tm,),
        in_specs=[
            pl.BlockSpec((tm, D_MODEL), lambda i: (i, 0)),
            pl.BlockSpec((None, None, 1, D_MODEL), lambda i: (layer, cond(i), 0, 2)),
            pl.BlockSpec((tm, ATT_W), lambda i: (i, 0)),
            pl.BlockSpec((tm, LRU_W), lambda i: (i, 0)),
            pl.BlockSpec((tm, HY_W), lambda i: (i, 0)),
            pl.BlockSpec((None, tm, ATT_W), lambda i: (S_AG, i, 0)),
            pl.BlockSpec((None, tm, LRU_W), lambda i: (S_LGHG, i, 0)),
            pl.BlockSpec((None, tm, HY_W), lambda i: (S_LGHG, i, 1)),
            pl.BlockSpec((D_MODEL, D_MODEL), lambda i: (0, 0)),
            pl.BlockSpec((1, D_MODEL), lambda i: (0, 0)),
        ],
        out_specs=pl.BlockSpec((tm, D_MODEL), lambda i: (i, 0)),
        out_shape=jax.ShapeDtypeStruct((n_tok, D_MODEL), F32),
        compiler_params=_params("arbitrary"),
        name="outproj",
    )(x, mod4, att, lru, hy, proj, proj, proj, w_out_bf, final_g)


def _permute_w_in(w):
    a, l, h = ATT_W, LRU_W, HY_W
    o = 4 * a
    parts = [w[:, :o], w[:, o + l:o + 2 * l], w[:, o + 2 * l + 3 * h:], w[:, o:o + l], w[:, o + 2 * l:o + 2 * l + 3 * h]]
    return jnp.concatenate(parts, axis=1).astype(BF16)


def kernel(x_prompt, x_sample, cache_k, cache_v, state_lru, c, c_ctx, norm_g, w_ada, b_ada, w_in, w_out, lam_q1, lam_k1, lam_q2, lam_k2, attn_subln_g, lru_conv_w, lru_conv_b, lru_wa, lru_ba, lru_wi, lru_bi, lru_lam, hy_conv_w, hy_conv_b, hy_w1, hy_b1, hy_w2, hy_b2, hy_w3, hy_d, final_g):
    cos, sin = _rope_tables()
    tables = {}
    for L in (SEQ, DEC_SEQ):
        fwd, inv = _dft_tables(L)
        z, decay = _hyena_tables(L)
        tables[L] = (fwd.astype(BF16), inv.astype(BF16), z, decay)

    cond = jnp.concatenate([c_ctx[None, :], c, jnp.zeros((N_COND - 1 - DEC_BATCH, D_MODEL), F32)], axis=0)
    mod4 = _ada(cond, w_ada, b_ada).reshape(DEPTH, N_COND, 1, 3 * D_MODEL)

    xc = x_prompt.reshape(T_CTX, D_MODEL)
    xs = x_sample.reshape(T_SMP, D_MODEL)
    ck = cache_k.reshape(DEC_BATCH, DEPTH, PAST_LEN, ATT_W)
    cv = cache_v.reshape(DEC_BATCH, DEPTH, PAST_LEN, ATT_W)
    h0_ctx = jnp.zeros((2, BATCH, LRU_W), F32)

    ks, vs, hs = [], [], []
    for l in range(DEPTH):
        final = l == DEPTH - 1
        lam_init = 0.8 - 0.6 * math.exp(-0.3 * l)
        lamp = jnp.stack([lam_q1[l], lam_k1[l], lam_q2[l], lam_k2[l]], axis=0)
        subln = attn_subln_g[l][None, :]
        w_in_p = _permute_w_in(w_in[l])
        w_out_bf = w_out[l].astype(BF16)
        wg, bg = _lru_gate_weights(lru_wa[l], lru_wi[l], lru_ba[l], lru_bi[l])
        lru_args = (lru_conv_w[l], lru_conv_b[l][None, :], wg, bg, lru_lam[l])
        hy_args = (hy_conv_w[l], hy_conv_b[l][None, :], hy_d[l][None, :])
        filt_args = (hy_w1[l], hy_b1[l], hy_w2[l], hy_b2[l], hy_w3[l])

        fwd, inv, z, decay = tables[SEQ]
        proj = _inproj(xc, norm_g[l][None, :], mod4, w_in_p, l, None)
        ks.append(proj[S_K].reshape(BATCH, SEQ, N_HEADS, HEAD_W))
        vs.append(proj[S_V].reshape(BATCH, SEQ, N_HEADS, HEAD_W))
        att = _attn_ctx(proj, lamp, subln, lam_init)
        lru, h_last = _lru(proj, *lru_args, h0_ctx, SEQ, 8)
        hs.append(jnp.transpose(h_last, (1, 0, 2)))
        gt, gx, gd = _filters(z, decay, *filt_args, fwd, SEQ)
        hy = _hyena(proj, *hy_args, gt, gx, gd, fwd, inv, SEQ, BATCH)
        xc = _outproj(xc, mod4, att, lru, hy, proj, w_out_bf, final_g[None, :], l, final, None)

        fwd, inv, z, decay = tables[DEC_SEQ]
        proj = _inproj(xs, norm_g[l][None, :], mod4, w_in_p, l, DEC_SEQ)
        att = _attn_smp(proj, ck, cv, lamp, subln, cos, sin, l, lam_init)
        lru, _ = _lru(proj, *lru_args, jnp.transpose(state_lru[:, l], (1, 0, 2)), DEC_SEQ, DEC_BATCH)
        gt, gx, gd = _filters(z, decay, *filt_args, fwd, DEC_SEQ)
        hy = _hyena(proj, *hy_args, gt, gx, gd, fwd, inv, DEC_SEQ, DEC_BATCH)
        xs = _outproj(xs, mod4, att, lru, hy, proj, w_out_bf, final_g[None, :], l, final, DEC_SEQ)

    y_prompt = xc.reshape(BATCH, SEQ, D_MODEL)
    y_sample = xs.reshape(DEC_BATCH, DEC_SEQ, D_MODEL)
    return (y_prompt, y_sample, jnp.stack(ks, axis=1), jnp.stack(vs, axis=1), jnp.stack(hs, axis=1))
```

```python
import functools
import math

import numpy as np
import jax
import jax.numpy as jnp
from jax import lax
from jax.experimental import pallas as pl
from jax.experimental.pallas import tpu as pltpu

D_MODEL = 2048
BATCH = 32
SEQ = 256
DEPTH = 2
DEC_BATCH = 2
DEC_SEQ = 1024
PAST_LEN = 512
GRID_W = 64
ATT_W = 1024
LRU_W = 512
HY_W = 512
HEAD_DIM = 64
N_HEADS = 8
HEAD_W = 2 * HEAD_DIM
LRU_BLOCKS = 8
LRU_BW = LRU_W // LRU_BLOCKS
LRU_C = 8.0
HY_BANDS = 16
HY_POS = 1 + 2 * HY_BANDS
HY_HIDDEN = 64
HY_DECAY_FAST = 0.3
HY_DECAY_SLOW = 1.5
HY_DECAY_TARGET = 1e-2
ROPE_BASE = 10000.0
EPS = 1e-6
IN_W = 4 * ATT_W + 2 * LRU_W + 4 * HY_W

T_CTX = BATCH * SEQ
T_SMP = DEC_BATCH * DEC_SEQ
N_COND = 8

SLAB_W = 1024
N_SLABS = IN_W // SLAB_W
S_Q, S_K, S_V, S_AG, S_LGHG, S_LXHV, S_HX = range(N_SLABS)

LANES = 128
F32 = jnp.float32
BF16 = jnp.bfloat16
VMEM_LIMIT = 58 * 1024 * 1024


def _sigmoid(x):
    return 1.0 / (1.0 + jnp.exp(-x))


def _silu(x):
    return x * _sigmoid(x)


def _dot(a, b):
    return jnp.dot(a, b, preferred_element_type=F32)


def _dot_nt(a, b):
    return lax.dot_general(a, b, (((1,), (1,)), ((), ())), preferred_element_type=F32)


def _dot_hi(a, b):
    return jnp.dot(a, b, precision=lax.Precision.HIGHEST, preferred_element_type=F32)


def _params(*sem):
    return pltpu.CompilerParams(dimension_semantics=sem, vmem_limit_bytes=VMEM_LIMIT)


def _rope_tables():
    t = np.arange(DEC_SEQ)
    pos = np.stack([t // GRID_W, t % GRID_W], axis=1).astype(np.float64)
    nf = HEAD_DIM // 4
    inv = ROPE_BASE ** (-np.arange(nf, dtype=np.float64) / nf)
    lane = np.arange(HEAD_W)
    j = lane % HEAD_DIM
    axis = j // (HEAD_DIM // 2)
    f = j % nf
    upper = (j % (HEAD_DIM // 2)) >= nf
    ang = pos[:, axis] * inv[f][None, :]
    cos = np.cos(ang)
    sin = np.sin(ang) * np.where(upper, 1.0, -1.0)[None, :]
    return jnp.asarray(cos, F32), jnp.asarray(sin, F32)


def _dft_tables(L):
    n = 2 * L
    k = np.arange(L)[:, None]
    t = np.arange(L)[None, :]
    ang = 2.0 * np.pi * ((k * t) % n).astype(np.float64) / n
    fwd = np.concatenate([np.cos(ang), -np.sin(ang)], axis=0)
    fwd[L, :] = (-1.0) ** np.arange(L)
    wk = np.where(np.arange(L) == 0, 1.0, 2.0)[None, :]
    ang_t = ang.T
    inv = np.concatenate([wk * np.cos(ang_t), -2.0 * np.sin(ang_t)], axis=1) / n
    inv[:, L] = ((-1.0) ** np.arange(L)) / n
    return jnp.asarray(fwd, F32), jnp.asarray(inv, F32)


def _hyena_tables(L):
    pos = np.arange(L, dtype=np.float64)
    t = pos / float(max(L - 1, 1))
    bands = np.linspace(1e-4, HY_BANDS - 1, HY_BANDS)
    ang = (2.0 * math.pi / L) * pos[:, None] * bands[None, :]
    z = np.concatenate([t[:, None], np.cos(ang), np.sin(ang)], axis=-1)
    lo = abs(math.log(HY_DECAY_TARGET) / HY_DECAY_SLOW)
    hi = abs(math.log(HY_DECAY_TARGET) / HY_DECAY_FAST)
    deltas = np.linspace(lo, hi, HY_W)
    decay = np.exp(-t[:, None] * deltas[None, :])
    return jnp.asarray(z, F32), jnp.asarray(decay, F32)


def _ada_kernel(c_ref, w_ref, b_ref, o_ref):
    s = _silu(c_ref[...])
    o_ref[...] = _dot(s.astype(BF16), w_ref[...].astype(BF16)) + b_ref[...]


def _ada(cond, w_ada, b_ada):
    tn = 1024
    return pl.pallas_call(
        _ada_kernel,
        grid=(DEPTH, 3 * D_MODEL // tn),
        in_specs=[
            pl.BlockSpec((N_COND, D_MODEL), lambda l, j: (0, 0)),
            pl.BlockSpec((None, D_MODEL, tn), lambda l, j: (l, 0, j)),
            pl.BlockSpec((None, 1, tn), lambda l, j: (l, 0, j)),
        ],
        out_specs=pl.BlockSpec((None, N_COND, tn), lambda l, j: (l, 0, j)),
        out_shape=jax.ShapeDtypeStruct((DEPTH, N_COND, 3 * D_MODEL), F32),
        compiler_params=_params("arbitrary", "arbitrary"),
        name="ada",
    )(cond, w_ada, b_ada.reshape(DEPTH, 1, 3 * D_MODEL))


def _cond_row(seq_len, tm):
    if seq_len is None:
        return lambda i: 0
    return lambda i: 1 + i // (seq_len // tm)


IN_TM = 1024
NORM_ROWS = 32


def _norm_modulate(x_ref, g_ref, shift_ref, scale_ref, h_ref):
    g = g_ref[...]
    sc = 1.0 + scale_ref[...]
    sh = shift_ref[...]

    def body(r, carry):
        rows = pl.ds(pl.multiple_of(r * NORM_ROWS, NORM_ROWS), NORM_ROWS)
        x = x_ref[rows, :]
        ms = jnp.mean(x * x, axis=-1, keepdims=True)
        xn = x * lax.rsqrt(ms + EPS)
        h_ref[rows, :] = ((xn * g) * sc + sh).astype(BF16)
        return carry

    lax.fori_loop(0, IN_TM // NORM_ROWS, body, 0)


def _inproj_kernel(x_ref, g_ref, shift_ref, scale_ref, w_ref, o_ref, h_ref):
    @pl.when(pl.program_id(1) == 0)
    def _():
        _norm_modulate(x_ref, g_ref, shift_ref, scale_ref, h_ref)

    o_ref[...] = _dot(h_ref[...], w_ref[...])


def _inproj_cache_kernel(*refs, first_layer):
    if first_layer:
        x_ref, g_ref, shift_ref, scale_ref, w_ref, o_ref, kc_ref, vc_ref, h_ref = refs
    else:
        x_ref, g_ref, shift_ref, scale_ref, w_ref, _, _, o_ref, kc_ref, vc_ref, h_ref = refs
    j = pl.program_id(1)

    @pl.when(j == 0)
    def _():
        _norm_modulate(x_ref, g_ref, shift_ref, scale_ref, h_ref)

    o_ref[...] = _dot(h_ref[...], w_ref[...])

    def scatter(dst_ref):
        for b in range(IN_TM // SEQ):
            for h in range(N_HEADS):
                dst_ref[b, pl.ds(h, SEQ, stride=N_HEADS), :] = o_ref[b * SEQ:(b + 1) * SEQ, h * HEAD_W:(h + 1) * HEAD_W]

    @pl.when(j == S_K)
    def _():
        scatter(kc_ref)

    @pl.when(j == S_V)
    def _():
        scatter(vc_ref)

    if first_layer:
        @pl.when(j == S_K + 1)
        def _():
            kc_ref[...] = jnp.zeros(kc_ref.shape, F32)

        @pl.when(j == S_V + 1)
        def _():
            vc_ref[...] = jnp.zeros(vc_ref.shape, F32)


def _inproj_specs(tm, layer, cond):
    return [
        pl.BlockSpec((tm, D_MODEL), lambda i, j: (i, 0)),
        pl.BlockSpec((1, D_MODEL), lambda i, j: (0, 0)),
        pl.BlockSpec((None, None, 1, D_MODEL), lambda i, j: (layer, cond(i), 0, 0)),
        pl.BlockSpec((None, None, 1, D_MODEL), lambda i, j: (layer, cond(i), 0, 1)),
        pl.BlockSpec((D_MODEL, SLAB_W), lambda i, j: (0, j)),
    ]


def _inproj(x, norm_g, mod4, w_in_p, layer, seq_len):
    n_tok = x.shape[0]
    tm = IN_TM
    return pl.pallas_call(
        _inproj_kernel,
        grid=(n_tok // tm, N_SLABS),
        in_specs=_inproj_specs(tm, layer, _cond_row(seq_len, tm)),
        out_specs=pl.BlockSpec((None, tm, SLAB_W), lambda i, j: (j, i, 0)),
        out_shape=jax.ShapeDtypeStruct((N_SLABS, n_tok, SLAB_W), F32),
        scratch_shapes=[pltpu.VMEM((tm, D_MODEL), BF16)],
        compiler_params=_params("arbitrary", "arbitrary"),
        name="inproj",
    )(x, norm_g, mod4, mod4, w_in_p)


def _inproj_cache(x, norm_g, mod4, w_in_p, layer, caches):
    tm = IN_TM
    nb = tm // SEQ
    first = caches is None
    cache_shape = jax.ShapeDtypeStruct((BATCH, DEPTH, SEQ * N_HEADS, HEAD_W), F32)
    if first:
        half = lambda slab: (lambda i, j: (i, jnp.where(j <= slab, layer, layer + 1), 0, 0))
    else:
        half = lambda slab: (lambda i, j: (i, layer, 0, 0))
    cache_spec = lambda slab: pl.BlockSpec((nb, None, SEQ * N_HEADS, HEAD_W), half(slab))
    in_specs = _inproj_specs(tm, layer, _cond_row(None, tm))
    args = (x, norm_g, mod4, mod4, w_in_p)
    aliases = {}
    if not first:
        in_specs += [pl.BlockSpec(memory_space=pl.ANY)] * 2
        aliases = {len(args): 1, len(args) + 1: 2}
        args += tuple(caches)
    return pl.pallas_call(
        functools.partial(_inproj_cache_kernel, first_layer=first),
        grid=(T_CTX // tm, N_SLABS),
        in_specs=in_specs,
        out_specs=[pl.BlockSpec((None, tm, SLAB_W), lambda i, j: (j, i, 0)), cache_spec(S_K), cache_spec(S_V)],
        out_shape=[jax.ShapeDtypeStruct((N_SLABS, T_CTX, SLAB_W), F32), cache_shape, cache_shape],
        scratch_shapes=[pltpu.VMEM((tm, D_MODEL), BF16)],
        input_output_aliases=aliases,
        compiler_params=_params("arbitrary", "arbitrary"),
        name="inproj_cache",
    )(*args)


def _lam_value(lamp_ref, lam_init):
    lp = lamp_ref[...]
    t1 = jnp.sum(lp[0:1] * lp[1:2], axis=-1, keepdims=True)
    t2 = jnp.sum(lp[2:3] * lp[3:4], axis=-1, keepdims=True)
    return jnp.exp(t1) - jnp.exp(t2) + lam_init


def _diff_attend(items, lam, g, lam_init):
    m = items[0][0].shape[0]
    lane = lax.broadcasted_iota(jnp.int32, (m, HEAD_W), 1)
    ss = []
    for q, k_bf, _ in items:
        q1 = jnp.where(lane < HEAD_DIM, q, 0.0).astype(BF16)
        q2 = jnp.where(lane < HEAD_DIM, 0.0, q).astype(BF16)
        ss.append(_dot_nt(jnp.concatenate([q1, q2], axis=0), k_bf))
    es = [jnp.exp(s - jnp.max(s, axis=-1, keepdims=True)) for s in ss]
    ps = [e * (1.0 / jnp.sum(e, axis=-1, keepdims=True)) for e in es]
    ws = [(p[:m] - lam * p[m:]).astype(BF16) for p in ps]
    os_ = [_dot(w, v_bf) for w, (_, _, v_bf) in zip(ws, items)]
    outs = []
    for o in os_:
        ms = jnp.mean(o * o, axis=-1, keepdims=True)
        outs.append((o * lax.rsqrt(ms + EPS) * g) * (1.0 - lam_init))
    return outs


def _attn_ctx_kernel(lamp_ref, g_ref, q_ref, k_ref, v_ref, o_ref, *, lam_init):
    lam = _lam_value(lamp_ref, lam_init)
    cols = [slice(h * HEAD_W, (h + 1) * HEAD_W) for h in range(N_HEADS)]
    items = [(q_ref[:, c] * (HEAD_DIM ** -0.5), k_ref[:, c].astype(BF16), v_ref[:, c].astype(BF16)) for c in cols]
    for c, o in zip(cols, _diff_attend(items, lam, g_ref[...], lam_init)):
        o_ref[:, c] = o


def _attn_ctx(proj, lamp, subln_g, lam_init):
    slab = lambda s: pl.BlockSpec((None, SEQ, ATT_W), lambda b, s=s: (s, b, 0))
    return pl.pallas_call(
        functools.partial(_attn_ctx_kernel, lam_init=lam_init),
        grid=(BATCH,),
        in_specs=[
            pl.BlockSpec((4, HEAD_DIM), lambda b: (0, 0)),
            pl.BlockSpec((1, HEAD_W), lambda b: (0, 0)),
            slab(S_Q), slab(S_K), slab(S_V),
        ],
        out_specs=pl.BlockSpec((SEQ, ATT_W), lambda b: (b, 0)),
        out_shape=jax.ShapeDtypeStruct((T_CTX, ATT_W), F32),
        compiler_params=_params("arbitrary"),
        name="attn_ctx",
    )(lamp, subln_g, proj, proj, proj)


Q_CHUNK = 512
Q_PART = 256
N_QC = DEC_SEQ // Q_CHUNK


def _rope(x, cos, sin):
    lane = lax.broadcasted_iota(jnp.int32, x.shape, 1)
    lower = (lane % (HEAD_DIM // 2)) < (HEAD_DIM // 4)
    nf = HEAD_DIM // 4
    partner = jnp.where(lower, pltpu.roll(x, HEAD_W - nf, 1), pltpu.roll(x, nf, 1))
    return x * cos + partner * sin


def _attn_smp_kernel(lamp_ref, g_ref, cosq_ref, sinq_ref, cos_ref, sin_ref, q_ref, k_ref, v_ref,
                     ck_ref, cv_ref, o_ref, ks_ref, vs_ref, *, lam_init):
    @pl.when(pl.program_id(2) == 0)
    def _():
        ks_ref[0:PAST_LEN, :] = ck_ref[...].astype(BF16)
        vs_ref[0:PAST_LEN, :] = cv_ref[...].astype(BF16)
        ks_ref[PAST_LEN:, :] = _rope(k_ref[...], cos_ref[...], sin_ref[...]).astype(BF16)
        vs_ref[PAST_LEN:, :] = v_ref[...].astype(BF16)

    lam = _lam_value(lamp_ref, lam_init)
    q = _rope(q_ref[...], cosq_ref[...], sinq_ref[...]) * (HEAD_DIM ** -0.5)
    k_bf, v_bf = ks_ref[...], vs_ref[...]
    parts = [slice(r, r + Q_PART) for r in range(0, Q_CHUNK, Q_PART)]
    items = [(q[p], k_bf, v_bf) for p in parts]
    for p, o in zip(parts, _diff_attend(items, lam, g_ref[...], lam_init)):
        o_ref[p, :] = o


def _attn_smp(proj, cache_k, cache_v, lamp, subln_g, cos, sin, layer, lam_init):
    full = lambda shape: pl.BlockSpec(shape, lambda b, h, c: (0, 0))
    return pl.pallas_call(
        functools.partial(_attn_smp_kernel, lam_init=lam_init),
        grid=(DEC_BATCH, N_HEADS, N_QC),
        in_specs=[
            full((4, HEAD_DIM)),
            full((1, HEAD_W)),
            pl.BlockSpec((Q_CHUNK, HEAD_W), lambda b, h, c: (c, 0)),
            pl.BlockSpec((Q_CHUNK, HEAD_W), lambda b, h, c: (c, 0)),
            full((DEC_SEQ, HEAD_W)),
            full((DEC_SEQ, HEAD_W)),
            pl.BlockSpec((None, Q_CHUNK, HEAD_W), lambda b, h, c: (S_Q, b * N_QC + c, h)),
            pl.BlockSpec((None, DEC_SEQ, HEAD_W), lambda b, h, c: (S_K, b, h)),
            pl.BlockSpec((None, DEC_SEQ, HEAD_W), lambda b, h, c: (S_V, b, h)),
            pl.BlockSpec((None, None, PAST_LEN, HEAD_W), lambda b, h, c: (b, layer, 0, h)),
            pl.BlockSpec((None, None, PAST_LEN, HEAD_W), lambda b, h, c: (b, layer, 0, h)),
        ],
        out_specs=pl.BlockSpec((Q_CHUNK, HEAD_W), lambda b, h, c: (b * N_QC + c, h)),
        out_shape=jax.ShapeDtypeStruct((T_SMP, ATT_W), F32),
        scratch_shapes=[
            pltpu.VMEM((PAST_LEN + DEC_SEQ, HEAD_W), BF16),
            pltpu.VMEM((PAST_LEN + DEC_SEQ, HEAD_W), BF16),
        ],
        compiler_params=_params("arbitrary", "arbitrary", "arbitrary"),
        name="attn_smp",
    )(lamp, subln_g, cos, sin, cos, sin, proj, proj, proj, cache_k, cache_v)


PAD = 8
GATE_ROWS = 256
LRU_SLABS = LRU_W // LANES


def _lru_kernel(x_ref, cw_ref, cb_ref, w_ref, gb_ref, lam_ref, h0_ref, o_ref, hl_ref,
                xpad_ref, a_ref, b_ref, *, L, G):
    pitch = L + PAD
    zeros = jnp.zeros((PAD, LRU_W), F32)
    xpad_ref[0:PAD, :] = zeros
    xpad_ref[PAD + L:, :] = zeros
    nl = -lam_ref[...]
    sp = jnp.maximum(nl, 0.0) + jnp.log1p(jnp.exp(-jnp.abs(nl)))
    cw = cw_ref[...]
    cb = cb_ref[...]
    gb = gb_ref[...]

    def gates_of_sequence(g, carry):
        xpad_ref[PAD:PAD + L, :] = x_ref[pl.ds(pl.multiple_of(g * L, L), L), :]
        base = pl.multiple_of(g * pitch, PAD)
        for c in range(L // GATE_ROWS):
            r0 = c * GATE_ROWS
            xc = cb + cw[0:1] * xpad_ref[PAD - 2 + r0:PAD - 2 + r0 + GATE_ROWS, :]
            xc = xc + cw[1:2] * xpad_ref[PAD - 1 + r0:PAD - 1 + r0 + GATE_ROWS, :]
            xc = xc + cw[2:3] * xpad_ref[PAD + r0:PAD + r0 + GATE_ROWS, :]
            xc = xc + cw[3:4] * xpad_ref[PAD + 1 + r0:PAD + 1 + r0 + GATE_ROWS, :]
            gates = _dot(xc.astype(BF16), w_ref[...]) + gb
            for d in range(2):
                r = _sigmoid(gates[:, (2 * d) * LRU_W:(2 * d + 1) * LRU_W])
                i = _sigmoid(gates[:, (2 * d + 1) * LRU_W:(2 * d + 2) * LRU_W])
                a = jnp.exp((-LRU_C) * r * sp[d:d + 1])
                b = jnp.sqrt(1.0 - a * a) * (i * xc)
                rows = pl.ds(base + r0, GATE_ROWS)
                for s in range(LRU_SLABS):
                    a_ref[d * LRU_SLABS + s, rows, :] = a[:, s * LANES:(s + 1) * LANES]
                    b_ref[d * LRU_SLABS + s, rows, :] = b[:, s * LANES:(s + 1) * LANES]
        return carry

    lax.fori_loop(0, G, gates_of_sequence, 0)

    def step(t, hs):
        out = []
        for d in range(2):
            tt = t if d == 0 else L - 1 - t
            rows = pl.ds(tt, G, stride=pitch)
            for s in range(LRU_SLABS):
                k = d * LRU_SLABS + s
                h = a_ref[k, rows, :] * hs[k] + b_ref[k, rows, :]
                b_ref[k, rows, :] = h
                out.append(h)
        return tuple(out)

    h0 = tuple(h0_ref[d, :, s * LANES:(s + 1) * LANES] for d in range(2) for s in range(LRU_SLABS))
    hs = lax.fori_loop(0, L, step, h0, unroll=2)
    for d in range(2):
        for s in range(LRU_SLABS):
            hl_ref[d, :, s * LANES:(s + 1) * LANES] = hs[d * LRU_SLABS + s]

    def write_sequence(g, carry):
        rows = pl.ds(pl.multiple_of(g * pitch, PAD), L)
        orow = pl.ds(pl.multiple_of(g * L, L), L)
        for s in range(LRU_SLABS):
            o_ref[orow, s * LANES:(s + 1) * LANES] = b_ref[s, rows, :] + b_ref[LRU_SLABS + s, rows, :]
        return carry

    lax.fori_loop(0, G, write_sequence, 0)


def _lru(proj, conv_w, conv_b, w_gates, b_gates, lam, h0, L, G):
    nseq = h0.shape[1]
    pitch = L + PAD
    full = lambda shape: pl.BlockSpec(shape, lambda s: (0,) * len(shape))
    return pl.pallas_call(
        functools.partial(_lru_kernel, L=L, G=G),
        grid=(nseq // G,),
        in_specs=[
            pl.BlockSpec((None, G * L, LRU_W), lambda s: (S_LXHV, s, 0)),
            full((4, LRU_W)),
            full((1, LRU_W)),
            full((LRU_W, 4 * LRU_W)),
            full((1, 4 * LRU_W)),
            full((2, LRU_W)),
            pl.BlockSpec((2, G, LRU_W), lambda s: (0, s, 0)),
        ],
        out_specs=[
            pl.BlockSpec((G * L, LRU_W), lambda s: (s, 0)),
            pl.BlockSpec((2, G, LRU_W), lambda s: (0, s, 0)),
        ],
        out_shape=[
            jax.ShapeDtypeStruct((nseq * L, LRU_W), F32),
            jax.ShapeDtypeStruct((2, nseq, LRU_W), F32),
        ],
        scratch_shapes=[
            pltpu.VMEM((L + 2 * PAD, LRU_W), F32),
            pltpu.VMEM((2 * LRU_SLABS, G * pitch, LANES), F32),
            pltpu.VMEM((2 * LRU_SLABS, G * pitch, LANES), F32),
        ],
        compiler_params=_params("arbitrary"),
        name=f"lru_{L}",
    )(proj, conv_w, conv_b, w_gates, b_gates, lam, h0)


def _lru_gate_weights(wa, wi, ba, bi):
    def dense(w):
        eye = jnp.eye(LRU_BLOCKS, dtype=w.dtype)
        return jnp.einsum("ncd,nm->ncmd", w, eye).reshape(LRU_W, LRU_W)

    w = jnp.concatenate([dense(wa[0]), dense(wi[0]), dense(wa[1]), dense(wi[1])], axis=1)
    b = jnp.concatenate([ba[0], bi[0], ba[1], bi[1]])[None, :]
    return w.astype(BF16), b


FILT_ROWS = 256


def _filt_kernel(z_ref, decay_ref, w1_ref, b1_ref, w2_ref, b2_ref, w3_ref, ft_ref, fb_ref, g_ref, taps_ref, *, L):
    i = pl.program_id(0)

    @pl.when(i == 0)
    def _():
        h = jnp.sin(_dot_hi(z_ref[...], w1_ref[...]) + b1_ref[...])
        h = jnp.sin(_dot_hi(h, w2_ref[...]) + b2_ref[...])
        filt = _dot_hi(h, w3_ref[...])
        decay = decay_ref[...]
        row = lax.broadcasted_iota(jnp.int32, (L, HY_W), 0)
        taps_ref[:, :HY_W] = (filt[:, :HY_W] * decay).astype(BF16)
        taps_ref[:, HY_W:] = jnp.where(row == 0, 0.0, filt[:, HY_W:] * decay).astype(BF16)

    taps = taps_ref[...]
    top = _dot(ft_ref[...], taps)
    bot = _dot(fb_ref[...], taps)
    first = (lax.broadcasted_iota(jnp.int32, (FILT_ROWS, HY_W), 0) + i * FILT_ROWS) == 0
    gt = top[:, :HY_W] + top[:, HY_W:]
    g_ref[0] = gt
    g_ref[1] = jnp.where(first, 0.0, bot[:, :HY_W] - bot[:, HY_W:])
    g_ref[2] = jnp.where(first, bot[:, :HY_W] + bot[:, HY_W:], gt)


def _filters(z, decay, w1, b1, w2, b2, w3, fwd_bf, L):
    full = lambda shape: pl.BlockSpec(shape, lambda i: (0,) * len(shape))
    nblk = L // FILT_ROWS
    return pl.pallas_call(
        functools.partial(_filt_kernel, L=L),
        grid=(nblk,),
        in_specs=[
            full((L, HY_POS)), full((L, HY_W)), full((HY_POS, HY_HIDDEN)), full((1, HY_HIDDEN)),
            full((HY_HIDDEN, HY_HIDDEN)), full((1, HY_HIDDEN)), full((HY_HIDDEN, 2 * HY_W)),
            pl.BlockSpec((FILT_ROWS, L), lambda i: (i, 0)),
            pl.BlockSpec((FILT_ROWS, L), lambda i: (nblk + i, 0)),
        ],
        out_specs=pl.BlockSpec((3, FILT_ROWS, HY_W), lambda i: (0, i, 0)),
        out_shape=jax.ShapeDtypeStruct((3, L, HY_W), F32),
        scratch_shapes=[pltpu.VMEM((L, 2 * HY_W), BF16)],
        compiler_params=_params("arbitrary"),
        name=f"hyena_filters_{L}",
    )(z, decay, w1, b1[None, :], w2, b2[None, :], w3, fwd_bf, fwd_bf)


HY_PARAM_ROWS = 16


def _hyena_kernel(hv_ref, hx1_ref, hx0_ref, prm_ref, g_ref, f_ref, fi_ref, o_ref, xpad_ref, *, L, ch):
    zeros = jnp.zeros((PAD, ch), F32)
    xpad_ref[0:PAD, :] = zeros
    xpad_ref[PAD + L:, :] = zeros
    prm = prm_ref[...]

    def conv(ref, stream):
        xpad_ref[PAD:PAD + L, :] = ref[...]
        u = prm[9 + stream:10 + stream] + prm[3 * stream:3 * stream + 1] * xpad_ref[PAD - 1:PAD - 1 + L, :]
        u = u + prm[3 * stream + 1:3 * stream + 2] * xpad_ref[PAD:PAD + L, :]
        return u + prm[3 * stream + 2:3 * stream + 3] * xpad_ref[PAD + 1:PAD + 1 + L, :]

    z = conv(hx1_ref, 1) * conv(hv_ref, 0)
    spec = _dot(f_ref[...], z.astype(BF16))
    top, bot = spec[:L], spec[L:]
    gx = g_ref[1]
    yt = top * g_ref[0] - bot * gx
    yb = top * gx + bot * g_ref[2]
    y = _dot(fi_ref[...], jnp.concatenate([yt, yb], axis=0).astype(BF16))
    o_ref[...] = conv(hx0_ref, 2) * (y + z * prm[12:13])


def _hyena(proj, prm, g, fwd, inv, L, nseq, ch):
    nch = HY_W // ch
    full = lambda shape: pl.BlockSpec(shape, lambda s, j: (0,) * len(shape))
    col = lambda slab, off: pl.BlockSpec((None, L, ch), lambda s, j, slab=slab, off=off: (slab, s, off * nch + j))
    return pl.pallas_call(
        functools.partial(_hyena_kernel, L=L, ch=ch),
        grid=(nseq, nch),
        in_specs=[
            col(S_LXHV, 1), col(S_HX, 0), col(S_HX, 1),
            pl.BlockSpec((HY_PARAM_ROWS, ch), lambda s, j: (0, j)),
            pl.BlockSpec((3, L, ch), lambda s, j: (0, 0, j)),
            full((2 * L, L)), full((L, 2 * L)),
        ],
        out_specs=pl.BlockSpec((L, ch), lambda s, j: (s, j)),
        out_shape=jax.ShapeDtypeStruct((nseq * L, HY_W), F32),
        scratch_shapes=[pltpu.VMEM((L + 2 * PAD, ch), F32)],
        compiler_params=_params("arbitrary", "arbitrary"),
        name=f"hyena_{L}",
    )(proj, proj, proj, prm, g, fwd, inv)


def _hyena_params(conv_w, conv_b, d):
    rows = [conv_w[:, s * HY_W:(s + 1) * HY_W] for s in range(3)]
    rows += [conv_b[None, s * HY_W:(s + 1) * HY_W] for s in range(3)]
    rows += [d[None, :], jnp.zeros((HY_PARAM_ROWS - 13, HY_W), F32)]
    return jnp.concatenate(rows, axis=0)


OUT_TM = 256


def _outproj_kernel(x_ref, gate_ref, att_ref, lru_ref, hy_ref, ag_ref, lg_ref, hg_ref, w_ref, fg_ref, o_ref, *, final):
    m_att = (att_ref[...] * _silu(ag_ref[...])).astype(BF16)
    m_lru = (lru_ref[...] * _silu(lg_ref[...])).astype(BF16)
    m_hy = (hy_ref[...] * _silu(hg_ref[...])).astype(BF16)
    acc = _dot(m_att, w_ref[0:ATT_W, :])
    acc = acc + _dot(m_lru, w_ref[ATT_W:ATT_W + LRU_W, :])
    acc = acc + _dot(m_hy, w_ref[ATT_W + LRU_W:, :])
    y = x_ref[...] + gate_ref[...] * acc
    if final:
        ms = jnp.mean(y * y, axis=-1, keepdims=True)
        y = (y * lax.rsqrt(ms + EPS)) * fg_ref[...]
    o_ref[...] = y


def _outproj(x, mod4, att, lru, hy, proj, w_out_bf, final_g, layer, final, seq_len):
    n_tok = x.shape[0]
    tm = OUT_TM
    cond = _cond_row(seq_len, tm)
    return pl.pallas_call(
        functools.partial(_outproj_kernel, final=final),
        grid=(n_tok // tm,),
        in_specs=[
            pl.BlockSpec((tm, D_MODEL), lambda i: (i, 0)),
            pl.BlockSpec((None, None, 1, D_MODEL), lambda i: (layer, cond(i), 0, 2)),
            pl.BlockSpec((tm, ATT_W), lambda i: (i, 0)),
            pl.BlockSpec((tm, LRU_W), lambda i: (i, 0)),
            pl.BlockSpec((tm, HY_W), lambda i: (i, 0)),
            pl.BlockSpec((None, tm, ATT_W), lambda i: (S_AG, i, 0)),
            pl.BlockSpec((None, tm, LRU_W), lambda i: (S_LGHG, i, 0)),
            pl.BlockSpec((None, tm, HY_W), lambda i: (S_LGHG, i, 1)),
            pl.BlockSpec((D_MODEL, D_MODEL), lambda i: (0, 0)),
            pl.BlockSpec((1, D_MODEL), lambda i: (0, 0)),
        ],
        out_specs=pl.BlockSpec((tm, D_MODEL), lambda i: (i, 0)),
        out_shape=jax.ShapeDtypeStruct((n_tok, D_MODEL), F32),
        compiler_params=_params("arbitrary"),
        name="outproj",
    )(x, mod4, att, lru, hy, proj, proj, proj, w_out_bf, final_g)


def _permute_w_in(w):
    a, l, h = ATT_W, LRU_W, HY_W
    o = 4 * a
    parts = [w[:, :o], w[:, o + l:o + 2 * l], w[:, o + 2 * l + 3 * h:], w[:, o:o + l], w[:, o + 2 * l:o + 2 * l + 3 * h]]
    return jnp.concatenate(parts, axis=1).astype(BF16)


def kernel(x_prompt, x_sample, cache_k, cache_v, state_lru, c, c_ctx, norm_g, w_ada, b_ada, w_in, w_out, lam_q1, lam_k1, lam_q2, lam_k2, attn_subln_g, lru_conv_w, lru_conv_b, lru_wa, lru_ba, lru_wi, lru_bi, lru_lam, hy_conv_w, hy_conv_b, hy_w1, hy_b1, hy_w2, hy_b2, hy_w3, hy_d, final_g):
    cos, sin = _rope_tables()
    tables = {}
    for L in (SEQ, DEC_SEQ):
        fwd, inv = _dft_tables(L)
        z, decay = _hyena_tables(L)
        tables[L] = (fwd.astype(BF16), inv.astype(BF16), z, decay)

    cond = jnp.concatenate([c_ctx[None, :], c, jnp.zeros((N_COND - 1 - DEC_BATCH, D_MODEL), F32)], axis=0)
    mod4 = _ada(cond, w_ada, b_ada).reshape(DEPTH, N_COND, 1, 3 * D_MODEL)

    xc = x_prompt.reshape(T_CTX, D_MODEL)
    xs = x_sample.reshape(T_SMP, D_MODEL)
    ck = cache_k.reshape(DEC_BATCH, DEPTH, PAST_LEN, ATT_W)
    cv = cache_v.reshape(DEC_BATCH, DEPTH, PAST_LEN, ATT_W)
    h0_ctx = jnp.zeros((2, BATCH, LRU_W), F32)

    assert DEPTH == 2
    caches, hs = None, []
    for l in range(DEPTH):
        final = l == DEPTH - 1
        lam_init = 0.8 - 0.6 * math.exp(-0.3 * l)
        lamp = jnp.stack([lam_q1[l], lam_k1[l], lam_q2[l], lam_k2[l]], axis=0)
        subln = attn_subln_g[l][None, :]
        w_in_p = _permute_w_in(w_in[l])
        w_out_bf = w_out[l].astype(BF16)
        wg, bg = _lru_gate_weights(lru_wa[l], lru_wi[l], lru_ba[l], lru_bi[l])
        lru_args = (lru_conv_w[l], lru_conv_b[l][None, :], wg, bg, lru_lam[l])
        hy_prm = _hyena_params(hy_conv_w[l], hy_conv_b[l], hy_d[l])
        filt_args = (hy_w1[l], hy_b1[l], hy_w2[l], hy_b2[l], hy_w3[l])

        fwd, inv, z, decay = tables[SEQ]
        proj, *caches = _inproj_cache(xc, norm_g[l][None, :], mod4, w_in_p, l, caches)
        att = _attn_ctx(proj, lamp, subln, lam_init)
        lru, h_last = _lru(proj, *lru_args, h0_ctx, SEQ, 8)
        hs.append(jnp.transpose(h_last, (1, 0, 2)))
        g = _filters(z, decay, *filt_args, fwd, SEQ)
        hy = _hyena(proj, hy_prm, g, fwd, inv, SEQ, BATCH, HY_W)
        xc = _outproj(xc, mod4, att, lru, hy, proj, w_out_bf, final_g[None, :], l, final, None)

        fwd, inv, z, decay = tables[DEC_SEQ]
        proj = _inproj(xs, norm_g[l][None, :], mod4, w_in_p, l, DEC_SEQ)
        att = _attn_smp(proj, ck, cv, lamp, subln, cos, sin, l, lam_init)
        lru, _ = _lru(proj, *lru_args, jnp.transpose(state_lru[:, l], (1, 0, 2)), DEC_SEQ, DEC_BATCH)
        g = _filters(z, decay, *filt_args, fwd, DEC_SEQ)
        hy = _hyena(proj, hy_prm, g, fwd, inv, DEC_SEQ, DEC_BATCH, HY_W // 2)
        xs = _outproj(xs, mod4, att, lru, hy, proj, w_out_bf, final_g[None, :], l, final, DEC_SEQ)

    y_prompt = xc.reshape(BATCH, SEQ, D_MODEL)
    y_sample = xs.reshape(DEC_BATCH, DEC_SEQ, D_MODEL)
    new_k, new_v = (a.reshape(BATCH, DEPTH, SEQ, N_HEADS, HEAD_W) for a in caches)
    return (y_prompt, y_sample, new_k, new_v, jnp.stack(hs, axis=1))
```

```python
import functools
import math

import numpy as np
import jax
import jax.numpy as jnp
from jax import lax
from jax.experimental import pallas as pl
from jax.experimental.pallas import tpu as pltpu

D_MODEL = 2048
BATCH = 32
SEQ = 256
DEPTH = 2
DEC_BATCH = 2
DEC_SEQ = 1024
PAST_LEN = 512
GRID_W = 64
ATT_W = 1024
LRU_W = 512
HY_W = 512
HEAD_DIM = 64
N_HEADS = 8
HEAD_W = 2 * HEAD_DIM
LRU_BLOCKS = 8
LRU_BW = LRU_W // LRU_BLOCKS
LRU_C = 8.0
HY_BANDS = 16
HY_POS = 1 + 2 * HY_BANDS
HY_HIDDEN = 64
HY_DECAY_FAST = 0.3
HY_DECAY_SLOW = 1.5
HY_DECAY_TARGET = 1e-2
ROPE_BASE = 10000.0
EPS = 1e-6
IN_W = 4 * ATT_W + 2 * LRU_W + 4 * HY_W

T_CTX = BATCH * SEQ
T_SMP = DEC_BATCH * DEC_SEQ
N_COND = 8

SLAB_W = 1024
N_SLABS = IN_W // SLAB_W
S_Q, S_K, S_V, S_AG, S_LGHG, S_LXHV, S_HX = range(N_SLABS)

LANES = 128
F32 = jnp.float32
BF16 = jnp.bfloat16
VMEM_LIMIT = 58 * 1024 * 1024


def _sigmoid(x):
    return 1.0 / (1.0 + jnp.exp(-x))


def _silu(x):
    return x * _sigmoid(x)


def _dot(a, b):
    return jnp.dot(a, b, preferred_element_type=F32)


def _dot_nt(a, b):
    return lax.dot_general(a, b, (((1,), (1,)), ((), ())), preferred_element_type=F32)


def _dot_hi(a, b):
    return jnp.dot(a, b, precision=lax.Precision.HIGHEST, preferred_element_type=F32)


def _params(*sem):
    return pltpu.CompilerParams(dimension_semantics=sem, vmem_limit_bytes=VMEM_LIMIT)


def _rope_tables():
    t = np.arange(DEC_SEQ)
    pos = np.stack([t // GRID_W, t % GRID_W], axis=1).astype(np.float64)
    nf = HEAD_DIM // 4
    inv = ROPE_BASE ** (-np.arange(nf, dtype=np.float64) / nf)
    lane = np.arange(HEAD_W)
    j = lane % HEAD_DIM
    axis = j // (HEAD_DIM // 2)
    f = j % nf
    upper = (j % (HEAD_DIM // 2)) >= nf
    ang = pos[:, axis] * inv[f][None, :]
    cos = np.cos(ang)
    sin = np.sin(ang) * np.where(upper, 1.0, -1.0)[None, :]
    return jnp.asarray(cos, F32), jnp.asarray(sin, F32)


def _dft_tables(L):
    n = 2 * L
    k = np.arange(L)[:, None]
    t = np.arange(L)[None, :]
    ang = 2.0 * np.pi * ((k * t) % n).astype(np.float64) / n
    fwd = np.concatenate([np.cos(ang), -np.sin(ang)], axis=0)
    fwd[L, :] = (-1.0) ** np.arange(L)
    wk = np.where(np.arange(L) == 0, 1.0, 2.0)[None, :]
    ang_t = ang.T
    inv = np.concatenate([wk * np.cos(ang_t), -2.0 * np.sin(ang_t)], axis=1) / n
    inv[:, L] = ((-1.0) ** np.arange(L)) / n
    return jnp.asarray(fwd, F32), jnp.asarray(inv, F32)


def _hyena_tables(L):
    pos = np.arange(L, dtype=np.float64)
    t = pos / float(max(L - 1, 1))
    bands = np.linspace(1e-4, HY_BANDS - 1, HY_BANDS)
    ang = (2.0 * math.pi / L) * pos[:, None] * bands[None, :]
    z = np.concatenate([t[:, None], np.cos(ang), np.sin(ang)], axis=-1)
    lo = abs(math.log(HY_DECAY_TARGET) / HY_DECAY_SLOW)
    hi = abs(math.log(HY_DECAY_TARGET) / HY_DECAY_FAST)
    deltas = np.linspace(lo, hi, HY_W)
    decay = np.exp(-t[:, None] * deltas[None, :])
    return jnp.asarray(z, F32), jnp.asarray(decay, F32)


def _ada_kernel(c_ref, w_ref, b_ref, o_ref):
    s = _silu(c_ref[...])
    o_ref[...] = _dot(s.astype(BF16), w_ref[...].astype(BF16)) + b_ref[...]


def _ada(cond, w_ada, b_ada):
    tn = 1024
    return pl.pallas_call(
        _ada_kernel,
        grid=(DEPTH, 3 * D_MODEL // tn),
        in_specs=[
            pl.BlockSpec((N_COND, D_MODEL), lambda l, j: (0, 0)),
            pl.BlockSpec((None, D_MODEL, tn), lambda l, j: (l, 0, j)),
            pl.BlockSpec((None, 1, tn), lambda l, j: (l, 0, j)),
        ],
        out_specs=pl.BlockSpec((None, N_COND, tn), lambda l, j: (l, 0, j)),
        out_shape=jax.ShapeDtypeStruct((DEPTH, N_COND, 3 * D_MODEL), F32),
        compiler_params=_params("arbitrary", "arbitrary"),
        name="ada",
    )(cond, w_ada, b_ada.reshape(DEPTH, 1, 3 * D_MODEL))


def _cond_row(seq_len, tm):
    if seq_len is None:
        return lambda i: 0
    return lambda i: 1 + i // (seq_len // tm)


IN_TM = 1024
NORM_ROWS = 32


def _norm_modulate(x_ref, g_ref, shift_ref, scale_ref, h_ref):
    g = g_ref[...]
    sc = 1.0 + scale_ref[...]
    sh = shift_ref[...]

    def body(r, carry):
        rows = pl.ds(pl.multiple_of(r * NORM_ROWS, NORM_ROWS), NORM_ROWS)
        x = x_ref[rows, :]
        ms = jnp.mean(x * x, axis=-1, keepdims=True)
        xn = x * lax.rsqrt(ms + EPS)
        h_ref[rows, :] = ((xn * g) * sc + sh).astype(BF16)
        return carry

    lax.fori_loop(0, IN_TM // NORM_ROWS, body, 0, unroll=4)


def _inproj_kernel(x_ref, g_ref, shift_ref, scale_ref, w_ref, o_ref, h_ref):
    @pl.when(pl.program_id(1) == 0)
    def _():
        _norm_modulate(x_ref, g_ref, shift_ref, scale_ref, h_ref)

    o_ref[...] = _dot(h_ref[...], w_ref[...])


def _inproj_cache_kernel(*refs, first_layer):
    if first_layer:
        x_ref, g_ref, shift_ref, scale_ref, w_ref, o_ref, kc_ref, vc_ref, h_ref = refs
    else:
        x_ref, g_ref, shift_ref, scale_ref, w_ref, _, _, o_ref, kc_ref, vc_ref, h_ref = refs
    j = pl.program_id(1)

    @pl.when(j == 0)
    def _():
        _norm_modulate(x_ref, g_ref, shift_ref, scale_ref, h_ref)

    o_ref[...] = _dot(h_ref[...], w_ref[...])

    def scatter(dst_ref):
        for b in range(IN_TM // SEQ):
            for h in range(N_HEADS):
                dst_ref[b, pl.ds(h, SEQ, stride=N_HEADS), :] = o_ref[b * SEQ:(b + 1) * SEQ, h * HEAD_W:(h + 1) * HEAD_W]

    @pl.when(j == S_K)
    def _():
        scatter(kc_ref)

    @pl.when(j == S_V)
    def _():
        scatter(vc_ref)

    if first_layer:
        @pl.when(j == S_K + 1)
        def _():
            kc_ref[...] = jnp.zeros(kc_ref.shape, F32)

        @pl.when(j == S_V + 1)
        def _():
            vc_ref[...] = jnp.zeros(vc_ref.shape, F32)


def _inproj_specs(tm, layer, cond):
    return [
        pl.BlockSpec((tm, D_MODEL), lambda i, j: (i, 0)),
        pl.BlockSpec((1, D_MODEL), lambda i, j: (0, 0)),
        pl.BlockSpec((None, None, 1, D_MODEL), lambda i, j: (layer, cond(i), 0, 0)),
        pl.BlockSpec((None, None, 1, D_MODEL), lambda i, j: (layer, cond(i), 0, 1)),
        pl.BlockSpec((D_MODEL, SLAB_W), lambda i, j: (0, j)),
    ]


def _inproj(x, norm_g, mod4, w_in_p, layer, seq_len):
    n_tok = x.shape[0]
    tm = IN_TM
    return pl.pallas_call(
        _inproj_kernel,
        grid=(n_tok // tm, N_SLABS),
        in_specs=_inproj_specs(tm, layer, _cond_row(seq_len, tm)),
        out_specs=pl.BlockSpec((None, tm, SLAB_W), lambda i, j: (j, i, 0)),
        out_shape=jax.ShapeDtypeStruct((N_SLABS, n_tok, SLAB_W), F32),
        scratch_shapes=[pltpu.VMEM((tm, D_MODEL), BF16)],
        compiler_params=_params("arbitrary", "arbitrary"),
        name="inproj",
    )(x, norm_g, mod4, mod4, w_in_p)


def _inproj_cache(x, norm_g, mod4, w_in_p, layer, caches):
    tm = IN_TM
    nb = tm // SEQ
    first = caches is None
    cache_shape = jax.ShapeDtypeStruct((BATCH, DEPTH, SEQ * N_HEADS, HEAD_W), F32)
    if first:
        half = lambda slab: (lambda i, j: (i, jnp.where(j <= slab, layer, layer + 1), 0, 0))
    else:
        half = lambda slab: (lambda i, j: (i, layer, 0, 0))
    cache_spec = lambda slab: pl.BlockSpec((nb, None, SEQ * N_HEADS, HEAD_W), half(slab))
    in_specs = _inproj_specs(tm, layer, _cond_row(None, tm))
    args = (x, norm_g, mod4, mod4, w_in_p)
    aliases = {}
    if not first:
        in_specs += [pl.BlockSpec(memory_space=pl.ANY)] * 2
        aliases = {len(args): 1, len(args) + 1: 2}
        args += tuple(caches)
    return pl.pallas_call(
        functools.partial(_inproj_cache_kernel, first_layer=first),
        grid=(T_CTX // tm, N_SLABS),
        in_specs=in_specs,
        out_specs=[pl.BlockSpec((None, tm, SLAB_W), lambda i, j: (j, i, 0)), cache_spec(S_K), cache_spec(S_V)],
        out_shape=[jax.ShapeDtypeStruct((N_SLABS, T_CTX, SLAB_W), F32), cache_shape, cache_shape],
        scratch_shapes=[pltpu.VMEM((tm, D_MODEL), BF16)],
        input_output_aliases=aliases,
        compiler_params=_params("arbitrary", "arbitrary"),
        name="inproj_cache",
    )(*args)


def _lam_value(lamp_ref, lam_init):
    lp = lamp_ref[...]
    t1 = jnp.sum(lp[0:1] * lp[1:2], axis=-1, keepdims=True)
    t2 = jnp.sum(lp[2:3] * lp[3:4], axis=-1, keepdims=True)
    return jnp.exp(t1) - jnp.exp(t2) + lam_init


def _diff_attend(items, lam, g, lam_init):
    m = items[0][0].shape[0]
    lane = lax.broadcasted_iota(jnp.int32, (m, HEAD_W), 1)
    ss = []
    for q, k_bf, _ in items:
        q1 = jnp.where(lane < HEAD_DIM, q, 0.0).astype(BF16)
        q2 = jnp.where(lane < HEAD_DIM, 0.0, q).astype(BF16)
        ss.append(_dot_nt(jnp.concatenate([q1, q2], axis=0), k_bf))
    es = [jnp.exp(s - jnp.max(s, axis=-1, keepdims=True)) for s in ss]
    ps = [e * (1.0 / jnp.sum(e, axis=-1, keepdims=True)) for e in es]
    ws = [(p[:m] - lam * p[m:]).astype(BF16) for p in ps]
    os_ = [_dot(w, v_bf) for w, (_, _, v_bf) in zip(ws, items)]
    outs = []
    for o in os_:
        ms = jnp.mean(o * o, axis=-1, keepdims=True)
        outs.append((o * lax.rsqrt(ms + EPS) * g) * (1.0 - lam_init))
    return outs


CTX_BATCHES = 2


def _attn_ctx_kernel(lamp_ref, g_ref, q_ref, k_ref, v_ref, ag_ref, o_ref, *, lam_init):
    lam = _lam_value(lamp_ref, lam_init)
    where = [(slice(b * SEQ, (b + 1) * SEQ), slice(h * HEAD_W, (h + 1) * HEAD_W))
             for b in range(CTX_BATCHES) for h in range(N_HEADS)]
    items = [(q_ref[r, c] * (HEAD_DIM ** -0.5), k_ref[r, c].astype(BF16), v_ref[r, c].astype(BF16)) for r, c in where]
    for (r, c), o in zip(where, _diff_attend(items, lam, g_ref[...], lam_init)):
        o_ref[r, c] = (o * _silu(ag_ref[r, c])).astype(BF16)


def _attn_ctx(proj, lamp, subln_g, lam_init):
    rows = CTX_BATCHES * SEQ
    slab = lambda s: pl.BlockSpec((None, rows, ATT_W), lambda b, s=s: (s, b, 0))
    return pl.pallas_call(
        functools.partial(_attn_ctx_kernel, lam_init=lam_init),
        grid=(BATCH // CTX_BATCHES,),
        in_specs=[
            pl.BlockSpec((4, HEAD_DIM), lambda b: (0, 0)),
            pl.BlockSpec((1, HEAD_W), lambda b: (0, 0)),
            slab(S_Q), slab(S_K), slab(S_V), slab(S_AG),
        ],
        out_specs=pl.BlockSpec((rows, ATT_W), lambda b: (b, 0)),
        out_shape=jax.ShapeDtypeStruct((T_CTX, ATT_W), BF16),
        compiler_params=_params("arbitrary"),
        name="attn_ctx",
    )(lamp, subln_g, proj, proj, proj, proj)


Q_CHUNK = 512
Q_PART = 256
N_QC = DEC_SEQ // Q_CHUNK


def _rope(x, cos, sin):
    lane = lax.broadcasted_iota(jnp.int32, x.shape, 1)
    lower = (lane % (HEAD_DIM // 2)) < (HEAD_DIM // 4)
    nf = HEAD_DIM // 4
    partner = jnp.where(lower, pltpu.roll(x, HEAD_W - nf, 1), pltpu.roll(x, nf, 1))
    return x * cos + partner * sin


def _attn_smp_kernel(lamp_ref, g_ref, cosq_ref, sinq_ref, cos_ref, sin_ref, q_ref, k_ref, v_ref,
                     ck_ref, cv_ref, ag_ref, o_ref, ks_ref, vs_ref, *, lam_init):
    @pl.when(pl.program_id(2) == 0)
    def _():
        ks_ref[0:PAST_LEN, :] = ck_ref[...].astype(BF16)
        vs_ref[0:PAST_LEN, :] = cv_ref[...].astype(BF16)
        ks_ref[PAST_LEN:, :] = _rope(k_ref[...], cos_ref[...], sin_ref[...]).astype(BF16)
        vs_ref[PAST_LEN:, :] = v_ref[...].astype(BF16)

    lam = _lam_value(lamp_ref, lam_init)
    q = _rope(q_ref[...], cosq_ref[...], sinq_ref[...]) * (HEAD_DIM ** -0.5)
    k_bf, v_bf = ks_ref[...], vs_ref[...]
    parts = [slice(r, r + Q_PART) for r in range(0, Q_CHUNK, Q_PART)]
    items = [(q[p], k_bf, v_bf) for p in parts]
    for p, o in zip(parts, _diff_attend(items, lam, g_ref[...], lam_init)):
        o_ref[p, :] = (o * _silu(ag_ref[p, :])).astype(BF16)


def _attn_smp(proj, cache_k, cache_v, lamp, subln_g, cos, sin, layer, lam_init):
    full = lambda shape: pl.BlockSpec(shape, lambda b, h, c: (0, 0))
    return pl.pallas_call(
        functools.partial(_attn_smp_kernel, lam_init=lam_init),
        grid=(DEC_BATCH, N_HEADS, N_QC),
        in_specs=[
            full((4, HEAD_DIM)),
            full((1, HEAD_W)),
            pl.BlockSpec((Q_CHUNK, HEAD_W), lambda b, h, c: (c, 0)),
            pl.BlockSpec((Q_CHUNK, HEAD_W), lambda b, h, c: (c, 0)),
            full((DEC_SEQ, HEAD_W)),
            full((DEC_SEQ, HEAD_W)),
            pl.BlockSpec((None, Q_CHUNK, HEAD_W), lambda b, h, c: (S_Q, b * N_QC + c, h)),
            pl.BlockSpec((None, DEC_SEQ, HEAD_W), lambda b, h, c: (S_K, b, h)),
            pl.BlockSpec((None, DEC_SEQ, HEAD_W), lambda b, h, c: (S_V, b, h)),
            pl.BlockSpec((None, None, PAST_LEN, HEAD_W), lambda b, h, c: (b, layer, 0, h)),
            pl.BlockSpec((None, None, PAST_LEN, HEAD_W), lambda b, h, c: (b, layer, 0, h)),
            pl.BlockSpec((None, Q_CHUNK, HEAD_W), lambda b, h, c: (S_AG, b * N_QC + c, h)),
        ],
        out_specs=pl.BlockSpec((Q_CHUNK, HEAD_W), lambda b, h, c: (b * N_QC + c, h)),
        out_shape=jax.ShapeDtypeStruct((T_SMP, ATT_W), BF16),
        scratch_shapes=[
            pltpu.VMEM((PAST_LEN + DEC_SEQ, HEAD_W), BF16),
            pltpu.VMEM((PAST_LEN + DEC_SEQ, HEAD_W), BF16),
        ],
        compiler_params=_params("arbitrary", "arbitrary", "arbitrary"),
        name="attn_smp",
    )(lamp, subln_g, cos, sin, cos, sin, proj, proj, proj, cache_k, cache_v, proj)


PAD = 8
GATE_ROWS = 256
LRU_SLABS = LRU_W // LANES


def _lru_kernel(x_ref, lg_ref, cw_ref, cb_ref, w_ref, gb_ref, lam_ref, h0_ref, o_ref, hl_ref,
                xpad_ref, a_ref, b_ref, *, L, G):
    pitch = L + PAD
    zeros = jnp.zeros((PAD, LRU_W), F32)
    xpad_ref[0:PAD, :] = zeros
    xpad_ref[PAD + L:, :] = zeros
    nl = -lam_ref[...]
    neg_c = (-LRU_C) * (jnp.maximum(nl, 0.0) + jnp.log1p(jnp.exp(-jnp.abs(nl))))
    cw = cw_ref[...]
    cb = cb_ref[...]
    gb = gb_ref[...]

    def gates_of_sequence(g, carry):
        xpad_ref[PAD:PAD + L, :] = x_ref[pl.ds(pl.multiple_of(g * L, L), L), :]
        base = pl.multiple_of(g * pitch, PAD)
        for c in range(L // GATE_ROWS):
            r0 = c * GATE_ROWS
            xc = cb + cw[0:1] * xpad_ref[PAD - 2 + r0:PAD - 2 + r0 + GATE_ROWS, :]
            xc = xc + cw[1:2] * xpad_ref[PAD - 1 + r0:PAD - 1 + r0 + GATE_ROWS, :]
            xc = xc + cw[2:3] * xpad_ref[PAD + r0:PAD + r0 + GATE_ROWS, :]
            xc = xc + cw[3:4] * xpad_ref[PAD + 1 + r0:PAD + 1 + r0 + GATE_ROWS, :]
            gates = _dot(xc.astype(BF16), w_ref[...]) + gb
            for d in range(2):
                r = _sigmoid(gates[:, (2 * d) * LRU_W:(2 * d + 1) * LRU_W])
                i = _sigmoid(gates[:, (2 * d + 1) * LRU_W:(2 * d + 2) * LRU_W])
                a = jnp.exp(r * neg_c[d:d + 1])
                a2 = 1.0 - a * a
                b = jnp.where(a2 > 0.0, a2 * lax.rsqrt(a2), 0.0) * (i * xc)
                rows = pl.ds(base + r0, GATE_ROWS)
                for s in range(LRU_SLABS):
                    a_ref[d * LRU_SLABS + s, rows, :] = a[:, s * LANES:(s + 1) * LANES]
                    b_ref[d * LRU_SLABS + s, rows, :] = b[:, s * LANES:(s + 1) * LANES]
        return carry

    lax.fori_loop(0, G, gates_of_sequence, 0)

    def step(t, hs):
        out = []
        for d in range(2):
            tt = t if d == 0 else L - 1 - t
            rows = pl.ds(tt, G, stride=pitch)
            for s in range(LRU_SLABS):
                k = d * LRU_SLABS + s
                h = a_ref[k, rows, :] * hs[k] + b_ref[k, rows, :]
                b_ref[k, rows, :] = h
                out.append(h)
        return tuple(out)

    h0 = tuple(h0_ref[d, :, s * LANES:(s + 1) * LANES] for d in range(2) for s in range(LRU_SLABS))
    hs = lax.fori_loop(0, L, step, h0, unroll=2)
    for d in range(2):
        for s in range(LRU_SLABS):
            hl_ref[d, :, s * LANES:(s + 1) * LANES] = hs[d * LRU_SLABS + s]

    def write_sequence(g, carry):
        rows = pl.ds(pl.multiple_of(g * pitch, PAD), L)
        orow = pl.ds(pl.multiple_of(g * L, L), L)
        for s in range(LRU_SLABS):
            lanes = slice(s * LANES, (s + 1) * LANES)
            h = b_ref[s, rows, :] + b_ref[LRU_SLABS + s, rows, :]
            o_ref[orow, lanes] = (h * _silu(lg_ref[orow, lanes])).astype(BF16)
        return carry

    lax.fori_loop(0, G, write_sequence, 0)


def _lru(proj, conv_w, conv_b, w_gates, b_gates, lam, h0, L, G):
    nseq = h0.shape[1]
    pitch = L + PAD
    full = lambda shape: pl.BlockSpec(shape, lambda s: (0,) * len(shape))
    return pl.pallas_call(
        functools.partial(_lru_kernel, L=L, G=G),
        grid=(nseq // G,),
        in_specs=[
            pl.BlockSpec((None, G * L, LRU_W), lambda s: (S_LXHV, s, 0)),
            pl.BlockSpec((None, G * L, LRU_W), lambda s: (S_LGHG, s, 0)),
            full((4, LRU_W)),
            full((1, LRU_W)),
            full((LRU_W, 4 * LRU_W)),
            full((1, 4 * LRU_W)),
            full((2, LRU_W)),
            pl.BlockSpec((2, G, LRU_W), lambda s: (0, s, 0)),
        ],
        out_specs=[
            pl.BlockSpec((G * L, LRU_W), lambda s: (s, 0)),
            pl.BlockSpec((2, G, LRU_W), lambda s: (0, s, 0)),
        ],
        out_shape=[
            jax.ShapeDtypeStruct((nseq * L, LRU_W), BF16),
            jax.ShapeDtypeStruct((2, nseq, LRU_W), F32),
        ],
        scratch_shapes=[
            pltpu.VMEM((L + 2 * PAD, LRU_W), F32),
            pltpu.VMEM((2 * LRU_SLABS, G * pitch, LANES), F32),
            pltpu.VMEM((2 * LRU_SLABS, G * pitch, LANES), F32),
        ],
        compiler_params=_params("arbitrary"),
        name=f"lru_{L}",
    )(proj, proj, conv_w, conv_b, w_gates, b_gates, lam, h0)


def _lru_gate_weights(wa, wi, ba, bi):
    def dense(w):
        eye = jnp.eye(LRU_BLOCKS, dtype=w.dtype)
        return jnp.einsum("ncd,nm->ncmd", w, eye).reshape(LRU_W, LRU_W)

    w = jnp.concatenate([dense(wa[0]), dense(wi[0]), dense(wa[1]), dense(wi[1])], axis=1)
    b = jnp.concatenate([ba[0], bi[0], ba[1], bi[1]])[None, :]
    return w.astype(BF16), b


FILT_ROWS = 256


def _filt_kernel(z_ref, decay_ref, w1_ref, b1_ref, w2_ref, b2_ref, w3_ref, ft_ref, fb_ref, g_ref, taps_ref, *, L):
    i = pl.program_id(0)

    @pl.when(i == 0)
    def _():
        h = jnp.sin(_dot_hi(z_ref[...], w1_ref[...]) + b1_ref[...])
        h = jnp.sin(_dot_hi(h, w2_ref[...]) + b2_ref[...])
        filt = _dot_hi(h, w3_ref[...])
        decay = decay_ref[...]
        row = lax.broadcasted_iota(jnp.int32, (L, HY_W), 0)
        taps_ref[:, :HY_W] = (filt[:, :HY_W] * decay).astype(BF16)
        taps_ref[:, HY_W:] = jnp.where(row == 0, 0.0, filt[:, HY_W:] * decay).astype(BF16)

    taps = taps_ref[...]
    top = _dot(ft_ref[...], taps)
    bot = _dot(fb_ref[...], taps)
    first = (lax.broadcasted_iota(jnp.int32, (FILT_ROWS, HY_W), 0) + i * FILT_ROWS) == 0
    gt = top[:, :HY_W] + top[:, HY_W:]
    g_ref[0] = gt
    g_ref[1] = jnp.where(first, 0.0, bot[:, :HY_W] - bot[:, HY_W:])
    g_ref[2] = jnp.where(first, bot[:, :HY_W] + bot[:, HY_W:], gt)


def _filters(z, decay, w1, b1, w2, b2, w3, fwd_bf, L):
    full = lambda shape: pl.BlockSpec(shape, lambda i: (0,) * len(shape))
    nblk = L // FILT_ROWS
    return pl.pallas_call(
        functools.partial(_filt_kernel, L=L),
        grid=(nblk,),
        in_specs=[
            full((L, HY_POS)), full((L, HY_W)), full((HY_POS, HY_HIDDEN)), full((1, HY_HIDDEN)),
            full((HY_HIDDEN, HY_HIDDEN)), full((1, HY_HIDDEN)), full((HY_HIDDEN, 2 * HY_W)),
            pl.BlockSpec((FILT_ROWS, L), lambda i: (i, 0)),
            pl.BlockSpec((FILT_ROWS, L), lambda i: (nblk + i, 0)),
        ],
        out_specs=pl.BlockSpec((3, FILT_ROWS, HY_W), lambda i: (0, i, 0)),
        out_shape=jax.ShapeDtypeStruct((3, L, HY_W), F32),
        scratch_shapes=[pltpu.VMEM((L, 2 * HY_W), BF16)],
        compiler_params=_params("arbitrary"),
        name=f"hyena_filters_{L}",
    )(z, decay, w1, b1[None, :], w2, b2[None, :], w3, fwd_bf, fwd_bf)


HY_PARAM_ROWS = 16


def _hyena_kernel(hv_ref, hx1_ref, hx0_ref, hg_ref, prm_ref, g_ref, f_ref, fi_ref, o_ref, xpad_ref, *, L, ch):
    zeros = jnp.zeros((PAD, ch), F32)
    xpad_ref[0:PAD, :] = zeros
    xpad_ref[PAD + L:, :] = zeros
    prm = prm_ref[...]

    def conv(ref, stream):
        xpad_ref[PAD:PAD + L, :] = ref[...]
        u = prm[9 + stream:10 + stream] + prm[3 * stream:3 * stream + 1] * xpad_ref[PAD - 1:PAD - 1 + L, :]
        u = u + prm[3 * stream + 1:3 * stream + 2] * xpad_ref[PAD:PAD + L, :]
        return u + prm[3 * stream + 2:3 * stream + 3] * xpad_ref[PAD + 1:PAD + 1 + L, :]

    z = conv(hx1_ref, 1) * conv(hv_ref, 0)
    spec = _dot(f_ref[...], z.astype(BF16))
    top, bot = spec[:L], spec[L:]
    gx = g_ref[1]
    yt = top * g_ref[0] - bot * gx
    yb = top * gx + bot * g_ref[2]
    y = _dot(fi_ref[...], jnp.concatenate([yt, yb], axis=0).astype(BF16))
    hy = conv(hx0_ref, 2) * (y + z * prm[12:13])
    o_ref[...] = (hy * _silu(hg_ref[...])).astype(BF16)


def _hyena(proj, prm, g, fwd, inv, L, nseq, ch):
    nch = HY_W // ch
    full = lambda shape: pl.BlockSpec(shape, lambda s, j: (0,) * len(shape))
    col = lambda slab, off: pl.BlockSpec((None, L, ch), lambda s, j, slab=slab, off=off: (slab, s, off * nch + j))
    return pl.pallas_call(
        functools.partial(_hyena_kernel, L=L, ch=ch),
        grid=(nseq, nch),
        in_specs=[
            col(S_LXHV, 1), col(S_HX, 0), col(S_HX, 1), col(S_LGHG, 1),
            pl.BlockSpec((HY_PARAM_ROWS, ch), lambda s, j: (0, j)),
            pl.BlockSpec((3, L, ch), lambda s, j: (0, 0, j)),
            full((2 * L, L)), full((L, 2 * L)),
        ],
        out_specs=pl.BlockSpec((L, ch), lambda s, j: (s, j)),
        out_shape=jax.ShapeDtypeStruct((nseq * L, HY_W), BF16),
        scratch_shapes=[pltpu.VMEM((L + 2 * PAD, ch), F32)],
        compiler_params=_params("arbitrary", "arbitrary"),
        name=f"hyena_{L}",
    )(proj, proj, proj, proj, prm, g, fwd, inv)


def _hyena_params(conv_w, conv_b, d):
    rows = [conv_w[:, s * HY_W:(s + 1) * HY_W] for s in range(3)]
    rows += [conv_b[None, s * HY_W:(s + 1) * HY_W] for s in range(3)]
    rows += [d[None, :], jnp.zeros((HY_PARAM_ROWS - 13, HY_W), F32)]
    return jnp.concatenate(rows, axis=0)


OUT_TM = 512


def _outproj_kernel(x_ref, gate_ref, att_ref, lru_ref, hy_ref, w_ref, fg_ref, o_ref, *, final):
    acc = _dot(att_ref[...], w_ref[0:ATT_W, :])
    acc = acc + _dot(lru_ref[...], w_ref[ATT_W:ATT_W + LRU_W, :])
    acc = acc + _dot(hy_ref[...], w_ref[ATT_W + LRU_W:, :])
    y = x_ref[...] + gate_ref[...] * acc
    if final:
        ms = jnp.mean(y * y, axis=-1, keepdims=True)
        y = (y * lax.rsqrt(ms + EPS)) * fg_ref[...]
    o_ref[...] = y


def _outproj(x, mod4, att, lru, hy, w_out_bf, final_g, layer, final, seq_len):
    n_tok = x.shape[0]
    tm = OUT_TM
    cond = _cond_row(seq_len, tm)
    return pl.pallas_call(
        functools.partial(_outproj_kernel, final=final),
        grid=(n_tok // tm,),
        in_specs=[
            pl.BlockSpec((tm, D_MODEL), lambda i: (i, 0)),
            pl.BlockSpec((None, None, 1, D_MODEL), lambda i: (layer, cond(i), 0, 2)),
            pl.BlockSpec((tm, ATT_W), lambda i: (i, 0)),
            pl.BlockSpec((tm, LRU_W), lambda i: (i, 0)),
            pl.BlockSpec((tm, HY_W), lambda i: (i, 0)),
            pl.BlockSpec((D_MODEL, D_MODEL), lambda i: (0, 0), pipeline_mode=pl.Buffered(1)),
            pl.BlockSpec((1, D_MODEL), lambda i: (0, 0)),
        ],
        out_specs=pl.BlockSpec((tm, D_MODEL), lambda i: (i, 0)),
        out_shape=jax.ShapeDtypeStruct((n_tok, D_MODEL), F32),
        compiler_params=_params("arbitrary"),
        name="outproj",
    )(x, mod4, att, lru, hy, w_out_bf, final_g)


def _permute_w_in(w):
    a, l, h = ATT_W, LRU_W, HY_W
    o = 4 * a
    parts = [w[:, :o], w[:, o + l:o + 2 * l], w[:, o + 2 * l + 3 * h:], w[:, o:o + l], w[:, o + 2 * l:o + 2 * l + 3 * h]]
    return jnp.concatenate(parts, axis=1).astype(BF16)


def kernel(x_prompt, x_sample, cache_k, cache_v, state_lru, c, c_ctx, norm_g, w_ada, b_ada, w_in, w_out, lam_q1, lam_k1, lam_q2, lam_k2, attn_subln_g, lru_conv_w, lru_conv_b, lru_wa, lru_ba, lru_wi, lru_bi, lru_lam, hy_conv_w, hy_conv_b, hy_w1, hy_b1, hy_w2, hy_b2, hy_w3, hy_d, final_g):
    cos, sin = _rope_tables()
    tables = {}
    for L in (SEQ, DEC_SEQ):
        fwd, inv = _dft_tables(L)
        z, decay = _hyena_tables(L)
        tables[L] = (fwd.astype(BF16), inv.astype(BF16), z, decay)

    cond = jnp.concatenate([c_ctx[None, :], c, jnp.zeros((N_COND - 1 - DEC_BATCH, D_MODEL), F32)], axis=0)
    mod4 = _ada(cond, w_ada, b_ada).reshape(DEPTH, N_COND, 1, 3 * D_MODEL)

    xc = x_prompt.reshape(T_CTX, D_MODEL)
    xs = x_sample.reshape(T_SMP, D_MODEL)
    ck = cache_k.reshape(DEC_BATCH, DEPTH, PAST_LEN, ATT_W)
    cv = cache_v.reshape(DEC_BATCH, DEPTH, PAST_LEN, ATT_W)
    h0_ctx = jnp.zeros((2, BATCH, LRU_W), F32)

    assert DEPTH == 2
    caches, hs = None, []
    for l in range(DEPTH):
        final = l == DEPTH - 1
        lam_init = 0.8 - 0.6 * math.exp(-0.3 * l)
        lamp = jnp.stack([lam_q1[l], lam_k1[l], lam_q2[l], lam_k2[l]], axis=0)
        subln = attn_subln_g[l][None, :]
        w_in_p = _permute_w_in(w_in[l])
        w_out_bf = w_out[l].astype(BF16)
        wg, bg = _lru_gate_weights(lru_wa[l], lru_wi[l], lru_ba[l], lru_bi[l])
        lru_args = (lru_conv_w[l], lru_conv_b[l][None, :], wg, bg, lru_lam[l])
        hy_prm = _hyena_params(hy_conv_w[l], hy_conv_b[l], hy_d[l])
        filt_args = (hy_w1[l], hy_b1[l], hy_w2[l], hy_b2[l], hy_w3[l])

        fwd, inv, z, decay = tables[SEQ]
        proj, *caches = _inproj_cache(xc, norm_g[l][None, :], mod4, w_in_p, l, caches)
        att = _attn_ctx(proj, lamp, subln, lam_init)
        lru, h_last = _lru(proj, *lru_args, h0_ctx, SEQ, 8)
        hs.append(jnp.transpose(h_last, (1, 0, 2)))
        g = _filters(z, decay, *filt_args, fwd, SEQ)
        hy = _hyena(proj, hy_prm, g, fwd, inv, SEQ, BATCH, HY_W)
        xc = _outproj(xc, mod4, att, lru, hy, w_out_bf, final_g[None, :], l, final, None)

        fwd, inv, z, decay = tables[DEC_SEQ]
        proj = _inproj(xs, norm_g[l][None, :], mod4, w_in_p, l, DEC_SEQ)
        att = _attn_smp(proj, ck, cv, lamp, subln, cos, sin, l, lam_init)
        lru, _ = _lru(proj, *lru_args, jnp.transpose(state_lru[:, l], (1, 0, 2)), DEC_SEQ, DEC_BATCH)
        g = _filters(z, decay, *filt_args, fwd, DEC_SEQ)
        hy = _hyena(proj, hy_prm, g, fwd, inv, DEC_SEQ, DEC_BATCH, HY_W // 2)
        xs = _outproj(xs, mod4, att, lru, hy, w_out_bf, final_g[None, :], l, final, DEC_SEQ)

    y_prompt = xc.reshape(BATCH, SEQ, D_MODEL)
    y_sample = xs.reshape(DEC_BATCH, DEC_SEQ, D_MODEL)
    new_k, new_v = (a.reshape(BATCH, DEPTH, SEQ, N_HEADS, HEAD_W) for a in caches)
    return (y_prompt, y_sample, new_k, new_v, jnp.stack(hs, axis=1))
```

```python
import functools
import math

import numpy as np
import jax
import jax.numpy as jnp
from jax import lax
from jax.experimental import pallas as pl
from jax.experimental.pallas import tpu as pltpu

D_MODEL = 2048
BATCH = 32
SEQ = 256
DEPTH = 2
DEC_BATCH = 2
DEC_SEQ = 1024
PAST_LEN = 512
GRID_W = 64
ATT_W = 1024
LRU_W = 512
HY_W = 512
HEAD_DIM = 64
N_HEADS = 8
HEAD_W = 2 * HEAD_DIM
LRU_BLOCKS = 8
LRU_BW = LRU_W // LRU_BLOCKS
LRU_C = 8.0
HY_BANDS = 16
HY_POS = 1 + 2 * HY_BANDS
HY_HIDDEN = 64
HY_DECAY_FAST = 0.3
HY_DECAY_SLOW = 1.5
HY_DECAY_TARGET = 1e-2
ROPE_BASE = 10000.0
EPS = 1e-6
IN_W = 4 * ATT_W + 2 * LRU_W + 4 * HY_W

T_CTX = BATCH * SEQ
T_SMP = DEC_BATCH * DEC_SEQ
N_COND = 8

SLAB_W = 1024
N_SLABS = IN_W // SLAB_W
S_Q, S_K, S_V, S_AG, S_LGHG, S_LXHV, S_HX = range(N_SLABS)

LANES = 128
F32 = jnp.float32
BF16 = jnp.bfloat16
VMEM_LIMIT = 58 * 1024 * 1024


LOG2E = 1.4426950408889634


def _sigmoid(x):
    return 1.0 / (1.0 + jnp.exp2(x * (-LOG2E)))


def _silu(x):
    return x * _sigmoid(x)


def _dot(a, b):
    return jnp.dot(a, b, preferred_element_type=F32)


def _dot_nt(a, b):
    return lax.dot_general(a, b, (((1,), (1,)), ((), ())), preferred_element_type=F32)


def _dot_hi(a, b):
    return jnp.dot(a, b, precision=lax.Precision.HIGHEST, preferred_element_type=F32)


def _params(*sem):
    return pltpu.CompilerParams(dimension_semantics=sem, vmem_limit_bytes=VMEM_LIMIT)


def _rope_tables():
    t = np.arange(DEC_SEQ)
    pos = np.stack([t // GRID_W, t % GRID_W], axis=1).astype(np.float64)
    nf = HEAD_DIM // 4
    inv = ROPE_BASE ** (-np.arange(nf, dtype=np.float64) / nf)
    lane = np.arange(HEAD_W)
    j = lane % HEAD_DIM
    axis = j // (HEAD_DIM // 2)
    f = j % nf
    upper = (j % (HEAD_DIM // 2)) >= nf
    ang = pos[:, axis] * inv[f][None, :]
    cos = np.cos(ang)
    sin = np.sin(ang) * np.where(upper, 1.0, -1.0)[None, :]
    return jnp.asarray(cos, F32), jnp.asarray(sin, F32)


def _dft_tables(L):
    n = 2 * L
    k = np.arange(L)[:, None]
    t = np.arange(L)[None, :]
    ang = 2.0 * np.pi * ((k * t) % n).astype(np.float64) / n
    fwd = np.concatenate([np.cos(ang), -np.sin(ang)], axis=0)
    fwd[L, :] = (-1.0) ** np.arange(L)
    wk = np.where(np.arange(L) == 0, 1.0, 2.0)[None, :]
    ang_t = ang.T
    inv = np.concatenate([wk * np.cos(ang_t), -2.0 * np.sin(ang_t)], axis=1) / n
    inv[:, L] = ((-1.0) ** np.arange(L)) / n
    return jnp.asarray(fwd, F32), jnp.asarray(inv, F32)


def _hyena_tables(L):
    pos = np.arange(L, dtype=np.float64)
    t = pos / float(max(L - 1, 1))
    bands = np.linspace(1e-4, HY_BANDS - 1, HY_BANDS)
    ang = (2.0 * math.pi / L) * pos[:, None] * bands[None, :]
    z = np.concatenate([t[:, None], np.cos(ang), np.sin(ang)], axis=-1)
    lo = abs(math.log(HY_DECAY_TARGET) / HY_DECAY_SLOW)
    hi = abs(math.log(HY_DECAY_TARGET) / HY_DECAY_FAST)
    deltas = np.linspace(lo, hi, HY_W)
    decay = np.exp(-t[:, None] * deltas[None, :])
    return jnp.asarray(z, F32), jnp.asarray(decay, F32)


def _ada_kernel(c_ref, w_ref, b_ref, o_ref):
    s = _silu(c_ref[...])
    o_ref[...] = _dot(s.astype(BF16), w_ref[...].astype(BF16)) + b_ref[...]


def _ada(cond, w_ada, b_ada):
    tn = 1024
    return pl.pallas_call(
        _ada_kernel,
        grid=(DEPTH, 3 * D_MODEL // tn),
        in_specs=[
            pl.BlockSpec((N_COND, D_MODEL), lambda l, j: (0, 0)),
            pl.BlockSpec((None, D_MODEL, tn), lambda l, j: (l, 0, j)),
            pl.BlockSpec((None, 1, tn), lambda l, j: (l, 0, j)),
        ],
        out_specs=pl.BlockSpec((None, N_COND, tn), lambda l, j: (l, 0, j)),
        out_shape=jax.ShapeDtypeStruct((DEPTH, N_COND, 3 * D_MODEL), F32),
        compiler_params=_params("arbitrary", "arbitrary"),
        name="ada",
    )(cond, w_ada, b_ada.reshape(DEPTH, 1, 3 * D_MODEL))


def _cond_row(seq_len, tm):
    if seq_len is None:
        return lambda i: 0
    return lambda i: 1 + i // (seq_len // tm)


IN_TM = 1024
NORM_ROWS = 32


def _norm_modulate(x_ref, g_ref, shift_ref, scale_ref, h_ref):
    g = g_ref[...]
    sc = 1.0 + scale_ref[...]
    sh = shift_ref[...]

    def body(r, carry):
        rows = pl.ds(pl.multiple_of(r * NORM_ROWS, NORM_ROWS), NORM_ROWS)
        x = x_ref[rows, :]
        ms = jnp.mean(x * x, axis=-1, keepdims=True)
        xn = x * lax.rsqrt(ms + EPS)
        h_ref[rows, :] = ((xn * g) * sc + sh).astype(BF16)
        return carry

    lax.fori_loop(0, IN_TM // NORM_ROWS, body, 0, unroll=4)


def _inproj_kernel(x_ref, g_ref, shift_ref, scale_ref, w_ref, o_ref, h_ref):
    @pl.when(pl.program_id(1) == 0)
    def _():
        _norm_modulate(x_ref, g_ref, shift_ref, scale_ref, h_ref)

    o_ref[...] = _dot(h_ref[...], w_ref[...])


def _inproj_cache_kernel(*refs, first_layer):
    if first_layer:
        x_ref, g_ref, shift_ref, scale_ref, w_ref, o_ref, kc_ref, vc_ref, h_ref = refs
    else:
        x_ref, g_ref, shift_ref, scale_ref, w_ref, _, _, o_ref, kc_ref, vc_ref, h_ref = refs
    j = pl.program_id(1)

    @pl.when(j == 0)
    def _():
        _norm_modulate(x_ref, g_ref, shift_ref, scale_ref, h_ref)

    o_ref[...] = _dot(h_ref[...], w_ref[...])

    def scatter(dst_ref):
        for b in range(IN_TM // SEQ):
            for h in range(N_HEADS):
                dst_ref[b, pl.ds(h, SEQ, stride=N_HEADS), :] = o_ref[b * SEQ:(b + 1) * SEQ, h * HEAD_W:(h + 1) * HEAD_W]

    @pl.when(j == S_K)
    def _():
        scatter(kc_ref)

    @pl.when(j == S_V)
    def _():
        scatter(vc_ref)

    if first_layer:
        @pl.when(j == S_K + 1)
        def _():
            kc_ref[...] = jnp.zeros(kc_ref.shape, F32)

        @pl.when(j == S_V + 1)
        def _():
            vc_ref[...] = jnp.zeros(vc_ref.shape, F32)


def _of_layer(layer, shape):
    return pl.BlockSpec((None,) + tuple(shape), lambda *_: (layer,) + (0,) * len(shape))


def _inproj_specs(tm, layer, cond):
    return [
        pl.BlockSpec((tm, D_MODEL), lambda i, j: (i, 0)),
        _of_layer(layer, (1, D_MODEL)),
        pl.BlockSpec((None, None, 1, D_MODEL), lambda i, j: (layer, cond(i), 0, 0)),
        pl.BlockSpec((None, None, 1, D_MODEL), lambda i, j: (layer, cond(i), 0, 1)),
        pl.BlockSpec((None, None, D_MODEL, SLAB_W), lambda i, j: (layer, j, 0, 0)),
    ]


def _inproj(x, norm_g, mod4, w_in_p, layer, seq_len):
    n_tok = x.shape[0]
    tm = IN_TM
    return pl.pallas_call(
        _inproj_kernel,
        grid=(n_tok // tm, N_SLABS),
        in_specs=_inproj_specs(tm, layer, _cond_row(seq_len, tm)),
        out_specs=pl.BlockSpec((None, tm, SLAB_W), lambda i, j: (j, i, 0)),
        out_shape=jax.ShapeDtypeStruct((N_SLABS, n_tok, SLAB_W), F32),
        scratch_shapes=[pltpu.VMEM((tm, D_MODEL), BF16)],
        compiler_params=_params("arbitrary", "arbitrary"),
        name="inproj",
    )(x, norm_g, mod4, mod4, w_in_p)


def _inproj_cache(x, norm_g, mod4, w_in_p, layer, caches):
    tm = IN_TM
    nb = tm // SEQ
    first = caches is None
    cache_shape = jax.ShapeDtypeStruct((BATCH, DEPTH, SEQ * N_HEADS, HEAD_W), F32)
    if first:
        half = lambda slab: (lambda i, j: (i, jnp.where(j <= slab, layer, layer + 1), 0, 0))
    else:
        half = lambda slab: (lambda i, j: (i, layer, 0, 0))
    cache_spec = lambda slab: pl.BlockSpec((nb, None, SEQ * N_HEADS, HEAD_W), half(slab))
    in_specs = _inproj_specs(tm, layer, _cond_row(None, tm))
    args = (x, norm_g, mod4, mod4, w_in_p)
    aliases = {}
    if not first:
        in_specs += [pl.BlockSpec(memory_space=pl.ANY)] * 2
        aliases = {len(args): 1, len(args) + 1: 2}
        args += tuple(caches)
    return pl.pallas_call(
        functools.partial(_inproj_cache_kernel, first_layer=first),
        grid=(T_CTX // tm, N_SLABS),
        in_specs=in_specs,
        out_specs=[pl.BlockSpec((None, tm, SLAB_W), lambda i, j: (j, i, 0)), cache_spec(S_K), cache_spec(S_V)],
        out_shape=[jax.ShapeDtypeStruct((N_SLABS, T_CTX, SLAB_W), F32), cache_shape, cache_shape],
        scratch_shapes=[pltpu.VMEM((tm, D_MODEL), BF16)],
        input_output_aliases=aliases,
        compiler_params=_params("arbitrary", "arbitrary"),
        name="inproj_cache",
    )(*args)


def _lam_value(lamp_ref, lam_init):
    lp = lamp_ref[...]
    t1 = jnp.sum(lp[0:1] * lp[1:2], axis=-1, keepdims=True)
    t2 = jnp.sum(lp[2:3] * lp[3:4], axis=-1, keepdims=True)
    return jnp.exp(t1) - jnp.exp(t2) + lam_init


def _diff_attend(items, lam, g, lam_init):
    m = items[0][0].shape[0]
    lane = lax.broadcasted_iota(jnp.int32, (m, HEAD_W), 1)
    ss = []
    for q, k_bf, _ in items:
        q1 = jnp.where(lane < HEAD_DIM, q, 0.0).astype(BF16)
        q2 = jnp.where(lane < HEAD_DIM, 0.0, q).astype(BF16)
        ss.append(_dot_nt(jnp.concatenate([q1, q2], axis=0), k_bf))
    es = [jnp.exp(s - jnp.max(s, axis=-1, keepdims=True)) for s in ss]
    ps = [e * (1.0 / jnp.sum(e, axis=-1, keepdims=True)) for e in es]
    ws = [(p[:m] - lam * p[m:]).astype(BF16) for p in ps]
    os_ = [_dot(w, v_bf) for w, (_, _, v_bf) in zip(ws, items)]
    outs = []
    for o in os_:
        ms = jnp.mean(o * o, axis=-1, keepdims=True)
        outs.append((o * lax.rsqrt(ms + EPS) * g) * (1.0 - lam_init))
    return outs


CTX_BATCHES = 2


def _attn_ctx_kernel(lamp_ref, g_ref, q_ref, k_ref, v_ref, ag_ref, o_ref, *, lam_init):
    lam = _lam_value(lamp_ref, lam_init)
    where = [(slice(b * SEQ, (b + 1) * SEQ), slice(h * HEAD_W, (h + 1) * HEAD_W))
             for b in range(CTX_BATCHES) for h in range(N_HEADS)]
    items = [(q_ref[r, c] * (HEAD_DIM ** -0.5), k_ref[r, c].astype(BF16), v_ref[r, c].astype(BF16)) for r, c in where]
    for (r, c), o in zip(where, _diff_attend(items, lam, g_ref[...], lam_init)):
        o_ref[r, c] = (o * _silu(ag_ref[r, c])).astype(BF16)


def _attn_ctx(proj, lamp, subln_g, layer, lam_init):
    rows = CTX_BATCHES * SEQ
    slab = lambda s: pl.BlockSpec((None, rows, ATT_W), lambda b, s=s: (s, b, 0))
    return pl.pallas_call(
        functools.partial(_attn_ctx_kernel, lam_init=lam_init),
        grid=(BATCH // CTX_BATCHES,),
        in_specs=[
            _of_layer(layer, (4, HEAD_DIM)),
            _of_layer(layer, (1, HEAD_W)),
            slab(S_Q), slab(S_K), slab(S_V), slab(S_AG),
        ],
        out_specs=pl.BlockSpec((rows, ATT_W), lambda b: (b, 0)),
        out_shape=jax.ShapeDtypeStruct((T_CTX, ATT_W), BF16),
        compiler_params=_params("arbitrary"),
        name="attn_ctx",
    )(lamp, subln_g, proj, proj, proj, proj)


Q_CHUNK = 512
Q_PART = 256
N_QC = DEC_SEQ // Q_CHUNK


def _rope(x, cos, sin):
    lane = lax.broadcasted_iota(jnp.int32, x.shape, 1)
    lower = (lane % (HEAD_DIM // 2)) < (HEAD_DIM // 4)
    nf = HEAD_DIM // 4
    partner = jnp.where(lower, pltpu.roll(x, HEAD_W - nf, 1), pltpu.roll(x, nf, 1))
    return x * cos + partner * sin


def _attn_smp_kernel(lamp_ref, g_ref, cosq_ref, sinq_ref, cos_ref, sin_ref, q_ref, k_ref, v_ref,
                     ck_ref, cv_ref, ag_ref, o_ref, ks_ref, vs_ref, *, lam_init):
    @pl.when(pl.program_id(2) == 0)
    def _():
        ks_ref[0:PAST_LEN, :] = ck_ref[...].astype(BF16)
        vs_ref[0:PAST_LEN, :] = cv_ref[...].astype(BF16)
        ks_ref[PAST_LEN:, :] = _rope(k_ref[...], cos_ref[...], sin_ref[...]).astype(BF16)
        vs_ref[PAST_LEN:, :] = v_ref[...].astype(BF16)

    lam = _lam_value(lamp_ref, lam_init)
    q = _rope(q_ref[...], cosq_ref[...], sinq_ref[...]) * (HEAD_DIM ** -0.5)
    k_bf, v_bf = ks_ref[...], vs_ref[...]
    parts = [slice(r, r + Q_PART) for r in range(0, Q_CHUNK, Q_PART)]
    items = [(q[p], k_bf, v_bf) for p in parts]
    for p, o in zip(parts, _diff_attend(items, lam, g_ref[...], lam_init)):
        o_ref[p, :] = (o * _silu(ag_ref[p, :])).astype(BF16)


def _attn_smp(proj, cache_k, cache_v, lamp, subln_g, cos, sin, layer, lam_init):
    full = lambda shape: pl.BlockSpec(shape, lambda b, h, c: (0, 0))
    return pl.pallas_call(
        functools.partial(_attn_smp_kernel, lam_init=lam_init),
        grid=(DEC_BATCH, N_HEADS, N_QC),
        in_specs=[
            _of_layer(layer, (4, HEAD_DIM)),
            _of_layer(layer, (1, HEAD_W)),
            pl.BlockSpec((Q_CHUNK, HEAD_W), lambda b, h, c: (c, 0)),
            pl.BlockSpec((Q_CHUNK, HEAD_W), lambda b, h, c: (c, 0)),
            full((DEC_SEQ, HEAD_W)),
            full((DEC_SEQ, HEAD_W)),
            pl.BlockSpec((None, Q_CHUNK, HEAD_W), lambda b, h, c: (S_Q, b * N_QC + c, h)),
            pl.BlockSpec((None, DEC_SEQ, HEAD_W), lambda b, h, c: (S_K, b, h)),
            pl.BlockSpec((None, DEC_SEQ, HEAD_W), lambda b, h, c: (S_V, b, h)),
            pl.BlockSpec((None, None, PAST_LEN, HEAD_W), lambda b, h, c: (b, layer, 0, h)),
            pl.BlockSpec((None, None, PAST_LEN, HEAD_W), lambda b, h, c: (b, layer, 0, h)),
            pl.BlockSpec((None, Q_CHUNK, HEAD_W), lambda b, h, c: (S_AG, b * N_QC + c, h)),
        ],
        out_specs=pl.BlockSpec((Q_CHUNK, HEAD_W), lambda b, h, c: (b * N_QC + c, h)),
        out_shape=jax.ShapeDtypeStruct((T_SMP, ATT_W), BF16),
        scratch_shapes=[
            pltpu.VMEM((PAST_LEN + DEC_SEQ, HEAD_W), BF16),
            pltpu.VMEM((PAST_LEN + DEC_SEQ, HEAD_W), BF16),
        ],
        compiler_params=_params("arbitrary", "arbitrary", "arbitrary"),
        name="attn_smp",
    )(lamp, subln_g, cos, sin, cos, sin, proj, proj, proj, cache_k, cache_v, proj)


PAD = 8
GATE_ROWS = 256
LRU_SLABS = LRU_W // LANES


def _lru_kernel(x_ref, lg_ref, cw_ref, cb_ref, w_ref, gb_ref, lam_ref, h0_ref, o_ref, hl_ref,
                xpad_ref, a_ref, b_ref, *, L, G):
    pitch = L + PAD
    zeros = jnp.zeros((PAD, LRU_W), F32)
    xpad_ref[0:PAD, :] = zeros
    xpad_ref[PAD + L:, :] = zeros
    nl = -lam_ref[...]
    neg_c = (-LRU_C * LOG2E) * (jnp.maximum(nl, 0.0) + jnp.log1p(jnp.exp(-jnp.abs(nl))))
    cw = cw_ref[...]
    cb = cb_ref[...]
    gb = gb_ref[...]

    def gates_of_sequence(g, carry):
        xpad_ref[PAD:PAD + L, :] = x_ref[pl.ds(pl.multiple_of(g * L, L), L), :]
        base = pl.multiple_of(g * pitch, PAD)
        for c in range(L // GATE_ROWS):
            r0 = c * GATE_ROWS
            xc = cb + cw[0:1] * xpad_ref[PAD - 2 + r0:PAD - 2 + r0 + GATE_ROWS, :]
            xc = xc + cw[1:2] * xpad_ref[PAD - 1 + r0:PAD - 1 + r0 + GATE_ROWS, :]
            xc = xc + cw[2:3] * xpad_ref[PAD + r0:PAD + r0 + GATE_ROWS, :]
            xc = xc + cw[3:4] * xpad_ref[PAD + 1 + r0:PAD + 1 + r0 + GATE_ROWS, :]
            gates = _dot(xc.astype(BF16), w_ref[...]) + gb
            for d in range(2):
                r = _sigmoid(gates[:, (2 * d) * LRU_W:(2 * d + 1) * LRU_W])
                i = _sigmoid(gates[:, (2 * d + 1) * LRU_W:(2 * d + 2) * LRU_W])
                a = jnp.exp2(r * neg_c[d:d + 1])
                a2 = 1.0 - a * a
                b = jnp.where(a2 > 0.0, a2 * lax.rsqrt(a2), 0.0) * (i * xc)
                rows = pl.ds(base + r0, GATE_ROWS)
                for s in range(LRU_SLABS):
                    a_ref[d * LRU_SLABS + s, rows, :] = a[:, s * LANES:(s + 1) * LANES]
                    b_ref[d * LRU_SLABS + s, rows, :] = b[:, s * LANES:(s + 1) * LANES]
        return carry

    lax.fori_loop(0, G, gates_of_sequence, 0)

    def step(t, hs):
        out = []
        for d in range(2):
            tt = t if d == 0 else L - 1 - t
            rows = pl.ds(tt, G, stride=pitch)
            for s in range(LRU_SLABS):
                k = d * LRU_SLABS + s
                h = a_ref[k, rows, :] * hs[k] + b_ref[k, rows, :]
                b_ref[k, rows, :] = h
                out.append(h)
        return tuple(out)

    h0 = tuple(h0_ref[d, :, s * LANES:(s + 1) * LANES] for d in range(2) for s in range(LRU_SLABS))
    hs = lax.fori_loop(0, L, step, h0, unroll=2)
    for d in range(2):
        for s in range(LRU_SLABS):
            hl_ref[d, :, s * LANES:(s + 1) * LANES] = hs[d * LRU_SLABS + s]

    def write_sequence(g, carry):
        rows = pl.ds(pl.multiple_of(g * pitch, PAD), L)
        orow = pl.ds(pl.multiple_of(g * L, L), L)
        for s in range(LRU_SLABS):
            lanes = slice(s * LANES, (s + 1) * LANES)
            h = b_ref[s, rows, :] + b_ref[LRU_SLABS + s, rows, :]
            o_ref[orow, lanes] = (h * _silu(lg_ref[orow, lanes])).astype(BF16)
        return carry

    lax.fori_loop(0, G, write_sequence, 0)


def _lru(proj, conv_w, conv_b, w_gates, b_gates, lam, h0, layer, h0_layer, L, G):
    nseq = h0.shape[2]
    pitch = L + PAD
    return pl.pallas_call(
        functools.partial(_lru_kernel, L=L, G=G),
        grid=(nseq // G,),
        in_specs=[
            pl.BlockSpec((None, G * L, LRU_W), lambda s: (S_LXHV, s, 0)),
            pl.BlockSpec((None, G * L, LRU_W), lambda s: (S_LGHG, s, 0)),
            _of_layer(layer, (4, LRU_W)),
            _of_layer(layer, (1, LRU_W)),
            _of_layer(layer, (LRU_W, 4 * LRU_W)),
            _of_layer(layer, (1, 4 * LRU_W)),
            _of_layer(layer, (2, LRU_W)),
            pl.BlockSpec((None, 2, G, LRU_W), lambda s: (h0_layer, 0, s, 0)),
        ],
        out_specs=[
            pl.BlockSpec((G * L, LRU_W), lambda s: (s, 0)),
            pl.BlockSpec((2, G, LRU_W), lambda s: (0, s, 0)),
        ],
        out_shape=[
            jax.ShapeDtypeStruct((nseq * L, LRU_W), BF16),
            jax.ShapeDtypeStruct((2, nseq, LRU_W), F32),
        ],
        scratch_shapes=[
            pltpu.VMEM((L + 2 * PAD, LRU_W), F32),
            pltpu.VMEM((2 * LRU_SLABS, G * pitch, LANES), F32),
            pltpu.VMEM((2 * LRU_SLABS, G * pitch, LANES), F32),
        ],
        compiler_params=_params("arbitrary"),
        name=f"lru_{L}",
    )(proj, proj, conv_w, conv_b, w_gates, b_gates, lam, h0)


def _lru_gate_weights(wa, wi, ba, bi):
    def dense(w):
        eye = jnp.eye(LRU_BLOCKS, dtype=w.dtype)
        return jnp.einsum("lncd,nm->lncmd", w, eye).reshape(DEPTH, LRU_W, LRU_W)

    w = jnp.concatenate([dense(wa[:, 0]), dense(wi[:, 0]), dense(wa[:, 1]), dense(wi[:, 1])], axis=2)
    b = jnp.concatenate([ba[:, 0], bi[:, 0], ba[:, 1], bi[:, 1]], axis=1)[:, None, :]
    return w.astype(BF16), b


FILT_ROWS = 256


def _filt_kernel(z_ref, decay_ref, w1_ref, b1_ref, w2_ref, b2_ref, w3_ref, ft_ref, fb_ref, g_ref, taps_ref, *, L):
    i = pl.program_id(0)

    @pl.when(i == 0)
    def _():
        h = jnp.sin(_dot_hi(z_ref[...], w1_ref[...]) + b1_ref[...])
        h = jnp.sin(_dot_hi(h, w2_ref[...]) + b2_ref[...])
        filt = _dot_hi(h, w3_ref[...])
        decay = decay_ref[...]
        row = lax.broadcasted_iota(jnp.int32, (L, HY_W), 0)
        taps_ref[:, :HY_W] = (filt[:, :HY_W] * decay).astype(BF16)
        taps_ref[:, HY_W:] = jnp.where(row == 0, 0.0, filt[:, HY_W:] * decay).astype(BF16)

    taps = taps_ref[...]
    top = _dot(ft_ref[...], taps)
    bot = _dot(fb_ref[...], taps)
    first = (lax.broadcasted_iota(jnp.int32, (FILT_ROWS, HY_W), 0) + i * FILT_ROWS) == 0
    gt = top[:, :HY_W] + top[:, HY_W:]
    g_ref[0] = gt
    g_ref[1] = jnp.where(first, 0.0, bot[:, :HY_W] - bot[:, HY_W:])
    g_ref[2] = jnp.where(first, bot[:, :HY_W] + bot[:, HY_W:], gt)


def _filters(z, decay, w1, b1, w2, b2, w3, fwd_bf, layer, L):
    full = lambda shape: pl.BlockSpec(shape, lambda i: (0,) * len(shape))
    nblk = L // FILT_ROWS
    return pl.pallas_call(
        functools.partial(_filt_kernel, L=L),
        grid=(nblk,),
        in_specs=[
            full((L, HY_POS)), full((L, HY_W)),
            _of_layer(layer, (HY_POS, HY_HIDDEN)), _of_layer(layer, (1, HY_HIDDEN)),
            _of_layer(layer, (HY_HIDDEN, HY_HIDDEN)), _of_layer(layer, (1, HY_HIDDEN)),
            _of_layer(layer, (HY_HIDDEN, 2 * HY_W)),
            pl.BlockSpec((FILT_ROWS, L), lambda i: (i, 0)),
            pl.BlockSpec((FILT_ROWS, L), lambda i: (nblk + i, 0)),
        ],
        out_specs=pl.BlockSpec((3, FILT_ROWS, HY_W), lambda i: (0, i, 0)),
        out_shape=jax.ShapeDtypeStruct((3, L, HY_W), F32),
        scratch_shapes=[pltpu.VMEM((L, 2 * HY_W), BF16)],
        compiler_params=_params("arbitrary"),
        name=f"hyena_filters_{L}",
    )(z, decay, w1, b1, w2, b2, w3, fwd_bf, fwd_bf)


HY_PARAM_ROWS = 16


def _hyena_kernel(hv_ref, hx1_ref, hx0_ref, hg_ref, prm_ref, g_ref, f_ref, fi_ref, o_ref, xpad_ref, *, L, ch):
    zeros = jnp.zeros((PAD, ch), F32)
    xpad_ref[0:PAD, :] = zeros
    xpad_ref[PAD + L:, :] = zeros
    prm = prm_ref[...]

    def conv(ref, stream):
        xpad_ref[PAD:PAD + L, :] = ref[...]
        u = prm[9 + stream:10 + stream] + prm[3 * stream:3 * stream + 1] * xpad_ref[PAD - 1:PAD - 1 + L, :]
        u = u + prm[3 * stream + 1:3 * stream + 2] * xpad_ref[PAD:PAD + L, :]
        return u + prm[3 * stream + 2:3 * stream + 3] * xpad_ref[PAD + 1:PAD + 1 + L, :]

    z = conv(hx1_ref, 1) * conv(hv_ref, 0)
    spec = _dot(f_ref[...], z.astype(BF16))
    top, bot = spec[:L], spec[L:]
    gx = g_ref[1]
    yt = top * g_ref[0] - bot * gx
    yb = top * gx + bot * g_ref[2]
    y = _dot(fi_ref[...], jnp.concatenate([yt, yb], axis=0).astype(BF16))
    hy = conv(hx0_ref, 2) * (y + z * prm[12:13])
    o_ref[...] = (hy * _silu(hg_ref[...])).astype(BF16)


def _hyena(proj, prm, g, fwd, inv, layer, L, nseq, ch):
    nch = HY_W // ch
    full = lambda shape: pl.BlockSpec(shape, lambda s, j: (0,) * len(shape))
    col = lambda slab, off: pl.BlockSpec((None, L, ch), lambda s, j, slab=slab, off=off: (slab, s, off * nch + j))
    return pl.pallas_call(
        functools.partial(_hyena_kernel, L=L, ch=ch),
        grid=(nseq, nch),
        in_specs=[
            col(S_LXHV, 1), col(S_HX, 0), col(S_HX, 1), col(S_LGHG, 1),
            pl.BlockSpec((None, HY_PARAM_ROWS, ch), lambda s, j: (layer, 0, j)),
            pl.BlockSpec((3, L, ch), lambda s, j: (0, 0, j)),
            full((2 * L, L)), full((L, 2 * L)),
        ],
        out_specs=pl.BlockSpec((L, ch), lambda s, j: (s, j)),
        out_shape=jax.ShapeDtypeStruct((nseq * L, HY_W), BF16),
        scratch_shapes=[pltpu.VMEM((L + 2 * PAD, ch), F32)],
        compiler_params=_params("arbitrary", "arbitrary"),
        name=f"hyena_{L}",
    )(proj, proj, proj, proj, prm, g, fwd, inv)


def _hyena_params(conv_w, conv_b, d):
    rows = [conv_w[:, :, s * HY_W:(s + 1) * HY_W] for s in range(3)]
    rows += [conv_b[:, None, s * HY_W:(s + 1) * HY_W] for s in range(3)]
    rows += [d[:, None, :], jnp.zeros((DEPTH, HY_PARAM_ROWS - 13, HY_W), F32)]
    return jnp.concatenate(rows, axis=1)


OUT_TM = 512


def _outproj_kernel(x_ref, gate_ref, att_ref, lru_ref, hy_ref, w_ref, fg_ref, o_ref, *, final):
    acc = _dot(att_ref[...], w_ref[0:ATT_W, :])
    acc = acc + _dot(lru_ref[...], w_ref[ATT_W:ATT_W + LRU_W, :])
    acc = acc + _dot(hy_ref[...], w_ref[ATT_W + LRU_W:, :])
    y = x_ref[...] + gate_ref[...] * acc
    if final:
        ms = jnp.mean(y * y, axis=-1, keepdims=True)
        y = (y * lax.rsqrt(ms + EPS)) * fg_ref[...]
    o_ref[...] = y


def _outproj(x, mod4, att, lru, hy, w_out_bf, final_g, layer, final, seq_len):
    n_tok = x.shape[0]
    tm = OUT_TM
    cond = _cond_row(seq_len, tm)
    return pl.pallas_call(
        functools.partial(_outproj_kernel, final=final),
        grid=(n_tok // tm,),
        in_specs=[
            pl.BlockSpec((tm, D_MODEL), lambda i: (i, 0)),
            pl.BlockSpec((None, None, 1, D_MODEL), lambda i: (layer, cond(i), 0, 2)),
            pl.BlockSpec((tm, ATT_W), lambda i: (i, 0)),
            pl.BlockSpec((tm, LRU_W), lambda i: (i, 0)),
            pl.BlockSpec((tm, HY_W), lambda i: (i, 0)),
            pl.BlockSpec((None, D_MODEL, D_MODEL), lambda i: (layer, 0, 0), pipeline_mode=pl.Buffered(1)),
            pl.BlockSpec((1, D_MODEL), lambda i: (0, 0)),
        ],
        out_specs=pl.BlockSpec((tm, D_MODEL), lambda i: (i, 0)),
        out_shape=jax.ShapeDtypeStruct((n_tok, D_MODEL), F32),
        compiler_params=_params("arbitrary"),
        name="outproj",
    )(x, mod4, att, lru, hy, w_out_bf, final_g)


def _slab_w_in(w):
    a, l, h = ATT_W, LRU_W, HY_W
    o = 4 * a
    parts = [w[..., :o], w[..., o + l:o + 2 * l], w[..., o + 2 * l + 3 * h:], w[..., o:o + l],
             w[..., o + 2 * l:o + 2 * l + 3 * h]]
    w = jnp.concatenate(parts, axis=-1).reshape(DEPTH, D_MODEL, N_SLABS, SLAB_W)
    return jnp.transpose(w, (0, 2, 1, 3)).astype(BF16)


def kernel(x_prompt, x_sample, cache_k, cache_v, state_lru, c, c_ctx, norm_g, w_ada, b_ada, w_in, w_out, lam_q1, lam_k1, lam_q2, lam_k2, attn_subln_g, lru_conv_w, lru_conv_b, lru_wa, lru_ba, lru_wi, lru_bi, lru_lam, hy_conv_w, hy_conv_b, hy_w1, hy_b1, hy_w2, hy_b2, hy_w3, hy_d, final_g):
    cos, sin = _rope_tables()
    tables = {}
    for L in (SEQ, DEC_SEQ):
        fwd, inv = _dft_tables(L)
        z, decay = _hyena_tables(L)
        tables[L] = (fwd.astype(BF16), inv.astype(BF16), z, decay)

    cond = jnp.concatenate([c_ctx[None, :], c, jnp.zeros((N_COND - 1 - DEC_BATCH, D_MODEL), F32)], axis=0)
    mod4 = _ada(cond, w_ada, b_ada).reshape(DEPTH, N_COND, 1, 3 * D_MODEL)

    xc = x_prompt.reshape(T_CTX, D_MODEL)
    xs = x_sample.reshape(T_SMP, D_MODEL)
    ck = cache_k.reshape(DEC_BATCH, DEPTH, PAST_LEN, ATT_W)
    cv = cache_v.reshape(DEC_BATCH, DEPTH, PAST_LEN, ATT_W)
    h0_ctx = jnp.zeros((1, 2, BATCH, LRU_W), F32)
    h0_smp = jnp.transpose(state_lru, (1, 2, 0, 3))
    norm_g3 = norm_g[:, None, :]
    lamp = jnp.stack([lam_q1, lam_k1, lam_q2, lam_k2], axis=1)
    subln = attn_subln_g[:, None, :]
    w_in_p = _slab_w_in(w_in)
    w_out_bf = w_out.astype(BF16)
    wg, bg = _lru_gate_weights(lru_wa, lru_wi, lru_ba, lru_bi)
    lru_args = (lru_conv_w, lru_conv_b[:, None, :], wg, bg, lru_lam)
    hy_prm = _hyena_params(hy_conv_w, hy_conv_b, hy_d)
    filt_args = (hy_w1, hy_b1[:, None, :], hy_w2, hy_b2[:, None, :], hy_w3)
    fg = final_g[None, :]

    assert DEPTH == 2
    caches, hs = None, []
    for l in range(DEPTH):
        final = l == DEPTH - 1
        lam_init = 0.8 - 0.6 * math.exp(-0.3 * l)

        fwd, inv, z, decay = tables[SEQ]
        proj, *caches = _inproj_cache(xc, norm_g3, mod4, w_in_p, l, caches)
        att = _attn_ctx(proj, lamp, subln, l, lam_init)
        lru, h_last = _lru(proj, *lru_args, h0_ctx, l, 0, SEQ, 8)
        hs.append(h_last)
        g = _filters(z, decay, *filt_args, fwd, l, SEQ)
        hy = _hyena(proj, hy_prm, g, fwd, inv, l, SEQ, BATCH, HY_W)
        xc = _outproj(xc, mod4, att, lru, hy, w_out_bf, fg, l, final, None)

        fwd, inv, z, decay = tables[DEC_SEQ]
        proj = _inproj(xs, norm_g3, mod4, w_in_p, l, DEC_SEQ)
        att = _attn_smp(proj, ck, cv, lamp, subln, cos, sin, l, lam_init)
        lru, _ = _lru(proj, *lru_args, h0_smp, l, l, DEC_SEQ, DEC_BATCH)
        g = _filters(z, decay, *filt_args, fwd, l, DEC_SEQ)
        hy = _hyena(proj, hy_prm, g, fwd, inv, l, DEC_SEQ, DEC_BATCH, HY_W // 2)
        xs = _outproj(xs, mod4, att, lru, hy, w_out_bf, fg, l, final, DEC_SEQ)

    y_prompt = xc.reshape(BATCH, SEQ, D_MODEL)
    y_sample = xs.reshape(DEC_BATCH, DEC_SEQ, D_MODEL)
    new_k, new_v = (a.reshape(BATCH, DEPTH, SEQ, N_HEADS, HEAD_W) for a in caches)
    new_state = jnp.transpose(jnp.stack(hs, axis=0), (2, 0, 1, 3))
    return (y_prompt, y_sample, new_k, new_v, new_state)
```

```python
import functools
import math

import numpy as np
import jax
import jax.numpy as jnp
from jax import lax
from jax.experimental import pallas as pl
from jax.experimental.pallas import tpu as pltpu

D_MODEL = 2048
BATCH = 32
SEQ = 256
DEPTH = 2
DEC_BATCH = 2
DEC_SEQ = 1024
PAST_LEN = 512
GRID_W = 64
ATT_W = 1024
LRU_W = 512
HY_W = 512
HEAD_DIM = 64
N_HEADS = 8
HEAD_W = 2 * HEAD_DIM
LRU_BLOCKS = 8
LRU_BW = LRU_W // LRU_BLOCKS
LRU_C = 8.0
HY_BANDS = 16
HY_POS = 1 + 2 * HY_BANDS
HY_HIDDEN = 64
HY_DECAY_FAST = 0.3
HY_DECAY_SLOW = 1.5
HY_DECAY_TARGET = 1e-2
ROPE_BASE = 10000.0
EPS = 1e-6
IN_W = 4 * ATT_W + 2 * LRU_W + 4 * HY_W

T_CTX = BATCH * SEQ
T_SMP = DEC_BATCH * DEC_SEQ
N_COND = 8

SLAB_W = 1024
N_SLABS = IN_W // SLAB_W
S_Q, S_K, S_V, S_AG, S_LX_LG, S_HV_HX1, S_HX0_HG = range(N_SLABS)

LANES = 128
F32 = jnp.float32
BF16 = jnp.bfloat16
VMEM_LIMIT = 58 * 1024 * 1024


LOG2E = 1.4426950408889634


def _sigmoid(x):
    return 1.0 / (1.0 + jnp.exp2(x * (-LOG2E)))


def _silu(x):
    return x * _sigmoid(x)


def _dot(a, b):
    return jnp.dot(a, b, preferred_element_type=F32)


def _dot_nt(a, b):
    return lax.dot_general(a, b, (((1,), (1,)), ((), ())), preferred_element_type=F32)


def _dot_hi(a, b):
    return jnp.dot(a, b, precision=lax.Precision.HIGHEST, preferred_element_type=F32)


def _params(*sem):
    return pltpu.CompilerParams(dimension_semantics=sem, vmem_limit_bytes=VMEM_LIMIT)


def _rope_tables():
    t = np.arange(DEC_SEQ)
    pos = np.stack([t // GRID_W, t % GRID_W], axis=1).astype(np.float64)
    nf = HEAD_DIM // 4
    inv = ROPE_BASE ** (-np.arange(nf, dtype=np.float64) / nf)
    lane = np.arange(HEAD_W)
    j = lane % HEAD_DIM
    axis = j // (HEAD_DIM // 2)
    f = j % nf
    upper = (j % (HEAD_DIM // 2)) >= nf
    ang = pos[:, axis] * inv[f][None, :]
    cos = np.cos(ang)
    sin = np.sin(ang) * np.where(upper, 1.0, -1.0)[None, :]
    return jnp.asarray(cos, F32), jnp.asarray(sin, F32)


def _dft_tables(L):
    n = 2 * L
    k = np.arange(L)[:, None]
    t = np.arange(L)[None, :]
    ang = 2.0 * np.pi * ((k * t) % n).astype(np.float64) / n
    fwd = np.concatenate([np.cos(ang), -np.sin(ang)], axis=0)
    fwd[L, :] = (-1.0) ** np.arange(L)
    wk = np.where(np.arange(L) == 0, 1.0, 2.0)[None, :]
    ang_t = ang.T
    inv = np.concatenate([wk * np.cos(ang_t), -2.0 * np.sin(ang_t)], axis=1) / n
    inv[:, L] = ((-1.0) ** np.arange(L)) / n
    return jnp.asarray(fwd, F32), jnp.asarray(inv, F32)


def _hyena_tables(L):
    pos = np.arange(L, dtype=np.float64)
    t = pos / float(max(L - 1, 1))
    bands = np.linspace(1e-4, HY_BANDS - 1, HY_BANDS)
    ang = (2.0 * math.pi / L) * pos[:, None] * bands[None, :]
    z = np.concatenate([t[:, None], np.cos(ang), np.sin(ang)], axis=-1)
    lo = abs(math.log(HY_DECAY_TARGET) / HY_DECAY_SLOW)
    hi = abs(math.log(HY_DECAY_TARGET) / HY_DECAY_FAST)
    deltas = np.linspace(lo, hi, HY_W)
    decay = np.exp(-t[:, None] * deltas[None, :])
    return jnp.asarray(z, F32), jnp.asarray(decay, F32)


def _ada_kernel(c_ref, w_ref, b_ref, o_ref):
    s = _silu(c_ref[...])
    o_ref[...] = _dot(s.astype(BF16), w_ref[...].astype(BF16)) + b_ref[...]


def _ada(cond, w_ada, b_ada):
    tn = 1024
    return pl.pallas_call(
        _ada_kernel,
        grid=(DEPTH, 3 * D_MODEL // tn),
        in_specs=[
            pl.BlockSpec((N_COND, D_MODEL), lambda l, j: (0, 0)),
            pl.BlockSpec((None, D_MODEL, tn), lambda l, j: (l, 0, j)),
            pl.BlockSpec((None, 1, tn), lambda l, j: (l, 0, j)),
        ],
        out_specs=pl.BlockSpec((None, N_COND, tn), lambda l, j: (l, 0, j)),
        out_shape=jax.ShapeDtypeStruct((DEPTH, N_COND, 3 * D_MODEL), F32),
        compiler_params=_params("arbitrary", "arbitrary"),
        name="ada",
    )(cond, w_ada, b_ada.reshape(DEPTH, 1, 3 * D_MODEL))


def _cond_row(seq_len, tm):
    if seq_len is None:
        return lambda i: 0
    return lambda i: 1 + i // (seq_len // tm)


IN_TM = 1024
NORM_ROWS = 32


def _norm_modulate(x_ref, g_ref, shift_ref, scale_ref, h_ref):
    g = g_ref[...]
    sc = 1.0 + scale_ref[...]
    sh = shift_ref[...]

    def body(r, carry):
        rows = pl.ds(pl.multiple_of(r * NORM_ROWS, NORM_ROWS), NORM_ROWS)
        x = x_ref[rows, :]
        ms = jnp.mean(x * x, axis=-1, keepdims=True)
        xn = x * lax.rsqrt(ms + EPS)
        h_ref[rows, :] = ((xn * g) * sc + sh).astype(BF16)
        return carry

    lax.fori_loop(0, IN_TM // NORM_ROWS, body, 0, unroll=4)


def _inproj_kernel(x_ref, g_ref, shift_ref, scale_ref, w_ref, o_ref, h_ref):
    @pl.when(pl.program_id(1) == 0)
    def _():
        _norm_modulate(x_ref, g_ref, shift_ref, scale_ref, h_ref)

    o_ref[...] = _dot(h_ref[...], w_ref[...])


def _inproj_cache_kernel(*refs, first_layer):
    if first_layer:
        x_ref, g_ref, shift_ref, scale_ref, w_ref, o_ref, kc_ref, vc_ref, h_ref = refs
    else:
        x_ref, g_ref, shift_ref, scale_ref, w_ref, _, _, o_ref, kc_ref, vc_ref, h_ref = refs
    j = pl.program_id(1)

    @pl.when(j == 0)
    def _():
        _norm_modulate(x_ref, g_ref, shift_ref, scale_ref, h_ref)

    o_ref[...] = _dot(h_ref[...], w_ref[...])

    def scatter(dst_ref):
        for b in range(IN_TM // SEQ):
            for h in range(N_HEADS):
                dst_ref[b, pl.ds(h, SEQ, stride=N_HEADS), :] = o_ref[b * SEQ:(b + 1) * SEQ, h * HEAD_W:(h + 1) * HEAD_W]

    @pl.when(j == S_K)
    def _():
        scatter(kc_ref)

    @pl.when(j == S_V)
    def _():
        scatter(vc_ref)

    if first_layer:
        @pl.when(j == S_K + 1)
        def _():
            kc_ref[...] = jnp.zeros(kc_ref.shape, F32)

        @pl.when(j == S_V + 1)
        def _():
            vc_ref[...] = jnp.zeros(vc_ref.shape, F32)


def _of_layer(layer, shape):
    return pl.BlockSpec((None,) + tuple(shape), lambda *_: (layer,) + (0,) * len(shape))


def _inproj_specs(tm, layer, cond):
    return [
        pl.BlockSpec((tm, D_MODEL), lambda i, j: (i, 0)),
        _of_layer(layer, (1, D_MODEL)),
        pl.BlockSpec((None, None, 1, D_MODEL), lambda i, j: (layer, cond(i), 0, 0)),
        pl.BlockSpec((None, None, 1, D_MODEL), lambda i, j: (layer, cond(i), 0, 1)),
        pl.BlockSpec((None, D_MODEL, SLAB_W), lambda i, j: (layer, 0, j)),
    ]


def _inproj(x, norm_g, mod4, w_in_p, layer, seq_len):
    n_tok = x.shape[0]
    tm = IN_TM
    return pl.pallas_call(
        _inproj_kernel,
        grid=(n_tok // tm, N_SLABS),
        in_specs=_inproj_specs(tm, layer, _cond_row(seq_len, tm)),
        out_specs=pl.BlockSpec((None, tm, SLAB_W), lambda i, j: (j, i, 0)),
        out_shape=jax.ShapeDtypeStruct((N_SLABS, n_tok, SLAB_W), F32),
        scratch_shapes=[pltpu.VMEM((tm, D_MODEL), BF16)],
        compiler_params=_params("arbitrary", "arbitrary"),
        name="inproj",
    )(x, norm_g, mod4, mod4, w_in_p)


def _inproj_cache(x, norm_g, mod4, w_in_p, layer, caches):
    tm = IN_TM
    nb = tm // SEQ
    first = caches is None
    cache_shape = jax.ShapeDtypeStruct((BATCH, DEPTH, SEQ * N_HEADS, HEAD_W), F32)
    if first:
        half = lambda slab: (lambda i, j: (i, jnp.where(j <= slab, layer, layer + 1), 0, 0))
    else:
        half = lambda slab: (lambda i, j: (i, layer, 0, 0))
    cache_spec = lambda slab: pl.BlockSpec((nb, None, SEQ * N_HEADS, HEAD_W), half(slab))
    in_specs = _inproj_specs(tm, layer, _cond_row(None, tm))
    args = (x, norm_g, mod4, mod4, w_in_p)
    aliases = {}
    if not first:
        in_specs += [pl.BlockSpec(memory_space=pl.ANY)] * 2
        aliases = {len(args): 1, len(args) + 1: 2}
        args += tuple(caches)
    return pl.pallas_call(
        functools.partial(_inproj_cache_kernel, first_layer=first),
        grid=(T_CTX // tm, N_SLABS),
        in_specs=in_specs,
        out_specs=[pl.BlockSpec((None, tm, SLAB_W), lambda i, j: (j, i, 0)), cache_spec(S_K), cache_spec(S_V)],
        out_shape=[jax.ShapeDtypeStruct((N_SLABS, T_CTX, SLAB_W), F32), cache_shape, cache_shape],
        scratch_shapes=[pltpu.VMEM((tm, D_MODEL), BF16)],
        input_output_aliases=aliases,
        compiler_params=_params("arbitrary", "arbitrary"),
        name="inproj_cache",
    )(*args)


def _lam_value(lamp_ref, lam_init):
    lp = lamp_ref[...]
    t1 = jnp.sum(lp[0:1] * lp[1:2], axis=-1, keepdims=True)
    t2 = jnp.sum(lp[2:3] * lp[3:4], axis=-1, keepdims=True)
    return jnp.exp(t1) - jnp.exp(t2) + lam_init


def _diff_attend(items, lam, g, lam_init):
    m = items[0][0].shape[0]
    lane = lax.broadcasted_iota(jnp.int32, (m, HEAD_W), 1)
    ss = []
    for q, k_bf, _ in items:
        q1 = jnp.where(lane < HEAD_DIM, q, 0.0).astype(BF16)
        q2 = jnp.where(lane < HEAD_DIM, 0.0, q).astype(BF16)
        ss.append(_dot_nt(jnp.concatenate([q1, q2], axis=0), k_bf))
    es = [jnp.exp(s - jnp.max(s, axis=-1, keepdims=True)) for s in ss]
    ps = [e * (1.0 / jnp.sum(e, axis=-1, keepdims=True)) for e in es]
    ws = [(p[:m] - lam * p[m:]).astype(BF16) for p in ps]
    os_ = [_dot(w, v_bf) for w, (_, _, v_bf) in zip(ws, items)]
    outs = []
    for o in os_:
        ms = jnp.mean(o * o, axis=-1, keepdims=True)
        outs.append((o * lax.rsqrt(ms + EPS) * g) * (1.0 - lam_init))
    return outs


CTX_BATCHES = 2


def _attn_ctx_kernel(lamp_ref, g_ref, q_ref, k_ref, v_ref, ag_ref, o_ref, *, lam_init):
    lam = _lam_value(lamp_ref, lam_init)
    where = [(slice(b * SEQ, (b + 1) * SEQ), slice(h * HEAD_W, (h + 1) * HEAD_W))
             for b in range(CTX_BATCHES) for h in range(N_HEADS)]
    items = [(q_ref[r, c] * (HEAD_DIM ** -0.5), k_ref[r, c].astype(BF16), v_ref[r, c].astype(BF16)) for r, c in where]
    for (r, c), o in zip(where, _diff_attend(items, lam, g_ref[...], lam_init)):
        o_ref[r, c] = (o * _silu(ag_ref[r, c])).astype(BF16)


def _attn_ctx(proj, lamp, subln_g, layer, lam_init):
    rows = CTX_BATCHES * SEQ
    slab = lambda s: pl.BlockSpec((None, rows, ATT_W), lambda b, s=s: (s, b, 0))
    return pl.pallas_call(
        functools.partial(_attn_ctx_kernel, lam_init=lam_init),
        grid=(BATCH // CTX_BATCHES,),
        in_specs=[
            _of_layer(layer, (4, HEAD_DIM)),
            _of_layer(layer, (1, HEAD_W)),
            slab(S_Q), slab(S_K), slab(S_V), slab(S_AG),
        ],
        out_specs=pl.BlockSpec((rows, ATT_W), lambda b: (b, 0)),
        out_shape=jax.ShapeDtypeStruct((T_CTX, ATT_W), BF16),
        compiler_params=_params("arbitrary"),
        name="attn_ctx",
    )(lamp, subln_g, proj, proj, proj, proj)


Q_CHUNK = 512
Q_PART = 256
N_QC = DEC_SEQ // Q_CHUNK


def _rope(x, cos, sin):
    lane = lax.broadcasted_iota(jnp.int32, x.shape, 1)
    lower = (lane % (HEAD_DIM // 2)) < (HEAD_DIM // 4)
    nf = HEAD_DIM // 4
    partner = jnp.where(lower, pltpu.roll(x, HEAD_W - nf, 1), pltpu.roll(x, nf, 1))
    return x * cos + partner * sin


def _attn_smp_kernel(lamp_ref, g_ref, cosq_ref, sinq_ref, cos_ref, sin_ref, q_ref, k_ref, v_ref,
                     ck_ref, cv_ref, ag_ref, o_ref, ks_ref, vs_ref, *, lam_init):
    @pl.when(pl.program_id(2) == 0)
    def _():
        head_rows = pl.ds(pl.program_id(1), PAST_LEN, stride=N_HEADS)
        ks_ref[0:PAST_LEN, :] = ck_ref[head_rows, :].astype(BF16)
        vs_ref[0:PAST_LEN, :] = cv_ref[head_rows, :].astype(BF16)
        ks_ref[PAST_LEN:, :] = _rope(k_ref[...], cos_ref[...], sin_ref[...]).astype(BF16)
        vs_ref[PAST_LEN:, :] = v_ref[...].astype(BF16)

    lam = _lam_value(lamp_ref, lam_init)
    q = _rope(q_ref[...], cosq_ref[...], sinq_ref[...]) * (HEAD_DIM ** -0.5)
    k_bf, v_bf = ks_ref[...], vs_ref[...]
    parts = [slice(r, r + Q_PART) for r in range(0, Q_CHUNK, Q_PART)]
    items = [(q[p], k_bf, v_bf) for p in parts]
    for p, o in zip(parts, _diff_attend(items, lam, g_ref[...], lam_init)):
        o_ref[p, :] = (o * _silu(ag_ref[p, :])).astype(BF16)


def _attn_smp(proj, cache_k, cache_v, lamp, subln_g, cos, sin, layer, lam_init):
    full = lambda shape: pl.BlockSpec(shape, lambda b, h, c: (0, 0))
    return pl.pallas_call(
        functools.partial(_attn_smp_kernel, lam_init=lam_init),
        grid=(DEC_BATCH, N_HEADS, N_QC),
        in_specs=[
            _of_layer(layer, (4, HEAD_DIM)),
            _of_layer(layer, (1, HEAD_W)),
            pl.BlockSpec((Q_CHUNK, HEAD_W), lambda b, h, c: (c, 0)),
            pl.BlockSpec((Q_CHUNK, HEAD_W), lambda b, h, c: (c, 0)),
            full((DEC_SEQ, HEAD_W)),
            full((DEC_SEQ, HEAD_W)),
            pl.BlockSpec((None, Q_CHUNK, HEAD_W), lambda b, h, c: (S_Q, b * N_QC + c, h)),
            pl.BlockSpec((None, DEC_SEQ, HEAD_W), lambda b, h, c: (S_K, b, h)),
            pl.BlockSpec((None, DEC_SEQ, HEAD_W), lambda b, h, c: (S_V, b, h)),
            pl.BlockSpec((None, None, PAST_LEN * N_HEADS, HEAD_W), lambda b, h, c: (b, layer, 0, 0)),
            pl.BlockSpec((None, None, PAST_LEN * N_HEADS, HEAD_W), lambda b, h, c: (b, layer, 0, 0)),
            pl.BlockSpec((None, Q_CHUNK, HEAD_W), lambda b, h, c: (S_AG, b * N_QC + c, h)),
        ],
        out_specs=pl.BlockSpec((Q_CHUNK, HEAD_W), lambda b, h, c: (b * N_QC + c, h)),
        out_shape=jax.ShapeDtypeStruct((T_SMP, ATT_W), BF16),
        scratch_shapes=[
            pltpu.VMEM((PAST_LEN + DEC_SEQ, HEAD_W), BF16),
            pltpu.VMEM((PAST_LEN + DEC_SEQ, HEAD_W), BF16),
        ],
        compiler_params=_params("arbitrary", "arbitrary", "arbitrary"),
        name="attn_smp",
    )(lamp, subln_g, cos, sin, cos, sin, proj, proj, proj, cache_k, cache_v, proj)


PAD = 8
GATE_ROWS = 256
LRU_SLABS = LRU_W // LANES


def _lru_kernel(x_ref, lg_ref, cw_ref, cb_ref, wa_ref, wi_ref, ba_ref, bi_ref, lam_ref, h0_ref, o_ref, hl_ref,
                xpad_ref, a_ref, b_ref, h_ref, w_ref, *, L, G):
    @pl.when(pl.program_id(0) == 0)
    def _():
        w_ref[...] = jnp.zeros(w_ref.shape, BF16)
        for d in range(2):
            for gate, src in enumerate((wa_ref, wi_ref)):
                col0 = (2 * d + gate) * LRU_W
                for n in range(LRU_BLOCKS):
                    lo = n * LRU_BW
                    w_ref[lo:lo + LRU_BW, col0 + lo:col0 + lo + LRU_BW] = src[d, n].astype(BF16)

    zeros = jnp.zeros((PAD, LRU_W), F32)
    xpad_ref[0:PAD, :] = zeros
    xpad_ref[PAD + L:, :] = zeros
    nl = -lam_ref[...]
    neg_c = (-LRU_C * LOG2E) * (jnp.maximum(nl, 0.0) + jnp.log1p(jnp.exp(-jnp.abs(nl))))
    cw = cw_ref[...]
    cb = cb_ref[...]
    ba = ba_ref[...]
    bi = bi_ref[...]

    def gates_of_sequence(g, carry):
        xpad_ref[PAD:PAD + L, :] = x_ref[pl.ds(pl.multiple_of(g * L, L), L), :]
        for c in range(L // GATE_ROWS):
            r0 = c * GATE_ROWS
            xc = cb + cw[0:1] * xpad_ref[PAD - 2 + r0:PAD - 2 + r0 + GATE_ROWS, :]
            xc = xc + cw[1:2] * xpad_ref[PAD - 1 + r0:PAD - 1 + r0 + GATE_ROWS, :]
            xc = xc + cw[2:3] * xpad_ref[PAD + r0:PAD + r0 + GATE_ROWS, :]
            xc = xc + cw[3:4] * xpad_ref[PAD + 1 + r0:PAD + 1 + r0 + GATE_ROWS, :]
            gates = _dot(xc.astype(BF16), w_ref[...])
            for d in range(2):
                r = _sigmoid(gates[:, (2 * d) * LRU_W:(2 * d + 1) * LRU_W] + ba[d:d + 1])
                i = _sigmoid(gates[:, (2 * d + 1) * LRU_W:(2 * d + 2) * LRU_W] + bi[d:d + 1])
                a = jnp.exp2(r * neg_c[d:d + 1])
                a2 = 1.0 - a * a
                b = jnp.where(a2 > 0.0, a2 * lax.rsqrt(a2), 0.0) * (i * xc)
                rows = pl.ds(r0 * G + g, GATE_ROWS, stride=G)
                for s in range(LRU_SLABS):
                    a_ref[d * LRU_SLABS + s, rows, :] = a[:, s * LANES:(s + 1) * LANES]
                    b_ref[d * LRU_SLABS + s, rows, :] = b[:, s * LANES:(s + 1) * LANES]
        return carry

    lax.fori_loop(0, G, gates_of_sequence, 0)

    def step(t, hs):
        out = []
        for d in range(2):
            tt = t if d == 0 else L - 1 - t
            rows = pl.ds(pl.multiple_of(tt * G, G), G)
            for s in range(LRU_SLABS):
                k = d * LRU_SLABS + s
                h = a_ref[k, rows, :] * hs[k] + b_ref[k, rows, :]
                h_ref[k, rows, :] = h
                out.append(h)
        return tuple(out)

    h0 = tuple(h0_ref[d, :, s * LANES:(s + 1) * LANES] for d in range(2) for s in range(LRU_SLABS))
    hs = lax.fori_loop(0, L, step, h0, unroll=2)
    for d in range(2):
        for s in range(LRU_SLABS):
            hl_ref[d, :, s * LANES:(s + 1) * LANES] = hs[d * LRU_SLABS + s]

    def write_sequence(g, carry):
        rows = pl.ds(g, L, stride=G)
        orow = pl.ds(pl.multiple_of(g * L, L), L)
        for s in range(LRU_SLABS):
            lanes = slice(s * LANES, (s + 1) * LANES)
            h = h_ref[s, rows, :] + h_ref[LRU_SLABS + s, rows, :]
            o_ref[orow, lanes] = (h * _silu(lg_ref[orow, lanes])).astype(BF16)
        return carry

    lax.fori_loop(0, G, write_sequence, 0)


def _lru(proj, conv_w, conv_b, wa, wi, ba, bi, lam, h0, layer, h0_layer, L, G):
    nseq = h0.shape[2]
    return pl.pallas_call(
        functools.partial(_lru_kernel, L=L, G=G),
        grid=(nseq // G,),
        in_specs=[
            pl.BlockSpec((None, G * L, LRU_W), lambda s: (S_LX_LG, s, 0)),
            pl.BlockSpec((None, G * L, LRU_W), lambda s: (S_LX_LG, s, 1)),
            _of_layer(layer, (4, LRU_W)),
            _of_layer(layer, (1, LRU_W)),
            _of_layer(layer, (2, LRU_BLOCKS, LRU_BW, LRU_BW)),
            _of_layer(layer, (2, LRU_BLOCKS, LRU_BW, LRU_BW)),
            _of_layer(layer, (2, LRU_W)),
            _of_layer(layer, (2, LRU_W)),
            _of_layer(layer, (2, LRU_W)),
            pl.BlockSpec((None, 2, G, LRU_W), lambda s: (h0_layer, 0, s, 0)),
        ],
        out_specs=[
            pl.BlockSpec((G * L, LRU_W), lambda s: (s, 0)),
            pl.BlockSpec((2, G, LRU_W), lambda s: (0, s, 0)),
        ],
        out_shape=[
            jax.ShapeDtypeStruct((nseq * L, LRU_W), BF16),
            jax.ShapeDtypeStruct((2, nseq, LRU_W), F32),
        ],
        scratch_shapes=[
            pltpu.VMEM((L + 2 * PAD, LRU_W), F32),
            pltpu.VMEM((2 * LRU_SLABS, L * G, LANES), F32),
            pltpu.VMEM((2 * LRU_SLABS, L * G, LANES), F32),
            pltpu.VMEM((2 * LRU_SLABS, L * G, LANES), F32),
            pltpu.VMEM((LRU_W, 4 * LRU_W), BF16),
        ],
        compiler_params=_params("arbitrary"),
        name=f"lru_{L}",
    )(proj, proj, conv_w, conv_b, wa, wi, ba, bi, lam, h0)


FILT_ROWS = 256


def _filt_kernel(z_ref, decay_ref, w1_ref, b1_ref, w2_ref, b2_ref, w3_ref, ft_ref, fb_ref, g_ref, taps_ref, *, L):
    i = pl.program_id(0)

    @pl.when(i == 0)
    def _():
        h = jnp.sin(_dot_hi(z_ref[...], w1_ref[...]) + b1_ref[...])
        h = jnp.sin(_dot_hi(h, w2_ref[...]) + b2_ref[...])
        filt = _dot_hi(h, w3_ref[...])
        decay = decay_ref[...]
        row = lax.broadcasted_iota(jnp.int32, (L, HY_W), 0)
        taps_ref[:, :HY_W] = (filt[:, :HY_W] * decay).astype(BF16)
        taps_ref[:, HY_W:] = jnp.where(row == 0, 0.0, filt[:, HY_W:] * decay).astype(BF16)

    taps = taps_ref[...]
    top = _dot(ft_ref[...], taps)
    bot = _dot(fb_ref[...], taps)
    first = (lax.broadcasted_iota(jnp.int32, (FILT_ROWS, HY_W), 0) + i * FILT_ROWS) == 0
    gt = top[:, :HY_W] + top[:, HY_W:]
    g_ref[0] = gt
    g_ref[1] = jnp.where(first, 0.0, bot[:, :HY_W] - bot[:, HY_W:])
    g_ref[2] = jnp.where(first, bot[:, :HY_W] + bot[:, HY_W:], gt)


def _filters(z, decay, w1, b1, w2, b2, w3, fwd_bf, layer, L):
    full = lambda shape: pl.BlockSpec(shape, lambda i: (0,) * len(shape))
    nblk = L // FILT_ROWS
    return pl.pallas_call(
        functools.partial(_filt_kernel, L=L),
        grid=(nblk,),
        in_specs=[
            full((L, HY_POS)), full((L, HY_W)),
            _of_layer(layer, (HY_POS, HY_HIDDEN)), _of_layer(layer, (1, HY_HIDDEN)),
            _of_layer(layer, (HY_HIDDEN, HY_HIDDEN)), _of_layer(layer, (1, HY_HIDDEN)),
            _of_layer(layer, (HY_HIDDEN, 2 * HY_W)),
            pl.BlockSpec((FILT_ROWS, L), lambda i: (i, 0)),
            pl.BlockSpec((FILT_ROWS, L), lambda i: (nblk + i, 0)),
        ],
        out_specs=pl.BlockSpec((3, FILT_ROWS, HY_W), lambda i: (0, i, 0)),
        out_shape=jax.ShapeDtypeStruct((3, L, HY_W), F32),
        scratch_shapes=[pltpu.VMEM((L, 2 * HY_W), BF16)],
        compiler_params=_params("arbitrary"),
        name=f"hyena_filters_{L}",
    )(z, decay, w1, b1, w2, b2, w3, fwd_bf, fwd_bf)


HY_PARAM_ROWS = 16


def _hyena_kernel(hv_ref, hx1_ref, hx0_ref, hg_ref, prm_ref, g_ref, f_ref, fi_ref, o_ref, xpad_ref, *, L, ch):
    zeros = jnp.zeros((PAD, ch), F32)
    xpad_ref[0:PAD, :] = zeros
    xpad_ref[PAD + L:, :] = zeros
    prm = prm_ref[...]

    def conv(ref, stream):
        xpad_ref[PAD:PAD + L, :] = ref[...]
        u = prm[9 + stream:10 + stream] + prm[3 * stream:3 * stream + 1] * xpad_ref[PAD - 1:PAD - 1 + L, :]
        u = u + prm[3 * stream + 1:3 * stream + 2] * xpad_ref[PAD:PAD + L, :]
        return u + prm[3 * stream + 2:3 * stream + 3] * xpad_ref[PAD + 1:PAD + 1 + L, :]

    z = conv(hx1_ref, 1) * conv(hv_ref, 0)
    spec = _dot(f_ref[...], z.astype(BF16))
    top, bot = spec[:L], spec[L:]
    gx = g_ref[1]
    yt = top * g_ref[0] - bot * gx
    yb = top * gx + bot * g_ref[2]
    y = _dot(fi_ref[...], jnp.concatenate([yt, yb], axis=0).astype(BF16))
    hy = conv(hx0_ref, 2) * (y + z * prm[12:13])
    o_ref[...] = (hy * _silu(hg_ref[...])).astype(BF16)


def _hyena(proj, prm, g, fwd, inv, layer, L, nseq, ch):
    nch = HY_W // ch
    full = lambda shape: pl.BlockSpec(shape, lambda s, j: (0,) * len(shape))
    col = lambda slab, off: pl.BlockSpec((None, L, ch), lambda s, j, slab=slab, off=off: (slab, s, off * nch + j))
    return pl.pallas_call(
        functools.partial(_hyena_kernel, L=L, ch=ch),
        grid=(nseq, nch),
        in_specs=[
            col(S_HV_HX1, 0), col(S_HV_HX1, 1), col(S_HX0_HG, 0), col(S_HX0_HG, 1),
            pl.BlockSpec((None, HY_PARAM_ROWS, ch), lambda s, j: (layer, 0, j)),
            pl.BlockSpec((3, L, ch), lambda s, j: (0, 0, j)),
            full((2 * L, L)), full((L, 2 * L)),
        ],
        out_specs=pl.BlockSpec((L, ch), lambda s, j: (s, j)),
        out_shape=jax.ShapeDtypeStruct((nseq * L, HY_W), BF16),
        scratch_shapes=[pltpu.VMEM((L + 2 * PAD, ch), F32)],
        compiler_params=_params("arbitrary", "arbitrary"),
        name=f"hyena_{L}",
    )(proj, proj, proj, proj, prm, g, fwd, inv)


def _hyena_params(conv_w, conv_b, d):
    rows = [conv_w[:, :, s * HY_W:(s + 1) * HY_W] for s in range(3)]
    rows += [conv_b[:, None, s * HY_W:(s + 1) * HY_W] for s in range(3)]
    rows += [d[:, None, :], jnp.zeros((DEPTH, HY_PARAM_ROWS - 13, HY_W), F32)]
    return jnp.concatenate(rows, axis=1)


OUT_TM = 512


def _outproj_kernel(x_ref, gate_ref, att_ref, lru_ref, hy_ref, w_ref, fg_ref, o_ref, *, final):
    acc = _dot(att_ref[...], w_ref[0:ATT_W, :])
    acc = acc + _dot(lru_ref[...], w_ref[ATT_W:ATT_W + LRU_W, :])
    acc = acc + _dot(hy_ref[...], w_ref[ATT_W + LRU_W:, :])
    y = x_ref[...] + gate_ref[...] * acc
    if final:
        ms = jnp.mean(y * y, axis=-1, keepdims=True)
        y = (y * lax.rsqrt(ms + EPS)) * fg_ref[...]
    o_ref[...] = y


def _outproj(x, mod4, att, lru, hy, w_out_bf, final_g, layer, final, seq_len):
    n_tok = x.shape[0]
    tm = OUT_TM
    cond = _cond_row(seq_len, tm)
    return pl.pallas_call(
        functools.partial(_outproj_kernel, final=final),
        grid=(n_tok // tm,),
        in_specs=[
            pl.BlockSpec((tm, D_MODEL), lambda i: (i, 0)),
            pl.BlockSpec((None, None, 1, D_MODEL), lambda i: (layer, cond(i), 0, 2)),
            pl.BlockSpec((tm, ATT_W), lambda i: (i, 0)),
            pl.BlockSpec((tm, LRU_W), lambda i: (i, 0)),
            pl.BlockSpec((tm, HY_W), lambda i: (i, 0)),
            pl.BlockSpec((None, D_MODEL, D_MODEL), lambda i: (layer, 0, 0), pipeline_mode=pl.Buffered(1)),
            pl.BlockSpec((1, D_MODEL), lambda i: (0, 0)),
        ],
        out_specs=pl.BlockSpec((tm, D_MODEL), lambda i: (i, 0)),
        out_shape=jax.ShapeDtypeStruct((n_tok, D_MODEL), F32),
        compiler_params=_params("arbitrary"),
        name="outproj",
    )(x, mod4, att, lru, hy, w_out_bf, final_g)


def kernel(x_prompt, x_sample, cache_k, cache_v, state_lru, c, c_ctx, norm_g, w_ada, b_ada, w_in, w_out, lam_q1, lam_k1, lam_q2, lam_k2, attn_subln_g, lru_conv_w, lru_conv_b, lru_wa, lru_ba, lru_wi, lru_bi, lru_lam, hy_conv_w, hy_conv_b, hy_w1, hy_b1, hy_w2, hy_b2, hy_w3, hy_d, final_g):
    cos, sin = _rope_tables()
    tables = {}
    for L in (SEQ, DEC_SEQ):
        fwd, inv = _dft_tables(L)
        z, decay = _hyena_tables(L)
        tables[L] = (fwd.astype(BF16), inv.astype(BF16), z, decay)

    cond = jnp.concatenate([c_ctx[None, :], c, jnp.zeros((N_COND - 1 - DEC_BATCH, D_MODEL), F32)], axis=0)
    mod4 = _ada(cond, w_ada, b_ada).reshape(DEPTH, N_COND, 1, 3 * D_MODEL)

    xc = x_prompt.reshape(T_CTX, D_MODEL)
    xs = x_sample.reshape(T_SMP, D_MODEL)
    ck = cache_k.reshape(DEC_BATCH, DEPTH, PAST_LEN * N_HEADS, HEAD_W)
    cv = cache_v.reshape(DEC_BATCH, DEPTH, PAST_LEN * N_HEADS, HEAD_W)
    h0_ctx = jnp.zeros((1, 2, BATCH, LRU_W), F32)
    h0_smp = jnp.transpose(state_lru, (1, 2, 0, 3))
    norm_g3 = norm_g[:, None, :]
    lamp = jnp.stack([lam_q1, lam_k1, lam_q2, lam_k2], axis=1)
    subln = attn_subln_g[:, None, :]
    w_in_p = w_in.astype(BF16)
    w_out_bf = w_out.astype(BF16)
    lru_args = (lru_conv_w, lru_conv_b[:, None, :], lru_wa, lru_wi, lru_ba, lru_bi, lru_lam)
    hy_prm = _hyena_params(hy_conv_w, hy_conv_b, hy_d)
    filt_args = (hy_w1, hy_b1[:, None, :], hy_w2, hy_b2[:, None, :], hy_w3)
    fg = final_g[None, :]

    assert DEPTH == 2
    caches, hs = None, []
    for l in range(DEPTH):
        final = l == DEPTH - 1
        lam_init = 0.8 - 0.6 * math.exp(-0.3 * l)

        fwd, inv, z, decay = tables[SEQ]
        proj, *caches = _inproj_cache(xc, norm_g3, mod4, w_in_p, l, caches)
        att = _attn_ctx(proj, lamp, subln, l, lam_init)
        lru, h_last = _lru(proj, *lru_args, h0_ctx, l, 0, SEQ, 8)
        hs.append(h_last)
        g = _filters(z, decay, *filt_args, fwd, l, SEQ)
        hy = _hyena(proj, hy_prm, g, fwd, inv, l, SEQ, BATCH, HY_W)
        xc = _outproj(xc, mod4, att, lru, hy, w_out_bf, fg, l, final, None)

        fwd, inv, z, decay = tables[DEC_SEQ]
        proj = _inproj(xs, norm_g3, mod4, w_in_p, l, DEC_SEQ)
        att = _attn_smp(proj, ck, cv, lamp, subln, cos, sin, l, lam_init)
        lru, _ = _lru(proj, *lru_args, h0_smp, l, l, DEC_SEQ, DEC_BATCH)
        g = _filters(z, decay, *filt_args, fwd, l, DEC_SEQ)
        hy = _hyena(proj, hy_prm, g, fwd, inv, l, DEC_SEQ, DEC_BATCH, HY_W // 2)
        xs = _outproj(xs, mod4, att, lru, hy, w_out_bf, fg, l, final, DEC_SEQ)

    y_prompt = xc.reshape(BATCH, SEQ, D_MODEL)
    y_sample = xs.reshape(DEC_BATCH, DEC_SEQ, D_MODEL)
    new_k, new_v = (a.reshape(BATCH, DEPTH, SEQ, N_HEADS, HEAD_W) for a in caches)
    new_state = jnp.transpose(jnp.stack(hs, axis=0), (2, 0, 1, 3))
    return (y_prompt, y_sample, new_k, new_v, new_state)
```

```python
import functools
import math

import numpy as np
import jax
import jax.numpy as jnp
from jax import lax
from jax.experimental import pallas as pl
from jax.experimental.pallas import tpu as pltpu

D_MODEL = 2048
BATCH = 32
SEQ = 256
DEPTH = 2
DEC_BATCH = 2
DEC_SEQ = 1024
PAST_LEN = 512
GRID_W = 64
ATT_W = 1024
LRU_W = 512
HY_W = 512
HEAD_DIM = 64
N_HEADS = 8
HEAD_W = 2 * HEAD_DIM
LRU_BLOCKS = 8
LRU_BW = LRU_W // LRU_BLOCKS
LRU_C = 8.0
HY_BANDS = 16
HY_POS = 1 + 2 * HY_BANDS
HY_HIDDEN = 64
HY_DECAY_FAST = 0.3
HY_DECAY_SLOW = 1.5
HY_DECAY_TARGET = 1e-2
ROPE_BASE = 10000.0
EPS = 1e-6
IN_W = 4 * ATT_W + 2 * LRU_W + 4 * HY_W

T_CTX = BATCH * SEQ
T_SMP = DEC_BATCH * DEC_SEQ
N_COND = 8

SLAB_W = 1024
N_SLABS = IN_W // SLAB_W
S_Q, S_K, S_V, S_AG, S_LX_LG, S_HV_HX1, S_HX0_HG = range(N_SLABS)

LANES = 128
F32 = jnp.float32
BF16 = jnp.bfloat16
VMEM_LIMIT = 58 * 1024 * 1024


LOG2E = 1.4426950408889634


def _sigmoid(x):
    return 1.0 / (1.0 + jnp.exp2(x * (-LOG2E)))


def _silu(x):
    return x * _sigmoid(x)


def _dot(a, b):
    return jnp.dot(a, b, preferred_element_type=F32)


def _dot_nt(a, b):
    return lax.dot_general(a, b, (((1,), (1,)), ((), ())), preferred_element_type=F32)


def _dot_hi(a, b):
    return jnp.dot(a, b, precision=lax.Precision.HIGHEST, preferred_element_type=F32)


def _params(*sem):
    return pltpu.CompilerParams(dimension_semantics=sem, vmem_limit_bytes=VMEM_LIMIT)


def _rope_tables():
    t = np.arange(DEC_SEQ)
    pos = np.stack([t // GRID_W, t % GRID_W], axis=1).astype(np.float64)
    nf = HEAD_DIM // 4
    inv = ROPE_BASE ** (-np.arange(nf, dtype=np.float64) / nf)
    lane = np.arange(HEAD_W)
    j = lane % HEAD_DIM
    axis = j // (HEAD_DIM // 2)
    f = j % nf
    upper = (j % (HEAD_DIM // 2)) >= nf
    ang = pos[:, axis] * inv[f][None, :]
    cos = np.cos(ang)
    sin = np.sin(ang) * np.where(upper, 1.0, -1.0)[None, :]
    return jnp.asarray(cos, F32), jnp.asarray(sin, F32)


def _dft_tables(L):
    n = 2 * L
    k = np.arange(L)[:, None]
    t = np.arange(L)[None, :]
    ang = 2.0 * np.pi * ((k * t) % n).astype(np.float64) / n
    fwd = np.concatenate([np.cos(ang), -np.sin(ang)], axis=0)
    fwd[L, :] = (-1.0) ** np.arange(L)
    wk = np.where(np.arange(L) == 0, 1.0, 2.0)[None, :]
    ang_t = ang.T
    inv = np.concatenate([wk * np.cos(ang_t), -2.0 * np.sin(ang_t)], axis=1) / n
    inv[:, L] = ((-1.0) ** np.arange(L)) / n
    return jnp.asarray(fwd, F32), jnp.asarray(inv, F32)


def _hyena_tables(L):
    pos = np.arange(L, dtype=np.float64)
    t = pos / float(max(L - 1, 1))
    bands = np.linspace(1e-4, HY_BANDS - 1, HY_BANDS)
    ang = (2.0 * math.pi / L) * pos[:, None] * bands[None, :]
    z = np.concatenate([t[:, None], np.cos(ang), np.sin(ang)], axis=-1)
    lo = abs(math.log(HY_DECAY_TARGET) / HY_DECAY_SLOW)
    hi = abs(math.log(HY_DECAY_TARGET) / HY_DECAY_FAST)
    deltas = np.linspace(lo, hi, HY_W)
    decay = np.exp(-t[:, None] * deltas[None, :])
    return jnp.asarray(z, F32), jnp.asarray(decay, F32)


def _ada_kernel(c_ref, w_ref, b_ref, o_ref):
    s = _silu(c_ref[...])
    o_ref[...] = _dot(s.astype(BF16), w_ref[...].astype(BF16)) + b_ref[...]


def _ada(cond, w_ada, b_ada):
    tn = 1024
    return pl.pallas_call(
        _ada_kernel,
        grid=(DEPTH, 3 * D_MODEL // tn),
        in_specs=[
            pl.BlockSpec((N_COND, D_MODEL), lambda l, j: (0, 0)),
            pl.BlockSpec((None, D_MODEL, tn), lambda l, j: (l, 0, j)),
            pl.BlockSpec((None, 1, tn), lambda l, j: (l, 0, j)),
        ],
        out_specs=pl.BlockSpec((None, N_COND, tn), lambda l, j: (l, 0, j)),
        out_shape=jax.ShapeDtypeStruct((DEPTH, N_COND, 3 * D_MODEL), F32),
        compiler_params=_params("arbitrary", "arbitrary"),
        name="ada",
    )(cond, w_ada, b_ada.reshape(DEPTH, 1, 3 * D_MODEL))


def _cond_row(seq_len, tm):
    if seq_len is None:
        return lambda i: 0
    return lambda i: 1 + i // (seq_len // tm)


IN_TM = 1024
NORM_ROWS = 32


def _norm_modulate(x_ref, g_ref, shift_ref, scale_ref, h_ref, h_row0=0):
    g = g_ref[...]
    sc = 1.0 + scale_ref[...]
    sh = shift_ref[...]

    def body(r, carry):
        start = pl.multiple_of(r * NORM_ROWS, NORM_ROWS)
        x = x_ref[pl.ds(start, NORM_ROWS), :]
        ms = jnp.mean(x * x, axis=-1, keepdims=True)
        xn = x * lax.rsqrt(ms + EPS)
        h_ref[pl.ds(pl.multiple_of(h_row0 + start, NORM_ROWS), NORM_ROWS), :] = ((xn * g) * sc + sh).astype(BF16)
        return carry

    lax.fori_loop(0, IN_TM // NORM_ROWS, body, 0, unroll=4)


def _inproj_cache_kernel(*refs, first_layer):
    if first_layer:
        x_ref, g_ref, shift_ref, scale_ref, w_ref, o_ref, kc_ref, vc_ref, h_ref = refs
    else:
        x_ref, g_ref, shift_ref, scale_ref, w_ref, _, _, o_ref, kc_ref, vc_ref, h_ref = refs
    j = pl.program_id(1)

    @pl.when(j == 0)
    def _():
        _norm_modulate(x_ref, g_ref, shift_ref, scale_ref, h_ref)

    o_ref[...] = _dot(h_ref[...], w_ref[...])

    def scatter(dst_ref):
        for b in range(IN_TM // SEQ):
            for h in range(N_HEADS):
                dst_ref[b, pl.ds(h, SEQ, stride=N_HEADS), :] = o_ref[b * SEQ:(b + 1) * SEQ, h * HEAD_W:(h + 1) * HEAD_W]

    @pl.when(j == S_K)
    def _():
        scatter(kc_ref)

    @pl.when(j == S_V)
    def _():
        scatter(vc_ref)

    if first_layer:
        @pl.when(j == S_K + 1)
        def _():
            kc_ref[...] = jnp.zeros(kc_ref.shape, F32)

        @pl.when(j == S_V + 1)
        def _():
            vc_ref[...] = jnp.zeros(vc_ref.shape, F32)


def _of_layer(layer, shape):
    return pl.BlockSpec((None,) + tuple(shape), lambda *_: (layer,) + (0,) * len(shape))


def _inproj_specs(tm, layer, cond, w_spec):
    return [
        pl.BlockSpec((tm, D_MODEL), lambda i, j: (i, 0)),
        _of_layer(layer, (1, D_MODEL)),
        pl.BlockSpec((None, None, 1, D_MODEL), lambda i, j: (layer, cond(i), 0, 0)),
        pl.BlockSpec((None, None, 1, D_MODEL), lambda i, j: (layer, cond(i), 0, 1)),
        w_spec,
    ]


W32_K_CHUNK = 512


def _inproj_w32_kernel(x_ref, g_ref, shift_ref, scale_ref, w_ref, o_ref, wbf_ref, h_ref):
    row0 = pl.multiple_of(pl.program_id(1) * IN_TM, IN_TM)

    @pl.when(pl.program_id(0) == 0)
    def _():
        _norm_modulate(x_ref, g_ref, shift_ref, scale_ref, h_ref, row0)

    for k0 in range(0, D_MODEL, W32_K_CHUNK):
        w = w_ref[k0:k0 + W32_K_CHUNK, :].astype(BF16)
        wbf_ref[k0:k0 + W32_K_CHUNK, :] = w
        part = _dot(h_ref[pl.ds(row0, IN_TM), k0:k0 + W32_K_CHUNK], w)
        if k0 == 0:
            o_ref[...] = part
        else:
            o_ref[...] += part


def _inproj_w32(x, norm_g, mod4, w_in, layer, seq_len):
    n_tok = x.shape[0]
    tm = IN_TM
    n_tiles = n_tok // tm
    cond = _cond_row(seq_len, tm)
    x_row = lambda j, i: jnp.where(j == 0, i, n_tiles - 1)
    return pl.pallas_call(
        _inproj_w32_kernel,
        grid=(N_SLABS, n_tiles),
        in_specs=[
            pl.BlockSpec((tm, D_MODEL), lambda j, i: (x_row(j, i), 0), pipeline_mode=pl.Buffered(1)),
            _of_layer(layer, (1, D_MODEL)),
            pl.BlockSpec((None, None, 1, D_MODEL), lambda j, i: (layer, cond(i), 0, 0)),
            pl.BlockSpec((None, None, 1, D_MODEL), lambda j, i: (layer, cond(i), 0, 1)),
            pl.BlockSpec((None, D_MODEL, SLAB_W), lambda j, i: (layer, 0, j)),
        ],
        out_specs=[
            pl.BlockSpec((None, tm, SLAB_W), lambda j, i: (j, i, 0)),
            pl.BlockSpec((D_MODEL, SLAB_W), lambda j, i: (0, j)),
        ],
        out_shape=[jax.ShapeDtypeStruct((N_SLABS, n_tok, SLAB_W), F32), jax.ShapeDtypeStruct((D_MODEL, IN_W), BF16)],
        scratch_shapes=[pltpu.VMEM((n_tok, D_MODEL), BF16)],
        compiler_params=_params("arbitrary", "arbitrary"),
        name="inproj_w32",
    )(x, norm_g, mod4, mod4, w_in)


def _inproj_cache(x, norm_g, mod4, w_in_bf, layer, caches):
    tm = IN_TM
    nb = tm // SEQ
    first = caches is None
    cache_shape = jax.ShapeDtypeStruct((BATCH, DEPTH, SEQ * N_HEADS, HEAD_W), F32)
    if first:
        half = lambda slab: (lambda i, j: (i, jnp.where(j <= slab, layer, layer + 1), 0, 0))
    else:
        half = lambda slab: (lambda i, j: (i, layer, 0, 0))
    cache_spec = lambda slab: pl.BlockSpec((nb, None, SEQ * N_HEADS, HEAD_W), half(slab))
    w_spec = pl.BlockSpec((D_MODEL, SLAB_W), lambda i, j: (0, j))
    in_specs = _inproj_specs(tm, layer, _cond_row(None, tm), w_spec)
    args = (x, norm_g, mod4, mod4, w_in_bf)
    aliases = {}
    if not first:
        in_specs += [pl.BlockSpec(memory_space=pl.ANY)] * 2
        aliases = {len(args): 1, len(args) + 1: 2}
        args += tuple(caches)
    return pl.pallas_call(
        functools.partial(_inproj_cache_kernel, first_layer=first),
        grid=(T_CTX // tm, N_SLABS),
        in_specs=in_specs,
        out_specs=[pl.BlockSpec((None, tm, SLAB_W), lambda i, j: (j, i, 0)), cache_spec(S_K), cache_spec(S_V)],
        out_shape=[jax.ShapeDtypeStruct((N_SLABS, T_CTX, SLAB_W), F32), cache_shape, cache_shape],
        scratch_shapes=[pltpu.VMEM((tm, D_MODEL), BF16)],
        input_output_aliases=aliases,
        compiler_params=_params("arbitrary", "arbitrary"),
        name="inproj_cache",
    )(*args)


def _lam_value(lamp_ref, lam_init):
    lp = lamp_ref[...]
    t1 = jnp.sum(lp[0:1] * lp[1:2], axis=-1, keepdims=True)
    t2 = jnp.sum(lp[2:3] * lp[3:4], axis=-1, keepdims=True)
    return jnp.exp(t1) - jnp.exp(t2) + lam_init


def _diff_attend(items, lam, g, lam_init):
    m = items[0][0].shape[0]
    lane = lax.broadcasted_iota(jnp.int32, (m, HEAD_W), 1)
    ss = []
    for q, k_bf, _ in items:
        q1 = jnp.where(lane < HEAD_DIM, q, 0.0).astype(BF16)
        q2 = jnp.where(lane < HEAD_DIM, 0.0, q).astype(BF16)
        ss.append(_dot_nt(jnp.concatenate([q1, q2], axis=0), k_bf))
    es = [jnp.exp(s - jnp.max(s, axis=-1, keepdims=True)) for s in ss]
    ps = [e * (1.0 / jnp.sum(e, axis=-1, keepdims=True)) for e in es]
    ws = [(p[:m] - lam * p[m:]).astype(BF16) for p in ps]
    os_ = [_dot(w, v_bf) for w, (_, _, v_bf) in zip(ws, items)]
    outs = []
    for o in os_:
        ms = jnp.mean(o * o, axis=-1, keepdims=True)
        outs.append((o * lax.rsqrt(ms + EPS) * g) * (1.0 - lam_init))
    return outs


CTX_BATCHES = 2


def _attn_ctx_kernel(lamp_ref, g_ref, q_ref, k_ref, v_ref, ag_ref, o_ref, *, lam_init):
    lam = _lam_value(lamp_ref, lam_init)
    where = [(slice(b * SEQ, (b + 1) * SEQ), slice(h * HEAD_W, (h + 1) * HEAD_W))
             for b in range(CTX_BATCHES) for h in range(N_HEADS)]
    items = [(q_ref[r, c] * (HEAD_DIM ** -0.5), k_ref[r, c].astype(BF16), v_ref[r, c].astype(BF16)) for r, c in where]
    for (r, c), o in zip(where, _diff_attend(items, lam, g_ref[...], lam_init)):
        o_ref[r, c] = (o * _silu(ag_ref[r, c])).astype(BF16)


def _attn_ctx(proj, lamp, subln_g, layer, lam_init):
    rows = CTX_BATCHES * SEQ
    slab = lambda s: pl.BlockSpec((None, rows, ATT_W), lambda b, s=s: (s, b, 0))
    return pl.pallas_call(
        functools.partial(_attn_ctx_kernel, lam_init=lam_init),
        grid=(BATCH // CTX_BATCHES,),
        in_specs=[
            _of_layer(layer, (4, HEAD_DIM)),
            _of_layer(layer, (1, HEAD_W)),
            slab(S_Q), slab(S_K), slab(S_V), slab(S_AG),
        ],
        out_specs=pl.BlockSpec((rows, ATT_W), lambda b: (b, 0)),
        out_shape=jax.ShapeDtypeStruct((T_CTX, ATT_W), BF16),
        compiler_params=_params("arbitrary"),
        name="attn_ctx",
    )(lamp, subln_g, proj, proj, proj, proj)


Q_CHUNK = 512
Q_PART = 256
N_QC = DEC_SEQ // Q_CHUNK


def _rope(x, cos, sin):
    lane = lax.broadcasted_iota(jnp.int32, x.shape, 1)
    lower = (lane % (HEAD_DIM // 2)) < (HEAD_DIM // 4)
    nf = HEAD_DIM // 4
    partner = jnp.where(lower, pltpu.roll(x, HEAD_W - nf, 1), pltpu.roll(x, nf, 1))
    return x * cos + partner * sin


def _attn_smp_kernel(lamp_ref, g_ref, cosq_ref, sinq_ref, cos_ref, sin_ref, q_ref, k_ref, v_ref,
                     ck_ref, cv_ref, ag_ref, o_ref, ks_ref, vs_ref, *, lam_init):
    @pl.when(pl.program_id(2) == 0)
    def _():
        head_rows = pl.ds(pl.program_id(1), PAST_LEN, stride=N_HEADS)
        ks_ref[0:PAST_LEN, :] = ck_ref[head_rows, :].astype(BF16)
        vs_ref[0:PAST_LEN, :] = cv_ref[head_rows, :].astype(BF16)
        ks_ref[PAST_LEN:, :] = _rope(k_ref[...], cos_ref[...], sin_ref[...]).astype(BF16)
        vs_ref[PAST_LEN:, :] = v_ref[...].astype(BF16)

    lam = _lam_value(lamp_ref, lam_init)
    q = _rope(q_ref[...], cosq_ref[...], sinq_ref[...]) * (HEAD_DIM ** -0.5)
    k_bf, v_bf = ks_ref[...], vs_ref[...]
    parts = [slice(r, r + Q_PART) for r in range(0, Q_CHUNK, Q_PART)]
    items = [(q[p], k_bf, v_bf) for p in parts]
    for p, o in zip(parts, _diff_attend(items, lam, g_ref[...], lam_init)):
        o_ref[p, :] = (o * _silu(ag_ref[p, :])).astype(BF16)


def _attn_smp(proj, cache_k, cache_v, lamp, subln_g, cos, sin, layer, lam_init):
    full = lambda shape: pl.BlockSpec(shape, lambda b, h, c: (0, 0))
    return pl.pallas_call(
        functools.partial(_attn_smp_kernel, lam_init=lam_init),
        grid=(DEC_BATCH, N_HEADS, N_QC),
        in_specs=[
            _of_layer(layer, (4, HEAD_DIM)),
            _of_layer(layer, (1, HEAD_W)),
            pl.BlockSpec((Q_CHUNK, HEAD_W), lambda b, h, c: (c, 0)),
            pl.BlockSpec((Q_CHUNK, HEAD_W), lambda b, h, c: (c, 0)),
            full((DEC_SEQ, HEAD_W)),
            full((DEC_SEQ, HEAD_W)),
            pl.BlockSpec((None, Q_CHUNK, HEAD_W), lambda b, h, c: (S_Q, b * N_QC + c, h)),
            pl.BlockSpec((None, DEC_SEQ, HEAD_W), lambda b, h, c: (S_K, b, h)),
            pl.BlockSpec((None, DEC_SEQ, HEAD_W), lambda b, h, c: (S_V, b, h)),
            pl.BlockSpec((None, None, PAST_LEN * N_HEADS, HEAD_W), lambda b, h, c: (b, layer, 0, 0)),
            pl.BlockSpec((None, None, PAST_LEN * N_HEADS, HEAD_W), lambda b, h, c: (b, layer, 0, 0)),
            pl.BlockSpec((None, Q_CHUNK, HEAD_W), lambda b, h, c: (S_AG, b * N_QC + c, h)),
        ],
        out_specs=pl.BlockSpec((Q_CHUNK, HEAD_W), lambda b, h, c: (b * N_QC + c, h)),
        out_shape=jax.ShapeDtypeStruct((T_SMP, ATT_W), BF16),
        scratch_shapes=[
            pltpu.VMEM((PAST_LEN + DEC_SEQ, HEAD_W), BF16),
            pltpu.VMEM((PAST_LEN + DEC_SEQ, HEAD_W), BF16),
        ],
        compiler_params=_params("arbitrary", "arbitrary", "arbitrary"),
        name="attn_smp",
    )(lamp, subln_g, cos, sin, cos, sin, proj, proj, proj, cache_k, cache_v, proj)


PAD = 8
GATE_ROWS = 256
LRU_SLABS = LRU_W // LANES


def _lru_kernel(x_ref, lg_ref, cw_ref, cb_ref, wa_ref, wi_ref, ba_ref, bi_ref, lam_ref, h0_ref, o_ref, hl_ref,
                xpad_ref, a_ref, b_ref, h_ref, w_ref, *, L, G):
    @pl.when(pl.program_id(0) == 0)
    def _():
        w_ref[...] = jnp.zeros(w_ref.shape, BF16)
        for d in range(2):
            for gate, src in enumerate((wa_ref, wi_ref)):
                col0 = (2 * d + gate) * LRU_W
                for n in range(LRU_BLOCKS):
                    lo = n * LRU_BW
                    w_ref[lo:lo + LRU_BW, col0 + lo:col0 + lo + LRU_BW] = src[d, n].astype(BF16)

    zeros = jnp.zeros((PAD, LRU_W), F32)
    xpad_ref[0:PAD, :] = zeros
    xpad_ref[PAD + L:, :] = zeros
    nl = -lam_ref[...]
    neg_c = (-LRU_C * LOG2E) * (jnp.maximum(nl, 0.0) + jnp.log1p(jnp.exp(-jnp.abs(nl))))
    cw = cw_ref[...]
    cb = cb_ref[...]
    ba = ba_ref[...]
    bi = bi_ref[...]

    def gates_of_sequence(g, carry):
        xpad_ref[PAD:PAD + L, :] = x_ref[pl.ds(pl.multiple_of(g * L, L), L), :]
        for c in range(L // GATE_ROWS):
            r0 = c * GATE_ROWS
            xc = cb + cw[0:1] * xpad_ref[PAD - 2 + r0:PAD - 2 + r0 + GATE_ROWS, :]
            xc = xc + cw[1:2] * xpad_ref[PAD - 1 + r0:PAD - 1 + r0 + GATE_ROWS, :]
            xc = xc + cw[2:3] * xpad_ref[PAD + r0:PAD + r0 + GATE_ROWS, :]
            xc = xc + cw[3:4] * xpad_ref[PAD + 1 + r0:PAD + 1 + r0 + GATE_ROWS, :]
            gates = _dot(xc.astype(BF16), w_ref[...])
            for d in range(2):
                r = _sigmoid(gates[:, (2 * d) * LRU_W:(2 * d + 1) * LRU_W] + ba[d:d + 1])
                i = _sigmoid(gates[:, (2 * d + 1) * LRU_W:(2 * d + 2) * LRU_W] + bi[d:d + 1])
                a = jnp.exp2(r * neg_c[d:d + 1])
                a2 = 1.0 - a * a
                b = jnp.where(a2 > 0.0, a2 * lax.rsqrt(a2), 0.0) * (i * xc)
                rows = pl.ds(r0 * G + g, GATE_ROWS, stride=G)
                for s in range(LRU_SLABS):
                    a_ref[d * LRU_SLABS + s, rows, :] = a[:, s * LANES:(s + 1) * LANES]
                    b_ref[d * LRU_SLABS + s, rows, :] = b[:, s * LANES:(s + 1) * LANES]
        return carry

    lax.fori_loop(0, G, gates_of_sequence, 0)

    def step(t, hs):
        out = []
        for d in range(2):
            tt = t if d == 0 else L - 1 - t
            rows = pl.ds(pl.multiple_of(tt * G, G), G)
            for s in range(LRU_SLABS):
                k = d * LRU_SLABS + s
                h = a_ref[k, rows, :] * hs[k] + b_ref[k, rows, :]
                h_ref[k, rows, :] = h
                out.append(h)
        return tuple(out)

    h0 = tuple(h0_ref[d, :, s * LANES:(s + 1) * LANES] for d in range(2) for s in range(LRU_SLABS))
    hs = lax.fori_loop(0, L, step, h0, unroll=2)
    for d in range(2):
        for s in range(LRU_SLABS):
            hl_ref[d, :, s * LANES:(s + 1) * LANES] = hs[d * LRU_SLABS + s]

    def write_sequence(g, carry):
        rows = pl.ds(g, L, stride=G)
        orow = pl.ds(pl.multiple_of(g * L, L), L)
        for s in range(LRU_SLABS):
            lanes = slice(s * LANES, (s + 1) * LANES)
            h = h_ref[s, rows, :] + h_ref[LRU_SLABS + s, rows, :]
            o_ref[orow, lanes] = (h * _silu(lg_ref[orow, lanes])).astype(BF16)
        return carry

    lax.fori_loop(0, G, write_sequence, 0)


def _lru(proj, conv_w, conv_b, wa, wi, ba, bi, lam, h0, layer, h0_layer, L, G):
    nseq = h0.shape[2]
    return pl.pallas_call(
        functools.partial(_lru_kernel, L=L, G=G),
        grid=(nseq // G,),
        in_specs=[
            pl.BlockSpec((None, G * L, LRU_W), lambda s: (S_LX_LG, s, 0)),
            pl.BlockSpec((None, G * L, LRU_W), lambda s: (S_LX_LG, s, 1)),
            _of_layer(layer, (4, LRU_W)),
            _of_layer(layer, (1, LRU_W)),
            _of_layer(layer, (2, LRU_BLOCKS, LRU_BW, LRU_BW)),
            _of_layer(layer, (2, LRU_BLOCKS, LRU_BW, LRU_BW)),
            _of_layer(layer, (2, LRU_W)),
            _of_layer(layer, (2, LRU_W)),
            _of_layer(layer, (2, LRU_W)),
            pl.BlockSpec((None, 2, G, LRU_W), lambda s: (h0_layer, 0, s, 0)),
        ],
        out_specs=[
            pl.BlockSpec((G * L, LRU_W), lambda s: (s, 0)),
            pl.BlockSpec((2, G, LRU_W), lambda s: (0, s, 0)),
        ],
        out_shape=[
            jax.ShapeDtypeStruct((nseq * L, LRU_W), BF16),
            jax.ShapeDtypeStruct((2, nseq, LRU_W), F32),
        ],
        scratch_shapes=[
            pltpu.VMEM((L + 2 * PAD, LRU_W), F32),
            pltpu.VMEM((2 * LRU_SLABS, L * G, LANES), F32),
            pltpu.VMEM((2 * LRU_SLABS, L * G, LANES), F32),
            pltpu.VMEM((2 * LRU_SLABS, L * G, LANES), F32),
            pltpu.VMEM((LRU_W, 4 * LRU_W), BF16),
        ],
        compiler_params=_params("arbitrary"),
        name=f"lru_{L}",
    )(proj, proj, conv_w, conv_b, wa, wi, ba, bi, lam, h0)


FILT_ROWS = 256


def _filt_kernel(z_ref, decay_ref, w1_ref, b1_ref, w2_ref, b2_ref, w3_ref, ft_ref, fb_ref, g_ref, taps_ref, *, L):
    i = pl.program_id(0)

    @pl.when(i == 0)
    def _():
        h = jnp.sin(_dot_hi(z_ref[...], w1_ref[...]) + b1_ref[...])
        h = jnp.sin(_dot_hi(h, w2_ref[...]) + b2_ref[...])
        filt = _dot_hi(h, w3_ref[...])
        decay = decay_ref[...]
        row = lax.broadcasted_iota(jnp.int32, (L, HY_W), 0)
        taps_ref[:, :HY_W] = (filt[:, :HY_W] * decay).astype(BF16)
        taps_ref[:, HY_W:] = jnp.where(row == 0, 0.0, filt[:, HY_W:] * decay).astype(BF16)

    taps = taps_ref[...]
    top = _dot(ft_ref[...], taps)
    bot = _dot(fb_ref[...], taps)
    first = (lax.broadcasted_iota(jnp.int32, (FILT_ROWS, HY_W), 0) + i * FILT_ROWS) == 0
    gt = top[:, :HY_W] + top[:, HY_W:]
    g_ref[0] = gt
    g_ref[1] = jnp.where(first, 0.0, bot[:, :HY_W] - bot[:, HY_W:])
    g_ref[2] = jnp.where(first, bot[:, :HY_W] + bot[:, HY_W:], gt)


def _filters(z, decay, w1, b1, w2, b2, w3, fwd_bf, layer, L):
    full = lambda shape: pl.BlockSpec(shape, lambda i: (0,) * len(shape))
    nblk = L // FILT_ROWS
    return pl.pallas_call(
        functools.partial(_filt_kernel, L=L),
        grid=(nblk,),
        in_specs=[
            full((L, HY_POS)), full((L, HY_W)),
            _of_layer(layer, (HY_POS, HY_HIDDEN)), _of_layer(layer, (1, HY_HIDDEN)),
            _of_layer(layer, (HY_HIDDEN, HY_HIDDEN)), _of_layer(layer, (1, HY_HIDDEN)),
            _of_layer(layer, (HY_HIDDEN, 2 * HY_W)),
            pl.BlockSpec((FILT_ROWS, L), lambda i: (i, 0)),
            pl.BlockSpec((FILT_ROWS, L), lambda i: (nblk + i, 0)),
        ],
        out_specs=pl.BlockSpec((3, FILT_ROWS, HY_W), lambda i: (0, i, 0)),
        out_shape=jax.ShapeDtypeStruct((3, L, HY_W), F32),
        scratch_shapes=[pltpu.VMEM((L, 2 * HY_W), BF16)],
        compiler_params=_params("arbitrary"),
        name=f"hyena_filters_{L}",
    )(z, decay, w1, b1, w2, b2, w3, fwd_bf, fwd_bf)


HY_PARAM_ROWS = 16


def _hyena_kernel(hv_ref, hx1_ref, hx0_ref, hg_ref, prm_ref, g_ref, f_ref, fi_ref, o_ref, xpad_ref, *, L, ch):
    zeros = jnp.zeros((PAD, ch), F32)
    xpad_ref[0:PAD, :] = zeros
    xpad_ref[PAD + L:, :] = zeros
    prm = prm_ref[...]

    def conv(ref, stream):
        xpad_ref[PAD:PAD + L, :] = ref[...]
        u = prm[9 + stream:10 + stream] + prm[3 * stream:3 * stream + 1] * xpad_ref[PAD - 1:PAD - 1 + L, :]
        u = u + prm[3 * stream + 1:3 * stream + 2] * xpad_ref[PAD:PAD + L, :]
        return u + prm[3 * stream + 2:3 * stream + 3] * xpad_ref[PAD + 1:PAD + 1 + L, :]

    z = conv(hx1_ref, 1) * conv(hv_ref, 0)
    spec = _dot(f_ref[...], z.astype(BF16))
    top, bot = spec[:L], spec[L:]
    gx = g_ref[1]
    yt = top * g_ref[0] - bot * gx
    yb = top * gx + bot * g_ref[2]
    y = _dot(fi_ref[...], jnp.concatenate([yt, yb], axis=0).astype(BF16))
    hy = conv(hx0_ref, 2) * (y + z * prm[12:13])
    o_ref[...] = (hy * _silu(hg_ref[...])).astype(BF16)


def _hyena(proj, prm, g, fwd, inv, layer, L, nseq, ch):
    nch = HY_W // ch
    full = lambda shape: pl.BlockSpec(shape, lambda s, j: (0,) * len(shape))
    col = lambda slab, off: pl.BlockSpec((None, L, ch), lambda s, j, slab=slab, off=off: (slab, s, off * nch + j))
    return pl.pallas_call(
        functools.partial(_hyena_kernel, L=L, ch=ch),
        grid=(nseq, nch),
        in_specs=[
            col(S_HV_HX1, 0), col(S_HV_HX1, 1), col(S_HX0_HG, 0), col(S_HX0_HG, 1),
            pl.BlockSpec((None, HY_PARAM_ROWS, ch), lambda s, j: (layer, 0, j)),
            pl.BlockSpec((3, L, ch), lambda s, j: (0, 0, j)),
            full((2 * L, L)), full((L, 2 * L)),
        ],
        out_specs=pl.BlockSpec((L, ch), lambda s, j: (s, j)),
        out_shape=jax.ShapeDtypeStruct((nseq * L, HY_W), BF16),
        scratch_shapes=[pltpu.VMEM((L + 2 * PAD, ch), F32)],
        compiler_params=_params("arbitrary", "arbitrary"),
        name=f"hyena_{L}",
    )(proj, proj, proj, proj, prm, g, fwd, inv)


def _hyena_params(conv_w, conv_b, d):
    rows = [conv_w[:, :, s * HY_W:(s + 1) * HY_W] for s in range(3)]
    rows += [conv_b[:, None, s * HY_W:(s + 1) * HY_W] for s in range(3)]
    rows += [d[:, None, :], jnp.zeros((DEPTH, HY_PARAM_ROWS - 13, HY_W), F32)]
    return jnp.concatenate(rows, axis=1)


OUT_TM = 512


def _outproj_kernel(x_ref, gate_ref, att_ref, lru_ref, hy_ref, w_ref, fg_ref, o_ref, *, final):
    acc = _dot(att_ref[...], w_ref[0:ATT_W, :])
    acc = acc + _dot(lru_ref[...], w_ref[ATT_W:ATT_W + LRU_W, :])
    acc = acc + _dot(hy_ref[...], w_ref[ATT_W + LRU_W:, :])
    y = x_ref[...] + gate_ref[...] * acc
    if final:
        ms = jnp.mean(y * y, axis=-1, keepdims=True)
        y = (y * lax.rsqrt(ms + EPS)) * fg_ref[...]
    o_ref[...] = y


def _outproj(x, mod4, att, lru, hy, w_out_bf, final_g, layer, final, seq_len):
    n_tok = x.shape[0]
    tm = OUT_TM
    cond = _cond_row(seq_len, tm)
    return pl.pallas_call(
        functools.partial(_outproj_kernel, final=final),
        grid=(n_tok // tm,),
        in_specs=[
            pl.BlockSpec((tm, D_MODEL), lambda i: (i, 0)),
            pl.BlockSpec((None, None, 1, D_MODEL), lambda i: (layer, cond(i), 0, 2)),
            pl.BlockSpec((tm, ATT_W), lambda i: (i, 0)),
            pl.BlockSpec((tm, LRU_W), lambda i: (i, 0)),
            pl.BlockSpec((tm, HY_W), lambda i: (i, 0)),
            pl.BlockSpec((None, D_MODEL, D_MODEL), lambda i: (layer, 0, 0), pipeline_mode=pl.Buffered(1)),
            pl.BlockSpec((1, D_MODEL), lambda i: (0, 0)),
        ],
        out_specs=pl.BlockSpec((tm, D_MODEL), lambda i: (i, 0)),
        out_shape=jax.ShapeDtypeStruct((n_tok, D_MODEL), F32),
        compiler_params=_params("arbitrary"),
        name="outproj",
    )(x, mod4, att, lru, hy, w_out_bf, final_g)


def kernel(x_prompt, x_sample, cache_k, cache_v, state_lru, c, c_ctx, norm_g, w_ada, b_ada, w_in, w_out, lam_q1, lam_k1, lam_q2, lam_k2, attn_subln_g, lru_conv_w, lru_conv_b, lru_wa, lru_ba, lru_wi, lru_bi, lru_lam, hy_conv_w, hy_conv_b, hy_w1, hy_b1, hy_w2, hy_b2, hy_w3, hy_d, final_g):
    cos, sin = _rope_tables()
    tables = {}
    for L in (SEQ, DEC_SEQ):
        fwd, inv = _dft_tables(L)
        z, decay = _hyena_tables(L)
        tables[L] = (fwd.astype(BF16), inv.astype(BF16), z, decay)

    cond = jnp.concatenate([c_ctx[None, :], c, jnp.zeros((N_COND - 1 - DEC_BATCH, D_MODEL), F32)], axis=0)
    mod4 = _ada(cond, w_ada, b_ada).reshape(DEPTH, N_COND, 1, 3 * D_MODEL)

    xc = x_prompt.reshape(T_CTX, D_MODEL)
    xs = x_sample.reshape(T_SMP, D_MODEL)
    ck = cache_k.reshape(DEC_BATCH, DEPTH, PAST_LEN * N_HEADS, HEAD_W)
    cv = cache_v.reshape(DEC_BATCH, DEPTH, PAST_LEN * N_HEADS, HEAD_W)
    h0_ctx = jnp.zeros((1, 2, BATCH, LRU_W), F32)
    h0_smp = jnp.transpose(state_lru, (1, 2, 0, 3))
    norm_g3 = norm_g[:, None, :]
    lamp = jnp.stack([lam_q1, lam_k1, lam_q2, lam_k2], axis=1)
    subln = attn_subln_g[:, None, :]
    w_out_bf = w_out.astype(BF16)
    lru_args = (lru_conv_w, lru_conv_b[:, None, :], lru_wa, lru_wi, lru_ba, lru_bi, lru_lam)
    hy_prm = _hyena_params(hy_conv_w, hy_conv_b, hy_d)
    filt_args = (hy_w1, hy_b1[:, None, :], hy_w2, hy_b2[:, None, :], hy_w3)
    fg = final_g[None, :]

    assert DEPTH == 2
    caches, hs = None, []
    for l in range(DEPTH):
        final = l == DEPTH - 1
        lam_init = 0.8 - 0.6 * math.exp(-0.3 * l)

        fwd, inv, z, decay = tables[DEC_SEQ]
        proj, w_in_bf = _inproj_w32(xs, norm_g3, mod4, w_in, l, DEC_SEQ)
        att = _attn_smp(proj, ck, cv, lamp, subln, cos, sin, l, lam_init)
        lru, _ = _lru(proj, *lru_args, h0_smp, l, l, DEC_SEQ, DEC_BATCH)
        g = _filters(z, decay, *filt_args, fwd, l, DEC_SEQ)
        hy = _hyena(proj, hy_prm, g, fwd, inv, l, DEC_SEQ, DEC_BATCH, HY_W // 2)
        xs = _outproj(xs, mod4, att, lru, hy, w_out_bf, fg, l, final, DEC_SEQ)

        fwd, inv, z, decay = tables[SEQ]
        proj, *caches = _inproj_cache(xc, norm_g3, mod4, w_in_bf, l, caches)
        att = _attn_ctx(proj, lamp, subln, l, lam_init)
        lru, h_last = _lru(proj, *lru_args, h0_ctx, l, 0, SEQ, 8)
        hs.append(h_last)
        g = _filters(z, decay, *filt_args, fwd, l, SEQ)
        hy = _hyena(proj, hy_prm, g, fwd, inv, l, SEQ, BATCH, HY_W)
        xc = _outproj(xc, mod4, att, lru, hy, w_out_bf, fg, l, final, None)

    y_prompt = xc.reshape(BATCH, SEQ, D_MODEL)
    y_sample = xs.reshape(DEC_BATCH, DEC_SEQ, D_MODEL)
    new_k, new_v = (a.reshape(BATCH, DEPTH, SEQ, N_HEADS, HEAD_W) for a in caches)
    new_state = jnp.transpose(jnp.stack(hs, axis=0), (2, 0, 1, 3))
    return (y_prompt, y_sample, new_k, new_v, new_state)
```

```python
import functools
import math

import numpy as np
import jax
import jax.numpy as jnp
from jax import lax
from jax.experimental import pallas as pl
from jax.experimental.pallas import tpu as pltpu

D_MODEL = 2048
BATCH = 32
SEQ = 256
DEPTH = 2
DEC_BATCH = 2
DEC_SEQ = 1024
PAST_LEN = 512
GRID_W = 64
ATT_W = 1024
LRU_W = 512
HY_W = 512
HEAD_DIM = 64
N_HEADS = 8
HEAD_W = 2 * HEAD_DIM
LRU_BLOCKS = 8
LRU_BW = LRU_W // LRU_BLOCKS
LRU_C = 8.0
HY_BANDS = 16
HY_POS = 1 + 2 * HY_BANDS
HY_HIDDEN = 64
HY_DECAY_FAST = 0.3
HY_DECAY_SLOW = 1.5
HY_DECAY_TARGET = 1e-2
ROPE_BASE = 10000.0
EPS = 1e-6
IN_W = 4 * ATT_W + 2 * LRU_W + 4 * HY_W

T_CTX = BATCH * SEQ
T_SMP = DEC_BATCH * DEC_SEQ
N_COND = 8

SLAB_W = 1024
N_SLABS = IN_W // SLAB_W
S_Q, S_K, S_V, S_AG, S_LX_LG, S_HV_HX1, S_HX0_HG = range(N_SLABS)

LANES = 128
F32 = jnp.float32
BF16 = jnp.bfloat16
VMEM_LIMIT = 58 * 1024 * 1024


LOG2E = 1.4426950408889634


def _sigmoid(x):
    return 1.0 / (1.0 + jnp.exp2(x * (-LOG2E)))


def _silu(x):
    return x * _sigmoid(x)


def _dot(a, b):
    return jnp.dot(a, b, preferred_element_type=F32)


def _dot_nt(a, b):
    return lax.dot_general(a, b, (((1,), (1,)), ((), ())), preferred_element_type=F32)


def _dot_hi(a, b):
    return jnp.dot(a, b, precision=lax.Precision.HIGHEST, preferred_element_type=F32)


def _params(*sem):
    return pltpu.CompilerParams(dimension_semantics=sem, vmem_limit_bytes=VMEM_LIMIT)


def _rope_tables():
    t = np.arange(DEC_SEQ)
    pos = np.stack([t // GRID_W, t % GRID_W], axis=1).astype(np.float64)
    nf = HEAD_DIM // 4
    inv = ROPE_BASE ** (-np.arange(nf, dtype=np.float64) / nf)
    lane = np.arange(HEAD_W)
    j = lane % HEAD_DIM
    axis = j // (HEAD_DIM // 2)
    f = j % nf
    upper = (j % (HEAD_DIM // 2)) >= nf
    ang = pos[:, axis] * inv[f][None, :]
    cos = np.cos(ang)
    sin = np.sin(ang) * np.where(upper, 1.0, -1.0)[None, :]
    return jnp.asarray(cos, F32), jnp.asarray(sin, F32)


def _dft_tables(L):
    n = 2 * L
    k = np.arange(L)[:, None]
    t = np.arange(L)[None, :]
    ang = 2.0 * np.pi * ((k * t) % n).astype(np.float64) / n
    fwd = np.concatenate([np.cos(ang), -np.sin(ang)], axis=0)
    fwd[L, :] = (-1.0) ** np.arange(L)
    wk = np.where(np.arange(L) == 0, 1.0, 2.0)[None, :]
    ang_t = ang.T
    inv = np.concatenate([wk * np.cos(ang_t), -2.0 * np.sin(ang_t)], axis=1) / n
    inv[:, L] = ((-1.0) ** np.arange(L)) / n
    return jnp.asarray(fwd, F32), jnp.asarray(inv, F32)


def _hyena_tables(L):
    pos = np.arange(L, dtype=np.float64)
    t = pos / float(max(L - 1, 1))
    bands = np.linspace(1e-4, HY_BANDS - 1, HY_BANDS)
    ang = (2.0 * math.pi / L) * pos[:, None] * bands[None, :]
    z = np.concatenate([t[:, None], np.cos(ang), np.sin(ang)], axis=-1)
    lo = abs(math.log(HY_DECAY_TARGET) / HY_DECAY_SLOW)
    hi = abs(math.log(HY_DECAY_TARGET) / HY_DECAY_FAST)
    deltas = np.linspace(lo, hi, HY_W)
    decay = np.exp(-t[:, None] * deltas[None, :])
    return jnp.asarray(z, F32), jnp.asarray(decay, F32)


def _ada_kernel(c_ref, w_ref, b_ref, o_ref):
    s = _silu(c_ref[...])
    o_ref[...] = _dot(s.astype(BF16), w_ref[...].astype(BF16)) + b_ref[...]


def _ada(cond, w_ada, b_ada):
    tn = 1024
    return pl.pallas_call(
        _ada_kernel,
        grid=(DEPTH, 3 * D_MODEL // tn),
        in_specs=[
            pl.BlockSpec((N_COND, D_MODEL), lambda l, j: (0, 0)),
            pl.BlockSpec((None, D_MODEL, tn), lambda l, j: (l, 0, j)),
            pl.BlockSpec((None, 1, tn), lambda l, j: (l, 0, j)),
        ],
        out_specs=pl.BlockSpec((None, N_COND, tn), lambda l, j: (l, 0, j)),
        out_shape=jax.ShapeDtypeStruct((DEPTH, N_COND, 3 * D_MODEL), F32),
        compiler_params=_params("arbitrary", "arbitrary"),
        name="ada",
    )(cond, w_ada, b_ada.reshape(DEPTH, 1, 3 * D_MODEL))


def _cond_row(seq_len, tm):
    if seq_len is None:
        return lambda i: 0
    return lambda i: 1 + i // (seq_len // tm)


IN_TM = 1024
NORM_ROWS = 32


def _norm_modulate(x_ref, g_ref, shift_ref, scale_ref, h_ref, h_row0=0):
    g = g_ref[...]
    sc = 1.0 + scale_ref[...]
    sh = shift_ref[...]

    def body(r, carry):
        start = pl.multiple_of(r * NORM_ROWS, NORM_ROWS)
        x = x_ref[pl.ds(start, NORM_ROWS), :]
        ms = jnp.mean(x * x, axis=-1, keepdims=True)
        xn = x * lax.rsqrt(ms + EPS)
        h_ref[pl.ds(pl.multiple_of(h_row0 + start, NORM_ROWS), NORM_ROWS), :] = ((xn * g) * sc + sh).astype(BF16)
        return carry

    lax.fori_loop(0, IN_TM // NORM_ROWS, body, 0, unroll=4)


def _inproj_cache_kernel(*refs, first_layer):
    if first_layer:
        x_ref, g_ref, shift_ref, scale_ref, w_ref, o_ref, kc_ref, vc_ref, h_ref = refs
    else:
        x_ref, g_ref, shift_ref, scale_ref, w_ref, _, _, o_ref, kc_ref, vc_ref, h_ref = refs
    j = pl.program_id(1)

    @pl.when(j == 0)
    def _():
        _norm_modulate(x_ref, g_ref, shift_ref, scale_ref, h_ref)

    o_ref[...] = _dot(h_ref[...], w_ref[...])

    def scatter(dst_ref):
        for b in range(IN_TM // SEQ):
            for h in range(N_HEADS):
                dst_ref[b, pl.ds(h, SEQ, stride=N_HEADS), :] = o_ref[b * SEQ:(b + 1) * SEQ, h * HEAD_W:(h + 1) * HEAD_W]

    @pl.when(j == S_K)
    def _():
        scatter(kc_ref)

    @pl.when(j == S_V)
    def _():
        scatter(vc_ref)

    if first_layer:
        @pl.when(j == S_K + 1)
        def _():
            kc_ref[...] = jnp.zeros(kc_ref.shape, F32)

        @pl.when(j == S_V + 1)
        def _():
            vc_ref[...] = jnp.zeros(vc_ref.shape, F32)


def _of_layer(layer, shape):
    return pl.BlockSpec((None,) + tuple(shape), lambda *_: (layer,) + (0,) * len(shape))


def _inproj_specs(tm, layer, cond, w_spec):
    return [
        pl.BlockSpec((tm, D_MODEL), lambda i, j: (i, 0)),
        _of_layer(layer, (1, D_MODEL)),
        pl.BlockSpec((None, None, 1, D_MODEL), lambda i, j: (layer, cond(i), 0, 0)),
        pl.BlockSpec((None, None, 1, D_MODEL), lambda i, j: (layer, cond(i), 0, 1)),
        w_spec,
    ]


W32_K_CHUNK = 512


def _inproj_w32_kernel(x_ref, g_ref, shift_ref, scale_ref, w_ref, o_ref, wbf_ref, h_ref):
    row0 = pl.multiple_of(pl.program_id(1) * IN_TM, IN_TM)

    @pl.when(pl.program_id(0) == 0)
    def _():
        _norm_modulate(x_ref, g_ref, shift_ref, scale_ref, h_ref, row0)

    for k0 in range(0, D_MODEL, W32_K_CHUNK):
        w = w_ref[k0:k0 + W32_K_CHUNK, :].astype(BF16)
        wbf_ref[k0:k0 + W32_K_CHUNK, :] = w
        part = _dot(h_ref[pl.ds(row0, IN_TM), k0:k0 + W32_K_CHUNK], w)
        if k0 == 0:
            o_ref[...] = part
        else:
            o_ref[...] += part


def _inproj_w32(x, norm_g, mod4, w_in, layer, seq_len):
    n_tok = x.shape[0]
    tm = IN_TM
    n_tiles = n_tok // tm
    cond = _cond_row(seq_len, tm)
    x_row = lambda j, i: jnp.where(j == 0, i, n_tiles - 1)
    return pl.pallas_call(
        _inproj_w32_kernel,
        grid=(N_SLABS, n_tiles),
        in_specs=[
            pl.BlockSpec((tm, D_MODEL), lambda j, i: (x_row(j, i), 0), pipeline_mode=pl.Buffered(1)),
            _of_layer(layer, (1, D_MODEL)),
            pl.BlockSpec((None, None, 1, D_MODEL), lambda j, i: (layer, cond(i), 0, 0)),
            pl.BlockSpec((None, None, 1, D_MODEL), lambda j, i: (layer, cond(i), 0, 1)),
            pl.BlockSpec((None, D_MODEL, SLAB_W), lambda j, i: (layer, 0, j)),
        ],
        out_specs=[
            pl.BlockSpec((None, tm, SLAB_W), lambda j, i: (j, i, 0)),
            pl.BlockSpec((D_MODEL, SLAB_W), lambda j, i: (0, j)),
        ],
        out_shape=[jax.ShapeDtypeStruct((N_SLABS, n_tok, SLAB_W), F32), jax.ShapeDtypeStruct((D_MODEL, IN_W), BF16)],
        scratch_shapes=[pltpu.VMEM((n_tok, D_MODEL), BF16)],
        compiler_params=_params("arbitrary", "arbitrary"),
        name="inproj_w32",
    )(x, norm_g, mod4, mod4, w_in)


def _inproj_cache(x, norm_g, mod4, w_in_bf, layer, caches):
    tm = IN_TM
    nb = tm // SEQ
    first = caches is None
    cache_shape = jax.ShapeDtypeStruct((BATCH, DEPTH, SEQ * N_HEADS, HEAD_W), F32)
    if first:
        half = lambda slab: (lambda i, j: (i, jnp.where(j <= slab, layer, layer + 1), 0, 0))
    else:
        half = lambda slab: (lambda i, j: (i, layer, 0, 0))
    cache_spec = lambda slab: pl.BlockSpec((nb, None, SEQ * N_HEADS, HEAD_W), half(slab))
    w_spec = pl.BlockSpec((D_MODEL, SLAB_W), lambda i, j: (0, j))
    in_specs = _inproj_specs(tm, layer, _cond_row(None, tm), w_spec)
    args = (x, norm_g, mod4, mod4, w_in_bf)
    aliases = {}
    if not first:
        in_specs += [pl.BlockSpec(memory_space=pl.ANY)] * 2
        aliases = {len(args): 1, len(args) + 1: 2}
        args += tuple(caches)
    return pl.pallas_call(
        functools.partial(_inproj_cache_kernel, first_layer=first),
        grid=(T_CTX // tm, N_SLABS),
        in_specs=in_specs,
        out_specs=[pl.BlockSpec((None, tm, SLAB_W), lambda i, j: (j, i, 0)), cache_spec(S_K), cache_spec(S_V)],
        out_shape=[jax.ShapeDtypeStruct((N_SLABS, T_CTX, SLAB_W), F32), cache_shape, cache_shape],
        scratch_shapes=[pltpu.VMEM((tm, D_MODEL), BF16)],
        input_output_aliases=aliases,
        compiler_params=_params("arbitrary", "arbitrary"),
        name="inproj_cache",
    )(*args)


def _lam_value(lamp_ref, lam_init):
    lp = lamp_ref[...]
    t1 = jnp.sum(lp[0:1] * lp[1:2], axis=-1, keepdims=True)
    t2 = jnp.sum(lp[2:3] * lp[3:4], axis=-1, keepdims=True)
    return jnp.exp(t1) - jnp.exp(t2) + lam_init


def _diff_attend(items, lam, g, lam_init):
    m = items[0][0].shape[0]
    lane = lax.broadcasted_iota(jnp.int32, (m, HEAD_W), 1)
    ss = []
    for q, k_bf, _ in items:
        q1 = jnp.where(lane < HEAD_DIM, q, 0.0).astype(BF16)
        q2 = jnp.where(lane < HEAD_DIM, 0.0, q).astype(BF16)
        ss.append(_dot_nt(jnp.concatenate([q1, q2], axis=0), k_bf))
    es = [jnp.exp(s - jnp.max(s, axis=-1, keepdims=True)) for s in ss]
    ps = [e * (1.0 / jnp.sum(e, axis=-1, keepdims=True)) for e in es]
    ws = [(p[:m] - lam * p[m:]).astype(BF16) for p in ps]
    os_ = [_dot(w, v_bf) for w, (_, _, v_bf) in zip(ws, items)]
    outs = []
    for o in os_:
        ms = jnp.mean(o * o, axis=-1, keepdims=True)
        outs.append((o * lax.rsqrt(ms + EPS) * g) * (1.0 - lam_init))
    return outs


CTX_BATCHES = 2


def _attn_ctx_kernel(lamp_ref, g_ref, q_ref, k_ref, v_ref, ag_ref, o_ref, *, lam_init):
    lam = _lam_value(lamp_ref, lam_init)
    where = [(slice(b * SEQ, (b + 1) * SEQ), slice(h * HEAD_W, (h + 1) * HEAD_W))
             for b in range(CTX_BATCHES) for h in range(N_HEADS)]
    items = [(q_ref[r, c] * (HEAD_DIM ** -0.5), k_ref[r, c].astype(BF16), v_ref[r, c].astype(BF16)) for r, c in where]
    for (r, c), o in zip(where, _diff_attend(items, lam, g_ref[...], lam_init)):
        o_ref[r, c] = (o * _silu(ag_ref[r, c])).astype(BF16)


def _attn_ctx(proj, lamp, subln_g, layer, lam_init):
    rows = CTX_BATCHES * SEQ
    slab = lambda s: pl.BlockSpec((None, rows, ATT_W), lambda b, s=s: (s, b, 0))
    return pl.pallas_call(
        functools.partial(_attn_ctx_kernel, lam_init=lam_init),
        grid=(BATCH // CTX_BATCHES,),
        in_specs=[
            _of_layer(layer, (4, HEAD_DIM)),
            _of_layer(layer, (1, HEAD_W)),
            slab(S_Q), slab(S_K), slab(S_V), slab(S_AG),
        ],
        out_specs=pl.BlockSpec((rows, ATT_W), lambda b: (b, 0)),
        out_shape=jax.ShapeDtypeStruct((T_CTX, ATT_W), BF16),
        compiler_params=_params("arbitrary"),
        name="attn_ctx",
    )(lamp, subln_g, proj, proj, proj, proj)


Q_CHUNK = 512
Q_PART = 256
N_QC = DEC_SEQ // Q_CHUNK


def _rope(x, cos, sin):
    lane = lax.broadcasted_iota(jnp.int32, x.shape, 1)
    lower = (lane % (HEAD_DIM // 2)) < (HEAD_DIM // 4)
    nf = HEAD_DIM // 4
    partner = jnp.where(lower, pltpu.roll(x, HEAD_W - nf, 1), pltpu.roll(x, nf, 1))
    return x * cos + partner * sin


def _attn_smp_kernel(lamp_ref, g_ref, cosq_ref, sinq_ref, cos_ref, sin_ref, q_ref, k_ref, v_ref,
                     ck_ref, cv_ref, ag_ref, o_ref, ks_ref, vs_ref, *, lam_init):
    @pl.when(pl.program_id(2) == 0)
    def _():
        head_rows = pl.ds(pl.program_id(1), PAST_LEN, stride=N_HEADS)
        ks_ref[0:PAST_LEN, :] = ck_ref[head_rows, :].astype(BF16)
        vs_ref[0:PAST_LEN, :] = cv_ref[head_rows, :].astype(BF16)
        ks_ref[PAST_LEN:, :] = _rope(k_ref[...], cos_ref[...], sin_ref[...]).astype(BF16)
        vs_ref[PAST_LEN:, :] = v_ref[...].astype(BF16)

    lam = _lam_value(lamp_ref, lam_init)
    q = _rope(q_ref[...], cosq_ref[...], sinq_ref[...]) * (HEAD_DIM ** -0.5)
    k_bf, v_bf = ks_ref[...], vs_ref[...]
    parts = [slice(r, r + Q_PART) for r in range(0, Q_CHUNK, Q_PART)]
    items = [(q[p], k_bf, v_bf) for p in parts]
    for p, o in zip(parts, _diff_attend(items, lam, g_ref[...], lam_init)):
        o_ref[p, :] = (o * _silu(ag_ref[p, :])).astype(BF16)


def _attn_smp(proj, cache_k, cache_v, lamp, subln_g, cos, sin, layer, lam_init):
    full = lambda shape: pl.BlockSpec(shape, lambda b, h, c: (0, 0))
    return pl.pallas_call(
        functools.partial(_attn_smp_kernel, lam_init=lam_init),
        grid=(DEC_BATCH, N_HEADS, N_QC),
        in_specs=[
            _of_layer(layer, (4, HEAD_DIM)),
            _of_layer(layer, (1, HEAD_W)),
            pl.BlockSpec((Q_CHUNK, HEAD_W), lambda b, h, c: (c, 0)),
            pl.BlockSpec((Q_CHUNK, HEAD_W), lambda b, h, c: (c, 0)),
            full((DEC_SEQ, HEAD_W)),
            full((DEC_SEQ, HEAD_W)),
            pl.BlockSpec((None, Q_CHUNK, HEAD_W), lambda b, h, c: (S_Q, b * N_QC + c, h)),
            pl.BlockSpec((None, DEC_SEQ, HEAD_W), lambda b, h, c: (S_K, b, h)),
            pl.BlockSpec((None, DEC_SEQ, HEAD_W), lambda b, h, c: (S_V, b, h)),
            pl.BlockSpec((None, None, PAST_LEN * N_HEADS, HEAD_W), lambda b, h, c: (b, layer, 0, 0)),
            pl.BlockSpec((None, None, PAST_LEN * N_HEADS, HEAD_W), lambda b, h, c: (b, layer, 0, 0)),
            pl.BlockSpec((None, Q_CHUNK, HEAD_W), lambda b, h, c: (S_AG, b * N_QC + c, h)),
        ],
        out_specs=pl.BlockSpec((Q_CHUNK, HEAD_W), lambda b, h, c: (b * N_QC + c, h)),
        out_shape=jax.ShapeDtypeStruct((T_SMP, ATT_W), BF16),
        scratch_shapes=[
            pltpu.VMEM((PAST_LEN + DEC_SEQ, HEAD_W), BF16),
            pltpu.VMEM((PAST_LEN + DEC_SEQ, HEAD_W), BF16),
        ],
        compiler_params=_params("arbitrary", "arbitrary", "arbitrary"),
        name="attn_smp",
    )(lamp, subln_g, cos, sin, cos, sin, proj, proj, proj, cache_k, cache_v, proj)


PAD = 8
GATE_ROWS = 256
LRU_SLABS = LRU_W // LANES


def _lru_kernel(x_ref, lg_ref, cw_ref, cb_ref, wa_ref, wi_ref, ba_ref, bi_ref, lam_ref, h0_ref, o_ref, hl_ref,
                xpad_ref, a_ref, b_ref, h_ref, w_ref, *, L, G):
    @pl.when(pl.program_id(0) == 0)
    def _():
        w_ref[...] = jnp.zeros(w_ref.shape, BF16)
        for d in range(2):
            for gate, src in enumerate((wa_ref, wi_ref)):
                col0 = (2 * d + gate) * LRU_W
                for n in range(LRU_BLOCKS):
                    lo = n * LRU_BW
                    w_ref[lo:lo + LRU_BW, col0 + lo:col0 + lo + LRU_BW] = src[d, n].astype(BF16)

    zeros = jnp.zeros((PAD, LRU_W), F32)
    xpad_ref[0:PAD, :] = zeros
    xpad_ref[PAD + L:, :] = zeros
    nl = -lam_ref[...]
    neg_c = (-LRU_C * LOG2E) * (jnp.maximum(nl, 0.0) + jnp.log1p(jnp.exp(-jnp.abs(nl))))
    cw = cw_ref[...]
    cb = cb_ref[...]
    ba = ba_ref[...]
    bi = bi_ref[...]

    def gates_of_sequence(g, carry):
        xpad_ref[PAD:PAD + L, :] = x_ref[pl.ds(pl.multiple_of(g * L, L), L), :]
        for c in range(L // GATE_ROWS):
            r0 = c * GATE_ROWS
            xc = cb + cw[0:1] * xpad_ref[PAD - 2 + r0:PAD - 2 + r0 + GATE_ROWS, :]
            xc = xc + cw[1:2] * xpad_ref[PAD - 1 + r0:PAD - 1 + r0 + GATE_ROWS, :]
            xc = xc + cw[2:3] * xpad_ref[PAD + r0:PAD + r0 + GATE_ROWS, :]
            xc = xc + cw[3:4] * xpad_ref[PAD + 1 + r0:PAD + 1 + r0 + GATE_ROWS, :]
            gates = _dot(xc.astype(BF16), w_ref[...])
            for d in range(2):
                r = _sigmoid(gates[:, (2 * d) * LRU_W:(2 * d + 1) * LRU_W] + ba[d:d + 1])
                i = _sigmoid(gates[:, (2 * d + 1) * LRU_W:(2 * d + 2) * LRU_W] + bi[d:d + 1])
                a = jnp.exp2(r * neg_c[d:d + 1])
                a2 = 1.0 - a * a
                b = jnp.where(a2 > 0.0, a2 * lax.rsqrt(a2), 0.0) * (i * xc)
                rows = pl.ds(r0 * G + g, GATE_ROWS, stride=G)
                for s in range(LRU_SLABS):
                    a_ref[d * LRU_SLABS + s, rows, :] = a[:, s * LANES:(s + 1) * LANES]
                    b_ref[d * LRU_SLABS + s, rows, :] = b[:, s * LANES:(s + 1) * LANES]
        return carry

    lax.fori_loop(0, G, gates_of_sequence, 0)

    def step(t, hs):
        out = []
        for d in range(2):
            tt = t if d == 0 else L - 1 - t
            rows = pl.ds(pl.multiple_of(tt * G, G), G)
            for s in range(LRU_SLABS):
                k = d * LRU_SLABS + s
                h = a_ref[k, rows, :] * hs[k] + b_ref[k, rows, :]
                h_ref[k, rows, :] = h
                out.append(h)
        return tuple(out)

    h0 = tuple(h0_ref[d, :, s * LANES:(s + 1) * LANES] for d in range(2) for s in range(LRU_SLABS))
    hs = lax.fori_loop(0, L, step, h0, unroll=2)
    for d in range(2):
        for s in range(LRU_SLABS):
            hl_ref[d, :, s * LANES:(s + 1) * LANES] = hs[d * LRU_SLABS + s]

    def write_sequence(g, carry):
        rows = pl.ds(g, L, stride=G)
        orow = pl.ds(pl.multiple_of(g * L, L), L)
        for s in range(LRU_SLABS):
            lanes = slice(s * LANES, (s + 1) * LANES)
            h = h_ref[s, rows, :] + h_ref[LRU_SLABS + s, rows, :]
            o_ref[orow, lanes] = (h * _silu(lg_ref[orow, lanes])).astype(BF16)
        return carry

    lax.fori_loop(0, G, write_sequence, 0)


def _lru(proj, conv_w, conv_b, wa, wi, ba, bi, lam, h0, layer, h0_layer, L, G):
    nseq = h0.shape[2]
    return pl.pallas_call(
        functools.partial(_lru_kernel, L=L, G=G),
        grid=(nseq // G,),
        in_specs=[
            pl.BlockSpec((None, G * L, LRU_W), lambda s: (S_LX_LG, s, 0)),
            pl.BlockSpec((None, G * L, LRU_W), lambda s: (S_LX_LG, s, 1)),
            _of_layer(layer, (4, LRU_W)),
            _of_layer(layer, (1, LRU_W)),
            _of_layer(layer, (2, LRU_BLOCKS, LRU_BW, LRU_BW)),
            _of_layer(layer, (2, LRU_BLOCKS, LRU_BW, LRU_BW)),
            _of_layer(layer, (2, LRU_W)),
            _of_layer(layer, (2, LRU_W)),
            _of_layer(layer, (2, LRU_W)),
            pl.BlockSpec((None, 2, G, LRU_W), lambda s: (h0_layer, 0, s, 0)),
        ],
        out_specs=[
            pl.BlockSpec((G * L, LRU_W), lambda s: (s, 0)),
            pl.BlockSpec((2, G, LRU_W), lambda s: (0, s, 0)),
        ],
        out_shape=[
            jax.ShapeDtypeStruct((nseq * L, LRU_W), BF16),
            jax.ShapeDtypeStruct((2, nseq, LRU_W), F32),
        ],
        scratch_shapes=[
            pltpu.VMEM((L + 2 * PAD, LRU_W), F32),
            pltpu.VMEM((2 * LRU_SLABS, L * G, LANES), F32),
            pltpu.VMEM((2 * LRU_SLABS, L * G, LANES), F32),
            pltpu.VMEM((2 * LRU_SLABS, L * G, LANES), F32),
            pltpu.VMEM((LRU_W, 4 * LRU_W), BF16),
        ],
        compiler_params=_params("arbitrary"),
        name=f"lru_{L}",
    )(proj, proj, conv_w, conv_b, wa, wi, ba, bi, lam, h0)


FILT_ROWS = 256


def _filt_kernel(z_ref, decay_ref, w1_ref, b1_ref, w2_ref, b2_ref, w3_ref, ft_ref, fb_ref, g_ref, taps_ref, *, L):
    i = pl.program_id(0)

    @pl.when(i == 0)
    def _():
        h = jnp.sin(_dot_hi(z_ref[...], w1_ref[...]) + b1_ref[...])
        h = jnp.sin(_dot_hi(h, w2_ref[...]) + b2_ref[...])
        filt = _dot_hi(h, w3_ref[...])
        decay = decay_ref[...]
        row = lax.broadcasted_iota(jnp.int32, (L, HY_W), 0)
        taps_ref[:, :HY_W] = (filt[:, :HY_W] * decay).astype(BF16)
        taps_ref[:, HY_W:] = jnp.where(row == 0, 0.0, filt[:, HY_W:] * decay).astype(BF16)

    taps = taps_ref[...]
    top = _dot(ft_ref[...], taps)
    bot = _dot(fb_ref[...], taps)
    first = (lax.broadcasted_iota(jnp.int32, (FILT_ROWS, HY_W), 0) + i * FILT_ROWS) == 0
    gt = top[:, :HY_W] + top[:, HY_W:]
    g_ref[0] = gt
    g_ref[1] = jnp.where(first, 0.0, bot[:, :HY_W] - bot[:, HY_W:])
    g_ref[2] = jnp.where(first, bot[:, :HY_W] + bot[:, HY_W:], gt)


def _filters(z, decay, w1, b1, w2, b2, w3, fwd_bf, layer, L):
    full = lambda shape: pl.BlockSpec(shape, lambda i: (0,) * len(shape))
    nblk = L // FILT_ROWS
    return pl.pallas_call(
        functools.partial(_filt_kernel, L=L),
        grid=(nblk,),
        in_specs=[
            full((L, HY_POS)), full((L, HY_W)),
            _of_layer(layer, (HY_POS, HY_HIDDEN)), _of_layer(layer, (1, HY_HIDDEN)),
            _of_layer(layer, (HY_HIDDEN, HY_HIDDEN)), _of_layer(layer, (1, HY_HIDDEN)),
            _of_layer(layer, (HY_HIDDEN, 2 * HY_W)),
            pl.BlockSpec((FILT_ROWS, L), lambda i: (i, 0)),
            pl.BlockSpec((FILT_ROWS, L), lambda i: (nblk + i, 0)),
        ],
        out_specs=pl.BlockSpec((3, FILT_ROWS, HY_W), lambda i: (0, i, 0)),
        out_shape=jax.ShapeDtypeStruct((3, L, HY_W), F32),
        scratch_shapes=[pltpu.VMEM((L, 2 * HY_W), BF16)],
        compiler_params=_params("arbitrary"),
        name=f"hyena_filters_{L}",
    )(z, decay, w1, b1, w2, b2, w3, fwd_bf, fwd_bf)


HY_PARAM_ROWS = 16


def _hyena_kernel(hv_ref, hx1_ref, hx0_ref, hg_ref, prm_ref, g_ref, f_ref, fi_ref, o_ref, xpad_ref, *, L, ch, seqs):
    zeros = jnp.zeros((PAD, ch), F32)
    for s in range(seqs):
        xpad_ref[s, 0:PAD, :] = zeros
        xpad_ref[s, PAD + L:, :] = zeros
    prm = prm_ref[...]
    rows = [slice(s * L, (s + 1) * L) for s in range(seqs)]

    def conv(ref, stream, s):
        xpad_ref[s, PAD:PAD + L, :] = ref[rows[s], :]
        u = prm[9 + stream:10 + stream] + prm[3 * stream:3 * stream + 1] * xpad_ref[s, PAD - 1:PAD - 1 + L, :]
        u = u + prm[3 * stream + 1:3 * stream + 2] * xpad_ref[s, PAD:PAD + L, :]
        return u + prm[3 * stream + 2:3 * stream + 3] * xpad_ref[s, PAD + 1:PAD + 1 + L, :]

    zs = [conv(hx1_ref, 1, s) * conv(hv_ref, 0, s) for s in range(seqs)]
    specs = [_dot(f_ref[...], z.astype(BF16)) for z in zs]
    gx = g_ref[1]
    prods = []
    for spec in specs:
        top, bot = spec[:L], spec[L:]
        yt = top * g_ref[0] - bot * gx
        yb = top * gx + bot * g_ref[2]
        prods.append(jnp.concatenate([yt, yb], axis=0).astype(BF16))
    ys = [_dot(fi_ref[...], p) for p in prods]
    for s in range(seqs):
        hy = conv(hx0_ref, 2, s) * (ys[s] + zs[s] * prm[12:13])
        o_ref[rows[s], :] = (hy * _silu(hg_ref[rows[s], :])).astype(BF16)


def _hyena(proj, prm, g, fwd, inv, layer, L, nseq, ch, seqs):
    nch = HY_W // ch
    full = lambda shape: pl.BlockSpec(shape, lambda s, j: (0,) * len(shape))
    col = lambda slab, off: pl.BlockSpec((None, seqs * L, ch), lambda s, j, slab=slab, off=off: (slab, s, off * nch + j))
    return pl.pallas_call(
        functools.partial(_hyena_kernel, L=L, ch=ch, seqs=seqs),
        grid=(nseq // seqs, nch),
        in_specs=[
            col(S_HV_HX1, 0), col(S_HV_HX1, 1), col(S_HX0_HG, 0), col(S_HX0_HG, 1),
            pl.BlockSpec((None, HY_PARAM_ROWS, ch), lambda s, j: (layer, 0, j)),
            pl.BlockSpec((3, L, ch), lambda s, j: (0, 0, j)),
            full((2 * L, L)), full((L, 2 * L)),
        ],
        out_specs=pl.BlockSpec((seqs * L, ch), lambda s, j: (s, j)),
        out_shape=jax.ShapeDtypeStruct((nseq * L, HY_W), BF16),
        scratch_shapes=[pltpu.VMEM((seqs, L + 2 * PAD, ch), F32)],
        compiler_params=_params("arbitrary", "arbitrary"),
        name=f"hyena_{L}",
    )(proj, proj, proj, proj, prm, g, fwd, inv)


def _hyena_params(conv_w, conv_b, d):
    rows = [conv_w[:, :, s * HY_W:(s + 1) * HY_W] for s in range(3)]
    rows += [conv_b[:, None, s * HY_W:(s + 1) * HY_W] for s in range(3)]
    rows += [d[:, None, :], jnp.zeros((DEPTH, HY_PARAM_ROWS - 13, HY_W), F32)]
    return jnp.concatenate(rows, axis=1)


OUT_TM = 512


def _outproj_kernel(x_ref, gate_ref, att_ref, lru_ref, hy_ref, w_ref, fg_ref, o_ref, *, final):
    acc = _dot(att_ref[...], w_ref[0:ATT_W, :])
    acc = acc + _dot(lru_ref[...], w_ref[ATT_W:ATT_W + LRU_W, :])
    acc = acc + _dot(hy_ref[...], w_ref[ATT_W + LRU_W:, :])
    y = x_ref[...] + gate_ref[...] * acc
    if final:
        ms = jnp.mean(y * y, axis=-1, keepdims=True)
        y = (y * lax.rsqrt(ms + EPS)) * fg_ref[...]
    o_ref[...] = y


def _outproj(x, mod4, att, lru, hy, w_out_bf, final_g, layer, final, seq_len):
    n_tok = x.shape[0]
    tm = OUT_TM
    cond = _cond_row(seq_len, tm)
    return pl.pallas_call(
        functools.partial(_outproj_kernel, final=final),
        grid=(n_tok // tm,),
        in_specs=[
            pl.BlockSpec((tm, D_MODEL), lambda i: (i, 0)),
            pl.BlockSpec((None, None, 1, D_MODEL), lambda i: (layer, cond(i), 0, 2)),
            pl.BlockSpec((tm, ATT_W), lambda i: (i, 0)),
            pl.BlockSpec((tm, LRU_W), lambda i: (i, 0)),
            pl.BlockSpec((tm, HY_W), lambda i: (i, 0)),
            pl.BlockSpec((None, D_MODEL, D_MODEL), lambda i: (layer, 0, 0), pipeline_mode=pl.Buffered(1)),
            pl.BlockSpec((1, D_MODEL), lambda i: (0, 0)),
        ],
        out_specs=pl.BlockSpec((tm, D_MODEL), lambda i: (i, 0)),
        out_shape=jax.ShapeDtypeStruct((n_tok, D_MODEL), F32),
        compiler_params=_params("arbitrary"),
        name="outproj",
    )(x, mod4, att, lru, hy, w_out_bf, final_g)


def kernel(x_prompt, x_sample, cache_k, cache_v, state_lru, c, c_ctx, norm_g, w_ada, b_ada, w_in, w_out, lam_q1, lam_k1, lam_q2, lam_k2, attn_subln_g, lru_conv_w, lru_conv_b, lru_wa, lru_ba, lru_wi, lru_bi, lru_lam, hy_conv_w, hy_conv_b, hy_w1, hy_b1, hy_w2, hy_b2, hy_w3, hy_d, final_g):
    cos, sin = _rope_tables()
    tables = {}
    for L in (SEQ, DEC_SEQ):
        fwd, inv = _dft_tables(L)
        z, decay = _hyena_tables(L)
        tables[L] = (fwd.astype(BF16), inv.astype(BF16), z, decay)

    cond = jnp.concatenate([c_ctx[None, :], c, jnp.zeros((N_COND - 1 - DEC_BATCH, D_MODEL), F32)], axis=0)
    mod4 = _ada(cond, w_ada, b_ada).reshape(DEPTH, N_COND, 1, 3 * D_MODEL)

    xc = x_prompt.reshape(T_CTX, D_MODEL)
    xs = x_sample.reshape(T_SMP, D_MODEL)
    ck = cache_k.reshape(DEC_BATCH, DEPTH, PAST_LEN * N_HEADS, HEAD_W)
    cv = cache_v.reshape(DEC_BATCH, DEPTH, PAST_LEN * N_HEADS, HEAD_W)
    h0_ctx = jnp.zeros((1, 2, BATCH, LRU_W), F32)
    h0_smp = jnp.transpose(state_lru, (1, 2, 0, 3))
    norm_g3 = norm_g[:, None, :]
    lamp = jnp.stack([lam_q1, lam_k1, lam_q2, lam_k2], axis=1)
    subln = attn_subln_g[:, None, :]
    w_out_bf = w_out.astype(BF16)
    lru_args = (lru_conv_w, lru_conv_b[:, None, :], lru_wa, lru_wi, lru_ba, lru_bi, lru_lam)
    hy_prm = _hyena_params(hy_conv_w, hy_conv_b, hy_d)
    filt_args = (hy_w1, hy_b1[:, None, :], hy_w2, hy_b2[:, None, :], hy_w3)
    fg = final_g[None, :]

    assert DEPTH == 2
    caches, hs = None, []
    for l in range(DEPTH):
        final = l == DEPTH - 1
        lam_init = 0.8 - 0.6 * math.exp(-0.3 * l)

        fwd, inv, z, decay = tables[DEC_SEQ]
        proj, w_in_bf = _inproj_w32(xs, norm_g3, mod4, w_in, l, DEC_SEQ)
        att = _attn_smp(proj, ck, cv, lamp, subln, cos, sin, l, lam_init)
        lru, _ = _lru(proj, *lru_args, h0_smp, l, l, DEC_SEQ, DEC_BATCH)
        g = _filters(z, decay, *filt_args, fwd, l, DEC_SEQ)
        hy = _hyena(proj, hy_prm, g, fwd, inv, l, DEC_SEQ, DEC_BATCH, HY_W // 2, 1)
        xs = _outproj(xs, mod4, att, lru, hy, w_out_bf, fg, l, final, DEC_SEQ)

        fwd, inv, z, decay = tables[SEQ]
        proj, *caches = _inproj_cache(xc, norm_g3, mod4, w_in_bf, l, caches)
        att = _attn_ctx(proj, lamp, subln, l, lam_init)
        lru, h_last = _lru(proj, *lru_args, h0_ctx, l, 0, SEQ, 8)
        hs.append(h_last)
        g = _filters(z, decay, *filt_args, fwd, l, SEQ)
        hy = _hyena(proj, hy_prm, g, fwd, inv, l, SEQ, BATCH, HY_W, 2)
        xc = _outproj(xc, mod4, att, lru, hy, w_out_bf, fg, l, final, None)

    y_prompt = xc.reshape(BATCH, SEQ, D_MODEL)
    y_sample = xs.reshape(DEC_BATCH, DEC_SEQ, D_MODEL)
    new_k, new_v = (a.reshape(BATCH, DEPTH, SEQ, N_HEADS, HEAD_W) for a in caches)
    new_state = jnp.transpose(jnp.stack(hs, axis=0), (2, 0, 1, 3))
    return (y_prompt, y_sample, new_k, new_v, new_state)
```

```python
import functools
import math

import numpy as np
import jax
import jax.numpy as jnp
from jax import lax
from jax.experimental import pallas as pl
from jax.experimental.pallas import tpu as pltpu

D_MODEL = 2048
BATCH = 32
SEQ = 256
DEPTH = 2
DEC_BATCH = 2
DEC_SEQ = 1024
PAST_LEN = 512
GRID_W = 64
ATT_W = 1024
LRU_W = 512
HY_W = 512
HEAD_DIM = 64
N_HEADS = 8
HEAD_W = 2 * HEAD_DIM
LRU_BLOCKS = 8
LRU_BW = LRU_W // LRU_BLOCKS
LRU_C = 8.0
HY_BANDS = 16
HY_POS = 1 + 2 * HY_BANDS
HY_HIDDEN = 64
HY_DECAY_FAST = 0.3
HY_DECAY_SLOW = 1.5
HY_DECAY_TARGET = 1e-2
ROPE_BASE = 10000.0
EPS = 1e-6
IN_W = 4 * ATT_W + 2 * LRU_W + 4 * HY_W

T_CTX = BATCH * SEQ
T_SMP = DEC_BATCH * DEC_SEQ
N_COND = 8

SLAB_W = 1024
N_SLABS = IN_W // SLAB_W
S_Q, S_K, S_V, S_AG, S_LX_LG, S_HV_HX1, S_HX0_HG = range(N_SLABS)

LANES = 128
F32 = jnp.float32
BF16 = jnp.bfloat16
VMEM_LIMIT = 58 * 1024 * 1024


LOG2E = 1.4426950408889634


def _sigmoid(x):
    return 1.0 / (1.0 + jnp.exp2(x * (-LOG2E)))


def _silu(x):
    return x * _sigmoid(x)


def _dot(a, b):
    return jnp.dot(a, b, preferred_element_type=F32)


def _dot_nt(a, b):
    return lax.dot_general(a, b, (((1,), (1,)), ((), ())), preferred_element_type=F32)


def _dot_hi(a, b):
    return jnp.dot(a, b, precision=lax.Precision.HIGHEST, preferred_element_type=F32)


def _params(*sem):
    return pltpu.CompilerParams(dimension_semantics=sem, vmem_limit_bytes=VMEM_LIMIT)


def _rope_tables():
    t = np.arange(DEC_SEQ)
    pos = np.stack([t // GRID_W, t % GRID_W], axis=1).astype(np.float64)
    nf = HEAD_DIM // 4
    inv = ROPE_BASE ** (-np.arange(nf, dtype=np.float64) / nf)
    lane = np.arange(HEAD_W)
    j = lane % HEAD_DIM
    axis = j // (HEAD_DIM // 2)
    f = j % nf
    upper = (j % (HEAD_DIM // 2)) >= nf
    ang = pos[:, axis] * inv[f][None, :]
    cos = np.cos(ang)
    sin = np.sin(ang) * np.where(upper, 1.0, -1.0)[None, :]
    return jnp.asarray(cos, F32), jnp.asarray(sin, F32)


def _dft_tables(L):
    n = 2 * L
    k = np.arange(L)[:, None]
    t = np.arange(L)[None, :]
    ang = 2.0 * np.pi * ((k * t) % n).astype(np.float64) / n
    fwd = np.concatenate([np.cos(ang), -np.sin(ang)], axis=0)
    fwd[L, :] = (-1.0) ** np.arange(L)
    wk = np.where(np.arange(L) == 0, 1.0, 2.0)[None, :]
    ang_t = ang.T
    inv = np.concatenate([wk * np.cos(ang_t), -2.0 * np.sin(ang_t)], axis=1) / n
    inv[:, L] = ((-1.0) ** np.arange(L)) / n
    return jnp.asarray(fwd, F32), jnp.asarray(inv, F32)


def _hyena_tables(L):
    pos = np.arange(L, dtype=np.float64)
    t = pos / float(max(L - 1, 1))
    bands = np.linspace(1e-4, HY_BANDS - 1, HY_BANDS)
    ang = (2.0 * math.pi / L) * pos[:, None] * bands[None, :]
    z = np.concatenate([t[:, None], np.cos(ang), np.sin(ang)], axis=-1)
    lo = abs(math.log(HY_DECAY_TARGET) / HY_DECAY_SLOW)
    hi = abs(math.log(HY_DECAY_TARGET) / HY_DECAY_FAST)
    deltas = np.linspace(lo, hi, HY_W)
    decay = np.exp(-t[:, None] * deltas[None, :])
    return jnp.asarray(z, F32), jnp.asarray(decay, F32)


def _ada_kernel(c_ref, w_ref, b_ref, o_ref):
    s = _silu(c_ref[...])
    o_ref[...] = _dot(s.astype(BF16), w_ref[...].astype(BF16)) + b_ref[...]


def _ada(cond, w_ada, b_ada):
    tn = 1024
    return pl.pallas_call(
        _ada_kernel,
        grid=(DEPTH, 3 * D_MODEL // tn),
        in_specs=[
            pl.BlockSpec((N_COND, D_MODEL), lambda l, j: (0, 0)),
            pl.BlockSpec((None, D_MODEL, tn), lambda l, j: (l, 0, j)),
            pl.BlockSpec((None, 1, tn), lambda l, j: (l, 0, j)),
        ],
        out_specs=pl.BlockSpec((None, N_COND, tn), lambda l, j: (l, 0, j)),
        out_shape=jax.ShapeDtypeStruct((DEPTH, N_COND, 3 * D_MODEL), F32),
        compiler_params=_params("arbitrary", "arbitrary"),
        name="ada",
    )(cond, w_ada, b_ada.reshape(DEPTH, 1, 3 * D_MODEL))


def _cond_row(seq_len, tm):
    if seq_len is None:
        return lambda i: 0
    return lambda i: 1 + i // (seq_len // tm)


IN_TM = 1024
NORM_ROWS = 32


def _norm_modulate(x_ref, g_ref, shift_ref, scale_ref, h_ref, h_row0=0, n_rows=IN_TM):
    g = g_ref[...]
    sc = 1.0 + scale_ref[...]
    sh = shift_ref[...]

    def body(r, carry):
        start = pl.multiple_of(r * NORM_ROWS, NORM_ROWS)
        x = x_ref[pl.ds(start, NORM_ROWS), :]
        ms = jnp.mean(x * x, axis=-1, keepdims=True)
        xn = x * lax.rsqrt(ms + EPS)
        h_ref[pl.ds(pl.multiple_of(h_row0 + start, NORM_ROWS), NORM_ROWS), :] = ((xn * g) * sc + sh).astype(BF16)
        return carry

    lax.fori_loop(0, n_rows // NORM_ROWS, body, 0, unroll=4)


CTX_TM = 2 * SEQ
N_REST = N_SLABS - 1 - S_AG


def _inproj_attn_kernel(*refs, first_layer, lam_init):
    if first_layer:
        (x_ref, g_ref, shift_ref, scale_ref, w_ref, lamp_ref, subln_ref,
         o_ref, kc_ref, vc_ref, att_ref, h_ref, qkv_ref, ag_ref) = refs
    else:
        (x_ref, g_ref, shift_ref, scale_ref, w_ref, lamp_ref, subln_ref, _, _,
         o_ref, kc_ref, vc_ref, att_ref, h_ref, qkv_ref, ag_ref) = refs
    j = pl.program_id(1)

    @pl.when(j == 0)
    def _():
        _norm_modulate(x_ref, g_ref, shift_ref, scale_ref, h_ref, n_rows=CTX_TM)

    def project():
        return _dot(h_ref[...], w_ref[...])

    def scatter(dst_ref, res):
        for b in range(CTX_TM // SEQ):
            for h in range(N_HEADS):
                dst_ref[b, pl.ds(h, SEQ, stride=N_HEADS), :] = res[b * SEQ:(b + 1) * SEQ, h * HEAD_W:(h + 1) * HEAD_W]

    @pl.when(j == S_Q)
    def _():
        qkv_ref[S_Q] = (project() * (HEAD_DIM ** -0.5)).astype(BF16)

    @pl.when(j == S_K)
    def _():
        res = project()
        qkv_ref[S_K] = res.astype(BF16)
        scatter(kc_ref, res)

    @pl.when(j == S_V)
    def _():
        res = project()
        qkv_ref[S_V] = res.astype(BF16)
        scatter(vc_ref, res)
        if first_layer:
            kc_ref[...] = jnp.zeros(kc_ref.shape, F32)

    @pl.when(j == S_AG)
    def _():
        ag_ref[...] = project()
        if first_layer:
            vc_ref[...] = jnp.zeros(vc_ref.shape, F32)

    def store_projection():
        o_ref[...] = project()

    def attend_and_project(seq):
        rows = slice(seq * SEQ, (seq + 1) * SEQ)
        cols = [slice(h * HEAD_W, (h + 1) * HEAD_W) for h in range(N_HEADS)]
        lam = _lam_value(lamp_ref, lam_init)
        items = [(qkv_ref[S_Q, rows, c], qkv_ref[S_K, rows, c], qkv_ref[S_V, rows, c]) for c in cols]
        for c, o in zip(cols, _diff_attend(items, lam, subln_ref[...], lam_init, after_scores=store_projection)):
            att_ref[rows, c] = (o * _silu(ag_ref[rows, c])).astype(BF16)

    for seq in range(CTX_TM // SEQ):
        @pl.when(j == S_AG + 1 + seq)
        def _(seq=seq):
            attend_and_project(seq)

    @pl.when(j > S_AG + CTX_TM // SEQ)
    def _():
        o_ref[...] = project()


def _of_layer(layer, shape):
    return pl.BlockSpec((None,) + tuple(shape), lambda *_: (layer,) + (0,) * len(shape))


W32_K_CHUNK = 512


def _inproj_w32_kernel(x_ref, g_ref, shift_ref, scale_ref, w_ref, o_ref, wbf_ref, h_ref):
    row0 = pl.multiple_of(pl.program_id(1) * IN_TM, IN_TM)

    @pl.when(pl.program_id(0) == 0)
    def _():
        _norm_modulate(x_ref, g_ref, shift_ref, scale_ref, h_ref, row0)

    for k0 in range(0, D_MODEL, W32_K_CHUNK):
        w = w_ref[k0:k0 + W32_K_CHUNK, :].astype(BF16)
        wbf_ref[k0:k0 + W32_K_CHUNK, :] = w
        part = _dot(h_ref[pl.ds(row0, IN_TM), k0:k0 + W32_K_CHUNK], w)
        if k0 == 0:
            o_ref[...] = part
        else:
            o_ref[...] += part


def _inproj_w32(x, norm_g, mod4, w_in, layer, seq_len):
    n_tok = x.shape[0]
    tm = IN_TM
    n_tiles = n_tok // tm
    cond = _cond_row(seq_len, tm)
    x_row = lambda j, i: jnp.where(j == 0, i, n_tiles - 1)
    return pl.pallas_call(
        _inproj_w32_kernel,
        grid=(N_SLABS, n_tiles),
        in_specs=[
            pl.BlockSpec((tm, D_MODEL), lambda j, i: (x_row(j, i), 0), pipeline_mode=pl.Buffered(1)),
            _of_layer(layer, (1, D_MODEL)),
            pl.BlockSpec((None, None, 1, D_MODEL), lambda j, i: (layer, cond(i), 0, 0)),
            pl.BlockSpec((None, None, 1, D_MODEL), lambda j, i: (layer, cond(i), 0, 1)),
            pl.BlockSpec((None, D_MODEL, SLAB_W), lambda j, i: (layer, 0, j)),
        ],
        out_specs=[
            pl.BlockSpec((None, tm, SLAB_W), lambda j, i: (j, i, 0)),
            pl.BlockSpec((D_MODEL, SLAB_W), lambda j, i: (0, j)),
        ],
        out_shape=[jax.ShapeDtypeStruct((N_SLABS, n_tok, SLAB_W), F32), jax.ShapeDtypeStruct((D_MODEL, IN_W), BF16)],
        scratch_shapes=[pltpu.VMEM((n_tok, D_MODEL), BF16)],
        compiler_params=_params("arbitrary", "arbitrary"),
        name="inproj_w32",
    )(x, norm_g, mod4, mod4, w_in)


def _inproj_attn(x, norm_g, mod4, w_in_bf, lamp, subln_g, layer, lam_init, caches):
    tm = CTX_TM
    nb = tm // SEQ
    first = caches is None
    cache_shape = jax.ShapeDtypeStruct((BATCH, DEPTH, SEQ * N_HEADS, HEAD_W), F32)
    if first:
        half = lambda slab: (lambda i, j: (i, jnp.where(j <= slab, layer, layer + 1), 0, 0))
    else:
        half = lambda slab: (lambda i, j: (i, layer, 0, 0))
    cache_spec = lambda slab: pl.BlockSpec((nb, None, SEQ * N_HEADS, HEAD_W), half(slab))
    in_specs = [
        pl.BlockSpec((tm, D_MODEL), lambda i, j: (i, 0)),
        _of_layer(layer, (1, D_MODEL)),
        pl.BlockSpec((None, None, 1, D_MODEL), lambda i, j: (layer, 0, 0, 0)),
        pl.BlockSpec((None, None, 1, D_MODEL), lambda i, j: (layer, 0, 0, 1)),
        pl.BlockSpec((D_MODEL, SLAB_W), lambda i, j: (0, j)),
        _of_layer(layer, (4, HEAD_DIM)),
        _of_layer(layer, (1, HEAD_W)),
    ]
    args = (x, norm_g, mod4, mod4, w_in_bf, lamp, subln_g)
    aliases = {}
    if not first:
        in_specs += [pl.BlockSpec(memory_space=pl.ANY)] * 2
        aliases = {len(args): 1, len(args) + 1: 2}
        args += tuple(caches)
    return pl.pallas_call(
        functools.partial(_inproj_attn_kernel, first_layer=first, lam_init=lam_init),
        grid=(T_CTX // tm, N_SLABS),
        in_specs=in_specs,
        out_specs=[
            pl.BlockSpec((None, tm, SLAB_W), lambda i, j: (jnp.maximum(j - (S_AG + 1), 0), i, 0)),
            cache_spec(S_K), cache_spec(S_V),
            pl.BlockSpec((tm, ATT_W), lambda i, j: (i, 0)),
        ],
        out_shape=[
            jax.ShapeDtypeStruct((N_REST, T_CTX, SLAB_W), F32), cache_shape, cache_shape,
            jax.ShapeDtypeStruct((T_CTX, ATT_W), BF16),
        ],
        scratch_shapes=[
            pltpu.VMEM((tm, D_MODEL), BF16),
            pltpu.VMEM((S_AG, tm, SLAB_W), BF16),
            pltpu.VMEM((tm, SLAB_W), F32),
        ],
        input_output_aliases=aliases,
        compiler_params=_params("arbitrary", "arbitrary"),
        name="inproj_attn",
    )(*args)


def _lam_value(lamp_ref, lam_init):
    lp = lamp_ref[...]
    t1 = jnp.sum(lp[0:1] * lp[1:2], axis=-1, keepdims=True)
    t2 = jnp.sum(lp[2:3] * lp[3:4], axis=-1, keepdims=True)
    return jnp.exp(t1) - jnp.exp(t2) + lam_init


def _diff_attend(items, lam, g, lam_init, after_scores=None):
    m = items[0][0].shape[0]
    lane = lax.broadcasted_iota(jnp.int32, (m, HEAD_W), 1)
    ss = []
    for q, k_bf, _ in items:
        q1 = jnp.where(lane < HEAD_DIM, q, 0.0).astype(BF16)
        q2 = jnp.where(lane < HEAD_DIM, 0.0, q).astype(BF16)
        ss.append(_dot_nt(jnp.concatenate([q1, q2], axis=0), k_bf))
    if after_scores is not None:
        after_scores()
    es = [jnp.exp(s - jnp.max(s, axis=-1, keepdims=True)) for s in ss]
    ps = [e * (1.0 / jnp.sum(e, axis=-1, keepdims=True)) for e in es]
    ws = [(p[:m] - lam * p[m:]).astype(BF16) for p in ps]
    os_ = [_dot(w, v_bf) for w, (_, _, v_bf) in zip(ws, items)]
    outs = []
    for o in os_:
        ms = jnp.mean(o * o, axis=-1, keepdims=True)
        outs.append((o * lax.rsqrt(ms + EPS) * g) * (1.0 - lam_init))
    return outs


Q_CHUNK = 512
Q_PART = 256
N_QC = DEC_SEQ // Q_CHUNK


def _rope(x, cos, sin):
    lane = lax.broadcasted_iota(jnp.int32, x.shape, 1)
    lower = (lane % (HEAD_DIM // 2)) < (HEAD_DIM // 4)
    nf = HEAD_DIM // 4
    partner = jnp.where(lower, pltpu.roll(x, HEAD_W - nf, 1), pltpu.roll(x, nf, 1))
    return x * cos + partner * sin


def _attn_smp_kernel(lamp_ref, g_ref, cosq_ref, sinq_ref, cos_ref, sin_ref, q_ref, k_ref, v_ref,
                     ck_ref, cv_ref, ag_ref, o_ref, ks_ref, vs_ref, *, lam_init):
    @pl.when(pl.program_id(2) == 0)
    def _():
        head_rows = pl.ds(pl.program_id(1), PAST_LEN, stride=N_HEADS)
        ks_ref[0:PAST_LEN, :] = ck_ref[head_rows, :].astype(BF16)
        vs_ref[0:PAST_LEN, :] = cv_ref[head_rows, :].astype(BF16)
        ks_ref[PAST_LEN:, :] = _rope(k_ref[...], cos_ref[...], sin_ref[...]).astype(BF16)
        vs_ref[PAST_LEN:, :] = v_ref[...].astype(BF16)

    lam = _lam_value(lamp_ref, lam_init)
    q = _rope(q_ref[...], cosq_ref[...], sinq_ref[...]) * (HEAD_DIM ** -0.5)
    k_bf, v_bf = ks_ref[...], vs_ref[...]
    parts = [slice(r, r + Q_PART) for r in range(0, Q_CHUNK, Q_PART)]
    items = [(q[p], k_bf, v_bf) for p in parts]
    for p, o in zip(parts, _diff_attend(items, lam, g_ref[...], lam_init)):
        o_ref[p, :] = (o * _silu(ag_ref[p, :])).astype(BF16)


def _attn_smp(proj, cache_k, cache_v, lamp, subln_g, cos, sin, layer, lam_init):
    full = lambda shape: pl.BlockSpec(shape, lambda b, h, c: (0, 0))
    return pl.pallas_call(
        functools.partial(_attn_smp_kernel, lam_init=lam_init),
        grid=(DEC_BATCH, N_HEADS, N_QC),
        in_specs=[
            _of_layer(layer, (4, HEAD_DIM)),
            _of_layer(layer, (1, HEAD_W)),
            pl.BlockSpec((Q_CHUNK, HEAD_W), lambda b, h, c: (c, 0)),
            pl.BlockSpec((Q_CHUNK, HEAD_W), lambda b, h, c: (c, 0)),
            full((DEC_SEQ, HEAD_W)),
            full((DEC_SEQ, HEAD_W)),
            pl.BlockSpec((None, Q_CHUNK, HEAD_W), lambda b, h, c: (S_Q, b * N_QC + c, h)),
            pl.BlockSpec((None, DEC_SEQ, HEAD_W), lambda b, h, c: (S_K, b, h)),
            pl.BlockSpec((None, DEC_SEQ, HEAD_W), lambda b, h, c: (S_V, b, h)),
            pl.BlockSpec((None, None, PAST_LEN * N_HEADS, HEAD_W), lambda b, h, c: (b, layer, 0, 0)),
            pl.BlockSpec((None, None, PAST_LEN * N_HEADS, HEAD_W), lambda b, h, c: (b, layer, 0, 0)),
            pl.BlockSpec((None, Q_CHUNK, HEAD_W), lambda b, h, c: (S_AG, b * N_QC + c, h)),
        ],
        out_specs=pl.BlockSpec((Q_CHUNK, HEAD_W), lambda b, h, c: (b * N_QC + c, h)),
        out_shape=jax.ShapeDtypeStruct((T_SMP, ATT_W), BF16),
        scratch_shapes=[
            pltpu.VMEM((PAST_LEN + DEC_SEQ, HEAD_W), BF16),
            pltpu.VMEM((PAST_LEN + DEC_SEQ, HEAD_W), BF16),
        ],
        compiler_params=_params("arbitrary", "arbitrary", "arbitrary"),
        name="attn_smp",
    )(lamp, subln_g, cos, sin, cos, sin, proj, proj, proj, cache_k, cache_v, proj)


PAD = 8
GATE_ROWS = 256
LRU_SLABS = LRU_W // LANES


def _lru_kernel(x_ref, lg_ref, cw_ref, cb_ref, wa_ref, wi_ref, ba_ref, bi_ref, lam_ref, h0_ref, o_ref, hl_ref,
                xpad_ref, a_ref, b_ref, h_ref, w_ref, *, L, G):
    @pl.when(pl.program_id(0) == 0)
    def _():
        w_ref[...] = jnp.zeros(w_ref.shape, BF16)
        for d in range(2):
            for gate, src in enumerate((wa_ref, wi_ref)):
                col0 = (2 * d + gate) * LRU_W
                for n in range(LRU_BLOCKS):
                    lo = n * LRU_BW
                    w_ref[lo:lo + LRU_BW, col0 + lo:col0 + lo + LRU_BW] = src[d, n].astype(BF16)

    zeros = jnp.zeros((PAD, LRU_W), F32)
    xpad_ref[0:PAD, :] = zeros
    xpad_ref[PAD + L:, :] = zeros
    nl = -lam_ref[...]
    neg_c = (-LRU_C * LOG2E) * (jnp.maximum(nl, 0.0) + jnp.log1p(jnp.exp(-jnp.abs(nl))))
    cw = cw_ref[...]
    cb = cb_ref[...]
    ba = ba_ref[...]
    bi = bi_ref[...]

    def gates_of_sequence(g, carry):
        xpad_ref[PAD:PAD + L, :] = x_ref[pl.ds(pl.multiple_of(g * L, L), L), :]
        for c in range(L // GATE_ROWS):
            r0 = c * GATE_ROWS
            xc = cb + cw[0:1] * xpad_ref[PAD - 2 + r0:PAD - 2 + r0 + GATE_ROWS, :]
            xc = xc + cw[1:2] * xpad_ref[PAD - 1 + r0:PAD - 1 + r0 + GATE_ROWS, :]
            xc = xc + cw[2:3] * xpad_ref[PAD + r0:PAD + r0 + GATE_ROWS, :]
            xc = xc + cw[3:4] * xpad_ref[PAD + 1 + r0:PAD + 1 + r0 + GATE_ROWS, :]
            gates = _dot(xc.astype(BF16), w_ref[...])
            for d in range(2):
                r = _sigmoid(gates[:, (2 * d) * LRU_W:(2 * d + 1) * LRU_W] + ba[d:d + 1])
                i = _sigmoid(gates[:, (2 * d + 1) * LRU_W:(2 * d + 2) * LRU_W] + bi[d:d + 1])
                a = jnp.exp2(r * neg_c[d:d + 1])
                a2 = 1.0 - a * a
                b = jnp.where(a2 > 0.0, a2 * lax.rsqrt(a2), 0.0) * (i * xc)
                rows = pl.ds(r0 * G + g, GATE_ROWS, stride=G)
                for s in range(LRU_SLABS):
                    a_ref[d * LRU_SLABS + s, rows, :] = a[:, s * LANES:(s + 1) * LANES]
                    b_ref[d * LRU_SLABS + s, rows, :] = b[:, s * LANES:(s + 1) * LANES]
        return carry

    lax.fori_loop(0, G, gates_of_sequence, 0)

    def step(t, hs):
        out = []
        for d in range(2):
            tt = t if d == 0 else L - 1 - t
            rows = pl.ds(pl.multiple_of(tt * G, G), G)
            for s in range(LRU_SLABS):
                k = d * LRU_SLABS + s
                h = a_ref[k, rows, :] * hs[k] + b_ref[k, rows, :]
                h_ref[k, rows, :] = h
                out.append(h)
        return tuple(out)

    h0 = tuple(h0_ref[d, :, s * LANES:(s + 1) * LANES] for d in range(2) for s in range(LRU_SLABS))
    hs = lax.fori_loop(0, L, step, h0, unroll=2)
    for d in range(2):
        for s in range(LRU_SLABS):
            hl_ref[d, :, s * LANES:(s + 1) * LANES] = hs[d * LRU_SLABS + s]

    def write_sequence(g, carry):
        rows = pl.ds(g, L, stride=G)
        orow = pl.ds(pl.multiple_of(g * L, L), L)
        for s in range(LRU_SLABS):
            lanes = slice(s * LANES, (s + 1) * LANES)
            h = h_ref[s, rows, :] + h_ref[LRU_SLABS + s, rows, :]
            o_ref[orow, lanes] = (h * _silu(lg_ref[orow, lanes])).astype(BF16)
        return carry

    lax.fori_loop(0, G, write_sequence, 0)


def _lru(proj, conv_w, conv_b, wa, wi, ba, bi, lam, h0, layer, h0_layer, L, G, slab0=0):
    nseq = h0.shape[2]
    return pl.pallas_call(
        functools.partial(_lru_kernel, L=L, G=G),
        grid=(nseq // G,),
        in_specs=[
            pl.BlockSpec((None, G * L, LRU_W), lambda s: (S_LX_LG - slab0, s, 0)),
            pl.BlockSpec((None, G * L, LRU_W), lambda s: (S_LX_LG - slab0, s, 1)),
            _of_layer(layer, (4, LRU_W)),
            _of_layer(layer, (1, LRU_W)),
            _of_layer(layer, (2, LRU_BLOCKS, LRU_BW, LRU_BW)),
            _of_layer(layer, (2, LRU_BLOCKS, LRU_BW, LRU_BW)),
            _of_layer(layer, (2, LRU_W)),
            _of_layer(layer, (2, LRU_W)),
            _of_layer(layer, (2, LRU_W)),
            pl.BlockSpec((None, 2, G, LRU_W), lambda s: (h0_layer, 0, s, 0)),
        ],
        out_specs=[
            pl.BlockSpec((G * L, LRU_W), lambda s: (s, 0)),
            pl.BlockSpec((2, G, LRU_W), lambda s: (0, s, 0)),
        ],
        out_shape=[
            jax.ShapeDtypeStruct((nseq * L, LRU_W), BF16),
            jax.ShapeDtypeStruct((2, nseq, LRU_W), F32),
        ],
        scratch_shapes=[
            pltpu.VMEM((L + 2 * PAD, LRU_W), F32),
            pltpu.VMEM((2 * LRU_SLABS, L * G, LANES), F32),
            pltpu.VMEM((2 * LRU_SLABS, L * G, LANES), F32),
            pltpu.VMEM((2 * LRU_SLABS, L * G, LANES), F32),
            pltpu.VMEM((LRU_W, 4 * LRU_W), BF16),
        ],
        compiler_params=_params("arbitrary"),
        name=f"lru_{L}",
    )(proj, proj, conv_w, conv_b, wa, wi, ba, bi, lam, h0)


FILT_ROWS = 256


def _filt_kernel(z_ref, decay_ref, w1_ref, b1_ref, w2_ref, b2_ref, w3_ref, ft_ref, fb_ref, g_ref, taps_ref, *, L):
    i = pl.program_id(0)

    @pl.when(i == 0)
    def _():
        h = jnp.sin(_dot_hi(z_ref[...], w1_ref[...]) + b1_ref[...])
        h = jnp.sin(_dot_hi(h, w2_ref[...]) + b2_ref[...])
        filt = _dot_hi(h, w3_ref[...])
        decay = decay_ref[...]
        row = lax.broadcasted_iota(jnp.int32, (L, HY_W), 0)
        taps_ref[:, :HY_W] = (filt[:, :HY_W] * decay).astype(BF16)
        taps_ref[:, HY_W:] = jnp.where(row == 0, 0.0, filt[:, HY_W:] * decay).astype(BF16)

    taps = taps_ref[...]
    top = _dot(ft_ref[...], taps)
    bot = _dot(fb_ref[...], taps)
    first = (lax.broadcasted_iota(jnp.int32, (FILT_ROWS, HY_W), 0) + i * FILT_ROWS) == 0
    gt = top[:, :HY_W] + top[:, HY_W:]
    g_ref[0] = gt
    g_ref[1] = jnp.where(first, 0.0, bot[:, :HY_W] - bot[:, HY_W:])
    g_ref[2] = jnp.where(first, bot[:, :HY_W] + bot[:, HY_W:], gt)


def _filters(z, decay, w1, b1, w2, b2, w3, fwd_bf, layer, L):
    full = lambda shape: pl.BlockSpec(shape, lambda i: (0,) * len(shape))
    nblk = L // FILT_ROWS
    return pl.pallas_call(
        functools.partial(_filt_kernel, L=L),
        grid=(nblk,),
        in_specs=[
            full((L, HY_POS)), full((L, HY_W)),
            _of_layer(layer, (HY_POS, HY_HIDDEN)), _of_layer(layer, (1, HY_HIDDEN)),
            _of_layer(layer, (HY_HIDDEN, HY_HIDDEN)), _of_layer(layer, (1, HY_HIDDEN)),
            _of_layer(layer, (HY_HIDDEN, 2 * HY_W)),
            pl.BlockSpec((FILT_ROWS, L), lambda i: (i, 0)),
            pl.BlockSpec((FILT_ROWS, L), lambda i: (nblk + i, 0)),
        ],
        out_specs=pl.BlockSpec((3, FILT_ROWS, HY_W), lambda i: (0, i, 0)),
        out_shape=jax.ShapeDtypeStruct((3, L, HY_W), F32),
        scratch_shapes=[pltpu.VMEM((L, 2 * HY_W), BF16)],
        compiler_params=_params("arbitrary"),
        name=f"hyena_filters_{L}",
    )(z, decay, w1, b1, w2, b2, w3, fwd_bf, fwd_bf)


HY_PARAM_ROWS = 16


def _hyena_kernel(hv_ref, hx1_ref, hx0_ref, hg_ref, prm_ref, g_ref, f_ref, fi_ref, o_ref, xpad_ref, *, L, ch, seqs):
    zeros = jnp.zeros((PAD, ch), F32)
    for s in range(seqs):
        xpad_ref[s, 0:PAD, :] = zeros
        xpad_ref[s, PAD + L:, :] = zeros
    prm = prm_ref[...]
    rows = [slice(s * L, (s + 1) * L) for s in range(seqs)]

    def conv(ref, stream, s):
        xpad_ref[s, PAD:PAD + L, :] = ref[rows[s], :]
        u = prm[9 + stream:10 + stream] + prm[3 * stream:3 * stream + 1] * xpad_ref[s, PAD - 1:PAD - 1 + L, :]
        u = u + prm[3 * stream + 1:3 * stream + 2] * xpad_ref[s, PAD:PAD + L, :]
        return u + prm[3 * stream + 2:3 * stream + 3] * xpad_ref[s, PAD + 1:PAD + 1 + L, :]

    zs = [conv(hx1_ref, 1, s) * conv(hv_ref, 0, s) for s in range(seqs)]
    specs = [_dot(f_ref[...], z.astype(BF16)) for z in zs]
    gx = g_ref[1]
    prods = []
    for spec in specs:
        top, bot = spec[:L], spec[L:]
        yt = top * g_ref[0] - bot * gx
        yb = top * gx + bot * g_ref[2]
        prods.append(jnp.concatenate([yt, yb], axis=0).astype(BF16))
    ys = [_dot(fi_ref[...], p) for p in prods]
    for s in range(seqs):
        hy = conv(hx0_ref, 2, s) * (ys[s] + zs[s] * prm[12:13])
        o_ref[rows[s], :] = (hy * _silu(hg_ref[rows[s], :])).astype(BF16)


def _hyena(proj, prm, g, fwd, inv, layer, L, nseq, ch, seqs, slab0=0):
    nch = HY_W // ch
    full = lambda shape: pl.BlockSpec(shape, lambda s, j: (0,) * len(shape))
    col = lambda slab, off: pl.BlockSpec((None, seqs * L, ch), lambda s, j, slab=slab, off=off: (slab - slab0, s, off * nch + j))
    return pl.pallas_call(
        functools.partial(_hyena_kernel, L=L, ch=ch, seqs=seqs),
        grid=(nseq // seqs, nch),
        in_specs=[
            col(S_HV_HX1, 0), col(S_HV_HX1, 1), col(S_HX0_HG, 0), col(S_HX0_HG, 1),
            pl.BlockSpec((None, HY_PARAM_ROWS, ch), lambda s, j: (layer, 0, j)),
            pl.BlockSpec((3, L, ch), lambda s, j: (0, 0, j)),
            full((2 * L, L)), full((L, 2 * L)),
        ],
        out_specs=pl.BlockSpec((seqs * L, ch), lambda s, j: (s, j)),
        out_shape=jax.ShapeDtypeStruct((nseq * L, HY_W), BF16),
        scratch_shapes=[pltpu.VMEM((seqs, L + 2 * PAD, ch), F32)],
        compiler_params=_params("arbitrary", "arbitrary"),
        name=f"hyena_{L}",
    )(proj, proj, proj, proj, prm, g, fwd, inv)


def _hyena_params(conv_w, conv_b, d):
    rows = [conv_w[:, :, s * HY_W:(s + 1) * HY_W] for s in range(3)]
    rows += [conv_b[:, None, s * HY_W:(s + 1) * HY_W] for s in range(3)]
    rows += [d[:, None, :], jnp.zeros((DEPTH, HY_PARAM_ROWS - 13, HY_W), F32)]
    return jnp.concatenate(rows, axis=1)


OUT_TM = 512


def _outproj_kernel(x_ref, gate_ref, att_ref, lru_ref, hy_ref, w_ref, fg_ref, o_ref, *, final):
    acc = _dot(att_ref[...], w_ref[0:ATT_W, :])
    acc = acc + _dot(lru_ref[...], w_ref[ATT_W:ATT_W + LRU_W, :])
    acc = acc + _dot(hy_ref[...], w_ref[ATT_W + LRU_W:, :])
    y = x_ref[...] + gate_ref[...] * acc
    if final:
        ms = jnp.mean(y * y, axis=-1, keepdims=True)
        y = (y * lax.rsqrt(ms + EPS)) * fg_ref[...]
    o_ref[...] = y


def _outproj(x, mod4, att, lru, hy, w_out_bf, final_g, layer, final, seq_len):
    n_tok = x.shape[0]
    tm = OUT_TM
    cond = _cond_row(seq_len, tm)
    return pl.pallas_call(
        functools.partial(_outproj_kernel, final=final),
        grid=(n_tok // tm,),
        in_specs=[
            pl.BlockSpec((tm, D_MODEL), lambda i: (i, 0)),
            pl.BlockSpec((None, None, 1, D_MODEL), lambda i: (layer, cond(i), 0, 2)),
            pl.BlockSpec((tm, ATT_W), lambda i: (i, 0)),
            pl.BlockSpec((tm, LRU_W), lambda i: (i, 0)),
            pl.BlockSpec((tm, HY_W), lambda i: (i, 0)),
            pl.BlockSpec((None, D_MODEL, D_MODEL), lambda i: (layer, 0, 0), pipeline_mode=pl.Buffered(1)),
            pl.BlockSpec((1, D_MODEL), lambda i: (0, 0)),
        ],
        out_specs=pl.BlockSpec((tm, D_MODEL), lambda i: (i, 0)),
        out_shape=jax.ShapeDtypeStruct((n_tok, D_MODEL), F32),
        compiler_params=_params("arbitrary"),
        name="outproj",
    )(x, mod4, att, lru, hy, w_out_bf, final_g)


def kernel(x_prompt, x_sample, cache_k, cache_v, state_lru, c, c_ctx, norm_g, w_ada, b_ada, w_in, w_out, lam_q1, lam_k1, lam_q2, lam_k2, attn_subln_g, lru_conv_w, lru_conv_b, lru_wa, lru_ba, lru_wi, lru_bi, lru_lam, hy_conv_w, hy_conv_b, hy_w1, hy_b1, hy_w2, hy_b2, hy_w3, hy_d, final_g):
    cos, sin = _rope_tables()
    tables = {}
    for L in (SEQ, DEC_SEQ):
        fwd, inv = _dft_tables(L)
        z, decay = _hyena_tables(L)
        tables[L] = (fwd.astype(BF16), inv.astype(BF16), z, decay)

    cond = jnp.concatenate([c_ctx[None, :], c, jnp.zeros((N_COND - 1 - DEC_BATCH, D_MODEL), F32)], axis=0)
    mod4 = _ada(cond, w_ada, b_ada).reshape(DEPTH, N_COND, 1, 3 * D_MODEL)

    xc = x_prompt.reshape(T_CTX, D_MODEL)
    xs = x_sample.reshape(T_SMP, D_MODEL)
    ck = cache_k.reshape(DEC_BATCH, DEPTH, PAST_LEN * N_HEADS, HEAD_W)
    cv = cache_v.reshape(DEC_BATCH, DEPTH, PAST_LEN * N_HEADS, HEAD_W)
    h0_ctx = jnp.zeros((1, 2, BATCH, LRU_W), F32)
    h0_smp = jnp.transpose(state_lru, (1, 2, 0, 3))
    norm_g3 = norm_g[:, None, :]
    lamp = jnp.stack([lam_q1, lam_k1, lam_q2, lam_k2], axis=1)
    subln = attn_subln_g[:, None, :]
    w_out_bf = w_out.astype(BF16)
    lru_args = (lru_conv_w, lru_conv_b[:, None, :], lru_wa, lru_wi, lru_ba, lru_bi, lru_lam)
    hy_prm = _hyena_params(hy_conv_w, hy_conv_b, hy_d)
    filt_args = (hy_w1, hy_b1[:, None, :], hy_w2, hy_b2[:, None, :], hy_w3)
    fg = final_g[None, :]

    assert DEPTH == 2
    caches, hs = None, []
    for l in range(DEPTH):
        final = l == DEPTH - 1
        lam_init = 0.8 - 0.6 * math.exp(-0.3 * l)

        fwd, inv, z, decay = tables[DEC_SEQ]
        proj, w_in_bf = _inproj_w32(xs, norm_g3, mod4, w_in, l, DEC_SEQ)
        att = _attn_smp(proj, ck, cv, lamp, subln, cos, sin, l, lam_init)
        lru, _ = _lru(proj, *lru_args, h0_smp, l, l, DEC_SEQ, DEC_BATCH)
        g = _filters(z, decay, *filt_args, fwd, l, DEC_SEQ)
        hy = _hyena(proj, hy_prm, g, fwd, inv, l, DEC_SEQ, DEC_BATCH, HY_W // 2, 1)
        xs = _outproj(xs, mod4, att, lru, hy, w_out_bf, fg, l, final, DEC_SEQ)

        fwd, inv, z, decay = tables[SEQ]
        proj, k_cache, v_cache, att = _inproj_attn(xc, norm_g3, mod4, w_in_bf, lamp, subln, l, lam_init, caches)
        caches = (k_cache, v_cache)
        lru, h_last = _lru(proj, *lru_args, h0_ctx, l, 0, SEQ, 8, slab0=S_AG + 1)
        hs.append(h_last)
        g = _filters(z, decay, *filt_args, fwd, l, SEQ)
        hy = _hyena(proj, hy_prm, g, fwd, inv, l, SEQ, BATCH, HY_W, 2, slab0=S_AG + 1)
        xc = _outproj(xc, mod4, att, lru, hy, w_out_bf, fg, l, final, None)

    y_prompt = xc.reshape(BATCH, SEQ, D_MODEL)
    y_sample = xs.reshape(DEC_BATCH, DEC_SEQ, D_MODEL)
    new_k, new_v = (a.reshape(BATCH, DEPTH, SEQ, N_HEADS, HEAD_W) for a in caches)
    new_state = jnp.transpose(jnp.stack(hs, axis=0), (2, 0, 1, 3))
    return (y_prompt, y_sample, new_k, new_v, new_state)
```

```python
import functools
import math

import numpy as np
import jax
import jax.numpy as jnp
from jax import lax
from jax.experimental import pallas as pl
from jax.experimental.pallas import tpu as pltpu

D_MODEL = 2048
BATCH = 32
SEQ = 256
DEPTH = 2
DEC_BATCH = 2
DEC_SEQ = 1024
PAST_LEN = 512
GRID_W = 64
ATT_W = 1024
LRU_W = 512
HY_W = 512
HEAD_DIM = 64
N_HEADS = 8
HEAD_W = 2 * HEAD_DIM
LRU_BLOCKS = 8
LRU_BW = LRU_W // LRU_BLOCKS
LRU_C = 8.0
HY_BANDS = 16
HY_POS = 1 + 2 * HY_BANDS
HY_HIDDEN = 64
HY_DECAY_FAST = 0.3
HY_DECAY_SLOW = 1.5
HY_DECAY_TARGET = 1e-2
ROPE_BASE = 10000.0
EPS = 1e-6
IN_W = 4 * ATT_W + 2 * LRU_W + 4 * HY_W

T_CTX = BATCH * SEQ
T_SMP = DEC_BATCH * DEC_SEQ
N_COND = 8

SLAB_W = 1024
N_SLABS = IN_W // SLAB_W
S_Q, S_K, S_V, S_AG, S_LX_LG, S_HV_HX1, S_HX0_HG = range(N_SLABS)

LANES = 128
F32 = jnp.float32
BF16 = jnp.bfloat16
VMEM_LIMIT = 58 * 1024 * 1024


LOG2E = 1.4426950408889634


def _sigmoid(x):
    return 1.0 / (1.0 + jnp.exp2(x * (-LOG2E)))


def _silu(x):
    return x * _sigmoid(x)


def _dot(a, b):
    return jnp.dot(a, b, preferred_element_type=F32)


def _dot_nt(a, b):
    return lax.dot_general(a, b, (((1,), (1,)), ((), ())), preferred_element_type=F32)


def _dot_hi(a, b):
    return jnp.dot(a, b, precision=lax.Precision.HIGHEST, preferred_element_type=F32)


def _params(*sem):
    return pltpu.CompilerParams(dimension_semantics=sem, vmem_limit_bytes=VMEM_LIMIT)


def _rope_tables():
    t = np.arange(DEC_SEQ)
    pos = np.stack([t // GRID_W, t % GRID_W], axis=1).astype(np.float64)
    nf = HEAD_DIM // 4
    inv = ROPE_BASE ** (-np.arange(nf, dtype=np.float64) / nf)
    lane = np.arange(HEAD_W)
    j = lane % HEAD_DIM
    axis = j // (HEAD_DIM // 2)
    f = j % nf
    upper = (j % (HEAD_DIM // 2)) >= nf
    ang = pos[:, axis] * inv[f][None, :]
    cos = np.cos(ang)
    sin = np.sin(ang) * np.where(upper, 1.0, -1.0)[None, :]
    return jnp.asarray(cos, F32), jnp.asarray(sin, F32)


def _dft_tables(L):
    n = 2 * L
    k = np.arange(L)[:, None]
    t = np.arange(L)[None, :]
    ang = 2.0 * np.pi * ((k * t) % n).astype(np.float64) / n
    fwd = np.concatenate([np.cos(ang), -np.sin(ang)], axis=0)
    fwd[L, :] = (-1.0) ** np.arange(L)
    wk = np.where(np.arange(L) == 0, 1.0, 2.0)[None, :]
    ang_t = ang.T
    inv = np.concatenate([wk * np.cos(ang_t), -2.0 * np.sin(ang_t)], axis=1) / n
    inv[:, L] = ((-1.0) ** np.arange(L)) / n
    return jnp.asarray(fwd, F32), jnp.asarray(inv, F32)


def _hyena_tables(L):
    pos = np.arange(L, dtype=np.float64)
    t = pos / float(max(L - 1, 1))
    bands = np.linspace(1e-4, HY_BANDS - 1, HY_BANDS)
    ang = (2.0 * math.pi / L) * pos[:, None] * bands[None, :]
    z = np.concatenate([t[:, None], np.cos(ang), np.sin(ang)], axis=-1)
    lo = abs(math.log(HY_DECAY_TARGET) / HY_DECAY_SLOW)
    hi = abs(math.log(HY_DECAY_TARGET) / HY_DECAY_FAST)
    deltas = np.linspace(lo, hi, HY_W)
    decay = np.exp(-t[:, None] * deltas[None, :])
    return jnp.asarray(z, F32), jnp.asarray(decay, F32)


def _ada_kernel(c_ref, w_ref, b_ref, o_ref):
    s = _silu(c_ref[...])
    o_ref[...] = _dot(s.astype(BF16), w_ref[...].astype(BF16)) + b_ref[...]


def _ada(cond, w_ada, b_ada):
    tn = 1024
    return pl.pallas_call(
        _ada_kernel,
        grid=(DEPTH, 3 * D_MODEL // tn),
        in_specs=[
            pl.BlockSpec((N_COND, D_MODEL), lambda l, j: (0, 0)),
            pl.BlockSpec((None, D_MODEL, tn), lambda l, j: (l, 0, j)),
            pl.BlockSpec((None, 1, tn), lambda l, j: (l, 0, j)),
        ],
        out_specs=pl.BlockSpec((None, N_COND, tn), lambda l, j: (l, 0, j)),
        out_shape=jax.ShapeDtypeStruct((DEPTH, N_COND, 3 * D_MODEL), F32),
        compiler_params=_params("arbitrary", "arbitrary"),
        name="ada",
    )(cond, w_ada, b_ada.reshape(DEPTH, 1, 3 * D_MODEL))


def _cond_row(seq_len, tm):
    if seq_len is None:
        return lambda i: 0
    return lambda i: 1 + i // (seq_len // tm)


IN_TM = 1024
NORM_ROWS = 32


def _norm_modulate(x_ref, g_ref, shift_ref, scale_ref, h_ref, h_row0=0):
    g = g_ref[...]
    sc = 1.0 + scale_ref[...]
    sh = shift_ref[...]

    def body(r, carry):
        start = pl.multiple_of(r * NORM_ROWS, NORM_ROWS)
        x = x_ref[pl.ds(start, NORM_ROWS), :]
        ms = jnp.mean(x * x, axis=-1, keepdims=True)
        xn = x * lax.rsqrt(ms + EPS)
        h_ref[pl.ds(pl.multiple_of(h_row0 + start, NORM_ROWS), NORM_ROWS), :] = ((xn * g) * sc + sh).astype(BF16)
        return carry

    lax.fori_loop(0, IN_TM // NORM_ROWS, body, 0, unroll=4)


def _inproj_cache_kernel(*refs, first_layer):
    if first_layer:
        x_ref, g_ref, shift_ref, scale_ref, w_ref, o_ref, kc_ref, vc_ref, h_ref = refs
    else:
        x_ref, g_ref, shift_ref, scale_ref, w_ref, _, _, o_ref, kc_ref, vc_ref, h_ref = refs
    j = pl.program_id(1)

    @pl.when(j == 0)
    def _():
        _norm_modulate(x_ref, g_ref, shift_ref, scale_ref, h_ref)

    o_ref[...] = _dot(h_ref[...], w_ref[...])

    def scatter(dst_ref):
        for b in range(IN_TM // SEQ):
            for h in range(N_HEADS):
                dst_ref[b, pl.ds(h, SEQ, stride=N_HEADS), :] = o_ref[b * SEQ:(b + 1) * SEQ, h * HEAD_W:(h + 1) * HEAD_W]

    @pl.when(j == S_K)
    def _():
        scatter(kc_ref)

    @pl.when(j == S_V)
    def _():
        scatter(vc_ref)

    if first_layer:
        @pl.when(j == S_K + 1)
        def _():
            kc_ref[...] = jnp.zeros(kc_ref.shape, F32)

        @pl.when(j == S_V + 1)
        def _():
            vc_ref[...] = jnp.zeros(vc_ref.shape, F32)


def _of_layer(layer, shape):
    return pl.BlockSpec((None,) + tuple(shape), lambda *_: (layer,) + (0,) * len(shape))


def _inproj_specs(tm, layer, cond, w_spec):
    return [
        pl.BlockSpec((tm, D_MODEL), lambda i, j: (i, 0)),
        _of_layer(layer, (1, D_MODEL)),
        pl.BlockSpec((None, None, 1, D_MODEL), lambda i, j: (layer, cond(i), 0, 0)),
        pl.BlockSpec((None, None, 1, D_MODEL), lambda i, j: (layer, cond(i), 0, 1)),
        w_spec,
    ]


W32_K_CHUNK = 512


def _inproj_w32_kernel(x_ref, g_ref, shift_ref, scale_ref, w_ref, o_ref, wbf_ref, h_ref):
    row0 = pl.multiple_of(pl.program_id(1) * IN_TM, IN_TM)

    @pl.when(pl.program_id(0) == 0)
    def _():
        _norm_modulate(x_ref, g_ref, shift_ref, scale_ref, h_ref, row0)

    for k0 in range(0, D_MODEL, W32_K_CHUNK):
        w = w_ref[k0:k0 + W32_K_CHUNK, :].astype(BF16)
        wbf_ref[k0:k0 + W32_K_CHUNK, :] = w
        part = _dot(h_ref[pl.ds(row0, IN_TM), k0:k0 + W32_K_CHUNK], w)
        if k0 == 0:
            o_ref[...] = part
        else:
            o_ref[...] += part


def _inproj_w32(x, norm_g, mod4, w_in, layer, seq_len):
    n_tok = x.shape[0]
    tm = IN_TM
    n_tiles = n_tok // tm
    cond = _cond_row(seq_len, tm)
    x_row = lambda j, i: jnp.where(j == 0, i, n_tiles - 1)
    return pl.pallas_call(
        _inproj_w32_kernel,
        grid=(N_SLABS, n_tiles),
        in_specs=[
            pl.BlockSpec((tm, D_MODEL), lambda j, i: (x_row(j, i), 0), pipeline_mode=pl.Buffered(1)),
            _of_layer(layer, (1, D_MODEL)),
            pl.BlockSpec((None, None, 1, D_MODEL), lambda j, i: (layer, cond(i), 0, 0)),
            pl.BlockSpec((None, None, 1, D_MODEL), lambda j, i: (layer, cond(i), 0, 1)),
            pl.BlockSpec((None, D_MODEL, SLAB_W), lambda j, i: (layer, 0, j)),
        ],
        out_specs=[
            pl.BlockSpec((None, tm, SLAB_W), lambda j, i: (j, i, 0)),
            pl.BlockSpec((D_MODEL, SLAB_W), lambda j, i: (0, j)),
        ],
        out_shape=[jax.ShapeDtypeStruct((N_SLABS, n_tok, SLAB_W), F32), jax.ShapeDtypeStruct((D_MODEL, IN_W), BF16)],
        scratch_shapes=[pltpu.VMEM((n_tok, D_MODEL), BF16)],
        compiler_params=_params("arbitrary", "arbitrary"),
        name="inproj_w32",
    )(x, norm_g, mod4, mod4, w_in)


def _inproj_cache(x, norm_g, mod4, w_in_bf, layer, caches):
    tm = IN_TM
    nb = tm // SEQ
    first = caches is None
    cache_shape = jax.ShapeDtypeStruct((BATCH, DEPTH, SEQ * N_HEADS, HEAD_W), F32)
    if first:
        half = lambda slab: (lambda i, j: (i, jnp.where(j <= slab, layer, layer + 1), 0, 0))
    else:
        half = lambda slab: (lambda i, j: (i, layer, 0, 0))
    cache_spec = lambda slab: pl.BlockSpec((nb, None, SEQ * N_HEADS, HEAD_W), half(slab))
    w_spec = pl.BlockSpec((D_MODEL, SLAB_W), lambda i, j: (0, j))
    in_specs = _inproj_specs(tm, layer, _cond_row(None, tm), w_spec)
    args = (x, norm_g, mod4, mod4, w_in_bf)
    aliases = {}
    if not first:
        in_specs += [pl.BlockSpec(memory_space=pl.ANY)] * 2
        aliases = {len(args): 1, len(args) + 1: 2}
        args += tuple(caches)
    return pl.pallas_call(
        functools.partial(_inproj_cache_kernel, first_layer=first),
        grid=(T_CTX // tm, N_SLABS),
        in_specs=in_specs,
        out_specs=[pl.BlockSpec((None, tm, SLAB_W), lambda i, j: (j, i, 0)), cache_spec(S_K), cache_spec(S_V)],
        out_shape=[jax.ShapeDtypeStruct((N_SLABS, T_CTX, SLAB_W), F32), cache_shape, cache_shape],
        scratch_shapes=[pltpu.VMEM((tm, D_MODEL), BF16)],
        input_output_aliases=aliases,
        compiler_params=_params("arbitrary", "arbitrary"),
        name="inproj_cache",
    )(*args)


def _lam_value(lamp_ref, lam_init):
    lp = lamp_ref[...]
    t1 = jnp.sum(lp[0:1] * lp[1:2], axis=-1, keepdims=True)
    t2 = jnp.sum(lp[2:3] * lp[3:4], axis=-1, keepdims=True)
    return jnp.exp(t1) - jnp.exp(t2) + lam_init


def _diff_attend(items, lam, g, lam_init):
    m = items[0][0].shape[0]
    lane = lax.broadcasted_iota(jnp.int32, (m, HEAD_W), 1)
    ss = []
    for q, k_bf, _ in items:
        q1 = jnp.where(lane < HEAD_DIM, q, 0.0).astype(BF16)
        q2 = jnp.where(lane < HEAD_DIM, 0.0, q).astype(BF16)
        ss.append(_dot_nt(jnp.concatenate([q1, q2], axis=0), k_bf))
    es = [jnp.exp(s - jnp.max(s, axis=-1, keepdims=True)) for s in ss]
    ps = [e * (1.0 / jnp.sum(e, axis=-1, keepdims=True)) for e in es]
    ws = [(p[:m] - lam * p[m:]).astype(BF16) for p in ps]
    os_ = [_dot(w, v_bf) for w, (_, _, v_bf) in zip(ws, items)]
    outs = []
    for o in os_:
        ms = jnp.mean(o * o, axis=-1, keepdims=True)
        outs.append((o * lax.rsqrt(ms + EPS) * g) * (1.0 - lam_init))
    return outs


CTX_BATCHES = 2


def _attn_ctx_kernel(lamp_ref, g_ref, q_ref, k_ref, v_ref, ag_ref, o_ref, *, lam_init):
    lam = _lam_value(lamp_ref, lam_init)
    where = [(slice(b * SEQ, (b + 1) * SEQ), slice(h * HEAD_W, (h + 1) * HEAD_W))
             for b in range(CTX_BATCHES) for h in range(N_HEADS)]
    items = [(q_ref[r, c] * (HEAD_DIM ** -0.5), k_ref[r, c].astype(BF16), v_ref[r, c].astype(BF16)) for r, c in where]
    for (r, c), o in zip(where, _diff_attend(items, lam, g_ref[...], lam_init)):
        o_ref[r, c] = (o * _silu(ag_ref[r, c])).astype(BF16)


def _attn_ctx(proj, lamp, subln_g, layer, lam_init):
    rows = CTX_BATCHES * SEQ
    slab = lambda s: pl.BlockSpec((None, rows, ATT_W), lambda b, s=s: (s, b, 0))
    return pl.pallas_call(
        functools.partial(_attn_ctx_kernel, lam_init=lam_init),
        grid=(BATCH // CTX_BATCHES,),
        in_specs=[
            _of_layer(layer, (4, HEAD_DIM)),
            _of_layer(layer, (1, HEAD_W)),
            slab(S_Q), slab(S_K), slab(S_V), slab(S_AG),
        ],
        out_specs=pl.BlockSpec((rows, ATT_W), lambda b: (b, 0)),
        out_shape=jax.ShapeDtypeStruct((T_CTX, ATT_W), BF16),
        compiler_params=_params("arbitrary"),
        name="attn_ctx",
    )(lamp, subln_g, proj, proj, proj, proj)


Q_CHUNK = 512
Q_PART = 256
N_QC = DEC_SEQ // Q_CHUNK


def _rope(x, cos, sin):
    lane = lax.broadcasted_iota(jnp.int32, x.shape, 1)
    lower = (lane % (HEAD_DIM // 2)) < (HEAD_DIM // 4)
    nf = HEAD_DIM // 4
    partner = jnp.where(lower, pltpu.roll(x, HEAD_W - nf, 1), pltpu.roll(x, nf, 1))
    return x * cos + partner * sin


def _attn_smp_kernel(lamp_ref, g_ref, cosq_ref, sinq_ref, cos_ref, sin_ref, q_ref, k_ref, v_ref,
                     ck_ref, cv_ref, ag_ref, o_ref, ks_ref, vs_ref, *, lam_init):
    @pl.when(pl.program_id(2) == 0)
    def _():
        head_rows = pl.ds(pl.program_id(1), PAST_LEN, stride=N_HEADS)
        ks_ref[0:PAST_LEN, :] = ck_ref[head_rows, :].astype(BF16)
        vs_ref[0:PAST_LEN, :] = cv_ref[head_rows, :].astype(BF16)
        ks_ref[PAST_LEN:, :] = _rope(k_ref[...], cos_ref[...], sin_ref[...]).astype(BF16)
        vs_ref[PAST_LEN:, :] = v_ref[...].astype(BF16)

    lam = _lam_value(lamp_ref, lam_init)
    q = _rope(q_ref[...], cosq_ref[...], sinq_ref[...]) * (HEAD_DIM ** -0.5)
    k_bf, v_bf = ks_ref[...], vs_ref[...]
    parts = [slice(r, r + Q_PART) for r in range(0, Q_CHUNK, Q_PART)]
    items = [(q[p], k_bf, v_bf) for p in parts]
    for p, o in zip(parts, _diff_attend(items, lam, g_ref[...], lam_init)):
        o_ref[p, :] = (o * _silu(ag_ref[p, :])).astype(BF16)


def _attn_smp(proj, cache_k, cache_v, lamp, subln_g, cos, sin, layer, lam_init):
    full = lambda shape: pl.BlockSpec(shape, lambda b, h, c: (0, 0))
    return pl.pallas_call(
        functools.partial(_attn_smp_kernel, lam_init=lam_init),
        grid=(DEC_BATCH, N_HEADS, N_QC),
        in_specs=[
            _of_layer(layer, (4, HEAD_DIM)),
            _of_layer(layer, (1, HEAD_W)),
            pl.BlockSpec((Q_CHUNK, HEAD_W), lambda b, h, c: (c, 0)),
            pl.BlockSpec((Q_CHUNK, HEAD_W), lambda b, h, c: (c, 0)),
            full((DEC_SEQ, HEAD_W)),
            full((DEC_SEQ, HEAD_W)),
            pl.BlockSpec((None, Q_CHUNK, HEAD_W), lambda b, h, c: (S_Q, b * N_QC + c, h)),
            pl.BlockSpec((None, DEC_SEQ, HEAD_W), lambda b, h, c: (S_K, b, h)),
            pl.BlockSpec((None, DEC_SEQ, HEAD_W), lambda b, h, c: (S_V, b, h)),
            pl.BlockSpec((None, None, PAST_LEN * N_HEADS, HEAD_W), lambda b, h, c: (b, layer, 0, 0)),
            pl.BlockSpec((None, None, PAST_LEN * N_HEADS, HEAD_W), lambda b, h, c: (b, layer, 0, 0)),
            pl.BlockSpec((None, Q_CHUNK, HEAD_W), lambda b, h, c: (S_AG, b * N_QC + c, h)),
        ],
        out_specs=pl.BlockSpec((Q_CHUNK, HEAD_W), lambda b, h, c: (b * N_QC + c, h)),
        out_shape=jax.ShapeDtypeStruct((T_SMP, ATT_W), BF16),
        scratch_shapes=[
            pltpu.VMEM((PAST_LEN + DEC_SEQ, HEAD_W), BF16),
            pltpu.VMEM((PAST_LEN + DEC_SEQ, HEAD_W), BF16),
        ],
        compiler_params=_params("arbitrary", "arbitrary", "arbitrary"),
        name="attn_smp",
    )(lamp, subln_g, cos, sin, cos, sin, proj, proj, proj, cache_k, cache_v, proj)


PAD = 8
GATE_ROWS = 256
LRU_SLABS = LRU_W // LANES


def _lru_kernel(x_ref, lg_ref, cw_ref, cb_ref, wa_ref, wi_ref, ba_ref, bi_ref, lam_ref, h0_ref, o_ref, hl_ref,
                xpad_ref, a_ref, b_ref, h_ref, w_ref, *, L, G):
    @pl.when(pl.program_id(0) == 0)
    def _():
        w_ref[...] = jnp.zeros(w_ref.shape, BF16)
        for d in range(2):
            for gate, src in enumerate((wa_ref, wi_ref)):
                col0 = (2 * d + gate) * LRU_W
                for n in range(LRU_BLOCKS):
                    lo = n * LRU_BW
                    w_ref[lo:lo + LRU_BW, col0 + lo:col0 + lo + LRU_BW] = src[d, n].astype(BF16)

    zeros = jnp.zeros((PAD, LRU_W), F32)
    xpad_ref[0:PAD, :] = zeros
    xpad_ref[PAD + L:, :] = zeros
    nl = -lam_ref[...]
    neg_c = (-LRU_C * LOG2E) * (jnp.maximum(nl, 0.0) + jnp.log1p(jnp.exp(-jnp.abs(nl))))
    cw = cw_ref[...]
    cb = cb_ref[...]
    ba = ba_ref[...]
    bi = bi_ref[...]

    def gates_of_sequence(g, carry):
        xpad_ref[PAD:PAD + L, :] = x_ref[pl.ds(pl.multiple_of(g * L, L), L), :]
        for c in range(L // GATE_ROWS):
            r0 = c * GATE_ROWS
            xc = cb + cw[0:1] * xpad_ref[PAD - 2 + r0:PAD - 2 + r0 + GATE_ROWS, :]
            xc = xc + cw[1:2] * xpad_ref[PAD - 1 + r0:PAD - 1 + r0 + GATE_ROWS, :]
            xc = xc + cw[2:3] * xpad_ref[PAD + r0:PAD + r0 + GATE_ROWS, :]
            xc = xc + cw[3:4] * xpad_ref[PAD + 1 + r0:PAD + 1 + r0 + GATE_ROWS, :]
            gates = _dot(xc.astype(BF16), w_ref[...])
            for d in range(2):
                r = _sigmoid(gates[:, (2 * d) * LRU_W:(2 * d + 1) * LRU_W] + ba[d:d + 1])
                i = _sigmoid(gates[:, (2 * d + 1) * LRU_W:(2 * d + 2) * LRU_W] + bi[d:d + 1])
                a = jnp.exp2(r * neg_c[d:d + 1])
                a2 = 1.0 - a * a
                b = jnp.where(a2 > 0.0, a2 * lax.rsqrt(a2), 0.0) * (i * xc)
                rows = pl.ds(r0 * G + g, GATE_ROWS, stride=G)
                for s in range(LRU_SLABS):
                    a_ref[d * LRU_SLABS + s, rows, :] = a[:, s * LANES:(s + 1) * LANES]
                    b_ref[d * LRU_SLABS + s, rows, :] = b[:, s * LANES:(s + 1) * LANES]
        return carry

    lax.fori_loop(0, G, gates_of_sequence, 0)

    def step(t, hs):
        out = []
        for d in range(2):
            tt = t if d == 0 else L - 1 - t
            rows = pl.ds(pl.multiple_of(tt * G, G), G)
            for s in range(LRU_SLABS):
                k = d * LRU_SLABS + s
                h = a_ref[k, rows, :] * hs[k] + b_ref[k, rows, :]
                h_ref[k, rows, :] = h
                out.append(h)
        return tuple(out)

    h0 = tuple(h0_ref[d, :, s * LANES:(s + 1) * LANES] for d in range(2) for s in range(LRU_SLABS))
    hs = lax.fori_loop(0, L, step, h0, unroll=2)
    for d in range(2):
        for s in range(LRU_SLABS):
            hl_ref[d, :, s * LANES:(s + 1) * LANES] = hs[d * LRU_SLABS + s]

    def write_sequence(g, carry):
        rows = pl.ds(g, L, stride=G)
        orow = pl.ds(pl.multiple_of(g * L, L), L)
        for s in range(LRU_SLABS):
            lanes = slice(s * LANES, (s + 1) * LANES)
            h = h_ref[s, rows, :] + h_ref[LRU_SLABS + s, rows, :]
            o_ref[orow, lanes] = (h * _silu(lg_ref[orow, lanes])).astype(BF16)
        return carry

    lax.fori_loop(0, G, write_sequence, 0)


def _lru(proj, conv_w, conv_b, wa, wi, ba, bi, lam, h0, layer, h0_layer, L, G):
    nseq = h0.shape[2]
    return pl.pallas_call(
        functools.partial(_lru_kernel, L=L, G=G),
        grid=(nseq // G,),
        in_specs=[
            pl.BlockSpec((None, G * L, LRU_W), lambda s: (S_LX_LG, s, 0)),
            pl.BlockSpec((None, G * L, LRU_W), lambda s: (S_LX_LG, s, 1)),
            _of_layer(layer, (4, LRU_W)),
            _of_layer(layer, (1, LRU_W)),
            _of_layer(layer, (2, LRU_BLOCKS, LRU_BW, LRU_BW)),
            _of_layer(layer, (2, LRU_BLOCKS, LRU_BW, LRU_BW)),
            _of_layer(layer, (2, LRU_W)),
            _of_layer(layer, (2, LRU_W)),
            _of_layer(layer, (2, LRU_W)),
            pl.BlockSpec((None, 2, G, LRU_W), lambda s: (h0_layer, 0, s, 0)),
        ],
        out_specs=[
            pl.BlockSpec((G * L, LRU_W), lambda s: (s, 0)),
            pl.BlockSpec((2, G, LRU_W), lambda s: (0, s, 0)),
        ],
        out_shape=[
            jax.ShapeDtypeStruct((nseq * L, LRU_W), BF16),
            jax.ShapeDtypeStruct((2, nseq, LRU_W), F32),
        ],
        scratch_shapes=[
            pltpu.VMEM((L + 2 * PAD, LRU_W), F32),
            pltpu.VMEM((2 * LRU_SLABS, L * G, LANES), F32),
            pltpu.VMEM((2 * LRU_SLABS, L * G, LANES), F32),
            pltpu.VMEM((2 * LRU_SLABS, L * G, LANES), F32),
            pltpu.VMEM((LRU_W, 4 * LRU_W), BF16),
        ],
        compiler_params=_params("arbitrary"),
        name=f"lru_{L}",
    )(proj, proj, conv_w, conv_b, wa, wi, ba, bi, lam, h0)


FILT_ROWS = 256


def _filt_kernel(z_ref, decay_ref, w1_ref, b1_ref, w2_ref, b2_ref, w3_ref, ft_ref, fb_ref, g_ref, taps_ref, *, L):
    i = pl.program_id(0)

    @pl.when(i == 0)
    def _():
        h = jnp.sin(_dot_hi(z_ref[...], w1_ref[...]) + b1_ref[...])
        h = jnp.sin(_dot_hi(h, w2_ref[...]) + b2_ref[...])
        filt = _dot_hi(h, w3_ref[...])
        decay = decay_ref[...]
        row = lax.broadcasted_iota(jnp.int32, (L, HY_W), 0)
        taps_ref[:, :HY_W] = (filt[:, :HY_W] * decay).astype(BF16)
        taps_ref[:, HY_W:] = jnp.where(row == 0, 0.0, filt[:, HY_W:] * decay).astype(BF16)

    taps = taps_ref[...]
    top = _dot(ft_ref[...], taps)
    bot = _dot(fb_ref[...], taps)
    first = (lax.broadcasted_iota(jnp.int32, (FILT_ROWS, HY_W), 0) + i * FILT_ROWS) == 0
    gt = top[:, :HY_W] + top[:, HY_W:]
    g_ref[0] = gt
    g_ref[1] = jnp.where(first, 0.0, bot[:, :HY_W] - bot[:, HY_W:])
    g_ref[2] = jnp.where(first, bot[:, :HY_W] + bot[:, HY_W:], gt)


def _filters(z, decay, w1, b1, w2, b2, w3, fwd_bf, layer, L):
    full = lambda shape: pl.BlockSpec(shape, lambda i: (0,) * len(shape))
    nblk = L // FILT_ROWS
    return pl.pallas_call(
        functools.partial(_filt_kernel, L=L),
        grid=(nblk,),
        in_specs=[
            full((L, HY_POS)), full((L, HY_W)),
            _of_layer(layer, (HY_POS, HY_HIDDEN)), _of_layer(layer, (1, HY_HIDDEN)),
            _of_layer(layer, (HY_HIDDEN, HY_HIDDEN)), _of_layer(layer, (1, HY_HIDDEN)),
            _of_layer(layer, (HY_HIDDEN, 2 * HY_W)),
            pl.BlockSpec((FILT_ROWS, L), lambda i: (i, 0)),
            pl.BlockSpec((FILT_ROWS, L), lambda i: (nblk + i, 0)),
        ],
        out_specs=pl.BlockSpec((3, FILT_ROWS, HY_W), lambda i: (0, i, 0)),
        out_shape=jax.ShapeDtypeStruct((3, L, HY_W), F32),
        scratch_shapes=[pltpu.VMEM((L, 2 * HY_W), BF16)],
        compiler_params=_params("arbitrary"),
        name=f"hyena_filters_{L}",
    )(z, decay, w1, b1, w2, b2, w3, fwd_bf, fwd_bf)


HY_PARAM_ROWS = 16


def _hyena_kernel(hv_ref, hx1_ref, hx0_ref, hg_ref, prm_ref, g_ref, f_ref, fi_ref, o_ref, xpad_ref, *, L, ch, seqs):
    zeros = jnp.zeros((PAD, ch), F32)
    for s in range(seqs):
        xpad_ref[s, 0:PAD, :] = zeros
        xpad_ref[s, PAD + L:, :] = zeros
    prm = prm_ref[...]
    rows = [slice(s * L, (s + 1) * L) for s in range(seqs)]

    def conv(ref, stream, s):
        xpad_ref[s, PAD:PAD + L, :] = ref[rows[s], :]
        u = prm[9 + stream:10 + stream] + prm[3 * stream:3 * stream + 1] * xpad_ref[s, PAD - 1:PAD - 1 + L, :]
        u = u + prm[3 * stream + 1:3 * stream + 2] * xpad_ref[s, PAD:PAD + L, :]
        return u + prm[3 * stream + 2:3 * stream + 3] * xpad_ref[s, PAD + 1:PAD + 1 + L, :]

    zs = [conv(hx1_ref, 1, s) * conv(hv_ref, 0, s) for s in range(seqs)]
    specs = [_dot(f_ref[...], z.astype(BF16)) for z in zs]
    gx = g_ref[1]
    prods = []
    for spec in specs:
        top, bot = spec[:L], spec[L:]
        yt = top * g_ref[0] - bot * gx
        yb = top * gx + bot * g_ref[2]
        prods.append(jnp.concatenate([yt, yb], axis=0).astype(BF16))
    ys = [_dot(fi_ref[...], p) for p in prods]
    for s in range(seqs):
        hy = conv(hx0_ref, 2, s) * (ys[s] + zs[s] * prm[12:13])
        o_ref[rows[s], :] = (hy * _silu(hg_ref[rows[s], :])).astype(BF16)


def _hyena(proj, prm, g, fwd, inv, layer, L, nseq, ch, seqs):
    nch = HY_W // ch
    full = lambda shape: pl.BlockSpec(shape, lambda s, j: (0,) * len(shape))
    col = lambda slab, off: pl.BlockSpec((None, seqs * L, ch), lambda s, j, slab=slab, off=off: (slab, s, off * nch + j))
    return pl.pallas_call(
        functools.partial(_hyena_kernel, L=L, ch=ch, seqs=seqs),
        grid=(nseq // seqs, nch),
        in_specs=[
            col(S_HV_HX1, 0), col(S_HV_HX1, 1), col(S_HX0_HG, 0), col(S_HX0_HG, 1),
            pl.BlockSpec((None, HY_PARAM_ROWS, ch), lambda s, j: (layer, 0, j)),
            pl.BlockSpec((3, L, ch), lambda s, j: (0, 0, j)),
            full((2 * L, L)), full((L, 2 * L)),
        ],
        out_specs=pl.BlockSpec((seqs * L, ch), lambda s, j: (s, j)),
        out_shape=jax.ShapeDtypeStruct((nseq * L, HY_W), BF16),
        scratch_shapes=[pltpu.VMEM((seqs, L + 2 * PAD, ch), F32)],
        compiler_params=_params("arbitrary", "arbitrary"),
        name=f"hyena_{L}",
    )(proj, proj, proj, proj, prm, g, fwd, inv)


def _hyena_params(conv_w, conv_b, d):
    rows = [conv_w[:, :, s * HY_W:(s + 1) * HY_W] for s in range(3)]
    rows += [conv_b[:, None, s * HY_W:(s + 1) * HY_W] for s in range(3)]
    rows += [d[:, None, :], jnp.zeros((DEPTH, HY_PARAM_ROWS - 13, HY_W), F32)]
    return jnp.concatenate(rows, axis=1)


OUT_TM = 512


def _outproj_kernel(x_ref, gate_ref, att_ref, lru_ref, hy_ref, w_ref, fg_ref, o_ref, *, final):
    acc = _dot(att_ref[...], w_ref[0:ATT_W, :])
    acc = acc + _dot(lru_ref[...], w_ref[ATT_W:ATT_W + LRU_W, :])
    acc = acc + _dot(hy_ref[...], w_ref[ATT_W + LRU_W:, :])
    y = x_ref[...] + gate_ref[...] * acc
    if final:
        ms = jnp.mean(y * y, axis=-1, keepdims=True)
        y = (y * lax.rsqrt(ms + EPS)) * fg_ref[...]
    o_ref[...] = y


def _outproj(x, mod4, att, lru, hy, w_out_bf, final_g, layer, final, seq_len):
    n_tok = x.shape[0]
    tm = OUT_TM
    cond = _cond_row(seq_len, tm)
    return pl.pallas_call(
        functools.partial(_outproj_kernel, final=final),
        grid=(n_tok // tm,),
        in_specs=[
            pl.BlockSpec((tm, D_MODEL), lambda i: (i, 0)),
            pl.BlockSpec((None, None, 1, D_MODEL), lambda i: (layer, cond(i), 0, 2)),
            pl.BlockSpec((tm, ATT_W), lambda i: (i, 0)),
            pl.BlockSpec((tm, LRU_W), lambda i: (i, 0)),
            pl.BlockSpec((tm, HY_W), lambda i: (i, 0)),
            pl.BlockSpec((None, D_MODEL, D_MODEL), lambda i: (layer, 0, 0), pipeline_mode=pl.Buffered(1)),
            pl.BlockSpec((1, D_MODEL), lambda i: (0, 0)),
        ],
        out_specs=pl.BlockSpec((tm, D_MODEL), lambda i: (i, 0)),
        out_shape=jax.ShapeDtypeStruct((n_tok, D_MODEL), F32),
        compiler_params=_params("arbitrary"),
        name="outproj",
    )(x, mod4, att, lru, hy, w_out_bf, final_g)


def kernel(x_prompt, x_sample, cache_k, cache_v, state_lru, c, c_ctx, norm_g, w_ada, b_ada, w_in, w_out, lam_q1, lam_k1, lam_q2, lam_k2, attn_subln_g, lru_conv_w, lru_conv_b, lru_wa, lru_ba, lru_wi, lru_bi, lru_lam, hy_conv_w, hy_conv_b, hy_w1, hy_b1, hy_w2, hy_b2, hy_w3, hy_d, final_g):
    cos, sin = _rope_tables()
    tables = {}
    for L in (SEQ, DEC_SEQ):
        fwd, inv = _dft_tables(L)
        z, decay = _hyena_tables(L)
        tables[L] = (fwd.astype(BF16), inv.astype(BF16), z, decay)

    cond = jnp.concatenate([c_ctx[None, :], c, jnp.zeros((N_COND - 1 - DEC_BATCH, D_MODEL), F32)], axis=0)
    mod4 = _ada(cond, w_ada, b_ada).reshape(DEPTH, N_COND, 1, 3 * D_MODEL)

    xc = x_prompt.reshape(T_CTX, D_MODEL)
    xs = x_sample.reshape(T_SMP, D_MODEL)
    ck = cache_k.reshape(DEC_BATCH, DEPTH, PAST_LEN * N_HEADS, HEAD_W)
    cv = cache_v.reshape(DEC_BATCH, DEPTH, PAST_LEN * N_HEADS, HEAD_W)
    h0_ctx = jnp.zeros((1, 2, BATCH, LRU_W), F32)
    h0_smp = jnp.transpose(state_lru, (1, 2, 0, 3))
    norm_g3 = norm_g[:, None, :]
    lamp = jnp.stack([lam_q1, lam_k1, lam_q2, lam_k2], axis=1)
    subln = attn_subln_g[:, None, :]
    w_out_bf = w_out.astype(BF16)
    lru_args = (lru_conv_w, lru_conv_b[:, None, :], lru_wa, lru_wi, lru_ba, lru_bi, lru_lam)
    hy_prm = _hyena_params(hy_conv_w, hy_conv_b, hy_d)
    filt_args = (hy_w1, hy_b1[:, None, :], hy_w2, hy_b2[:, None, :], hy_w3)
    fg = final_g[None, :]

    assert DEPTH == 2
    caches, hs = None, []
    for l in range(DEPTH):
        final = l == DEPTH - 1
        lam_init = 0.8 - 0.6 * math.exp(-0.3 * l)

        fwd, inv, z, decay = tables[DEC_SEQ]
        proj, w_in_bf = _inproj_w32(xs, norm_g3, mod4, w_in, l, DEC_SEQ)
        att = _attn_smp(proj, ck, cv, lamp, subln, cos, sin, l, lam_init)
        lru, _ = _lru(proj, *lru_args, h0_smp, l, l, DEC_SEQ, DEC_BATCH)
        g = _filters(z, decay, *filt_args, fwd, l, DEC_SEQ)
        hy = _hyena(proj, hy_prm, g, fwd, inv, l, DEC_SEQ, DEC_BATCH, HY_W // 2, 2)
        xs = _outproj(xs, mod4, att, lru, hy, w_out_bf, fg, l, final, DEC_SEQ)

        fwd, inv, z, decay = tables[SEQ]
        proj, *caches = _inproj_cache(xc, norm_g3, mod4, w_in_bf, l, caches)
        att = _attn_ctx(proj, lamp, subln, l, lam_init)
        lru, h_last = _lru(proj, *lru_args, h0_ctx, l, 0, SEQ, 8)
        hs.append(h_last)
        g = _filters(z, decay, *filt_args, fwd, l, SEQ)
        hy = _hyena(proj, hy_prm, g, fwd, inv, l, SEQ, BATCH, HY_W, 2)
        xc = _outproj(xc, mod4, att, lru, hy, w_out_bf, fg, l, final, None)

    y_prompt = xc.reshape(BATCH, SEQ, D_MODEL)
    y_sample = xs.reshape(DEC_BATCH, DEC_SEQ, D_MODEL)
    new_k, new_v = (a.reshape(BATCH, DEPTH, SEQ, N_HEADS, HEAD_W) for a in caches)
    new_state = jnp.transpose(jnp.stack(hs, axis=0), (2, 0, 1, 3))
    return (y_prompt, y_sample, new_k, new_v, new_state)
```

```python
import functools
import math

import numpy as np
import jax
import jax.numpy as jnp
from jax import lax
from jax.experimental import pallas as pl
from jax.experimental.pallas import tpu as pltpu

D_MODEL = 2048
BATCH = 32
SEQ = 256
DEPTH = 2
DEC_BATCH = 2
DEC_SEQ = 1024
PAST_LEN = 512
GRID_W = 64
ATT_W = 1024
LRU_W = 512
HY_W = 512
HEAD_DIM = 64
N_HEADS = 8
HEAD_W = 2 * HEAD_DIM
LRU_BLOCKS = 8
LRU_BW = LRU_W // LRU_BLOCKS
LRU_C = 8.0
HY_BANDS = 16
HY_POS = 1 + 2 * HY_BANDS
HY_HIDDEN = 64
HY_DECAY_FAST = 0.3
HY_DECAY_SLOW = 1.5
HY_DECAY_TARGET = 1e-2
ROPE_BASE = 10000.0
EPS = 1e-6
IN_W = 4 * ATT_W + 2 * LRU_W + 4 * HY_W

T_CTX = BATCH * SEQ
T_SMP = DEC_BATCH * DEC_SEQ
N_COND = 8

SLAB_W = 1024
N_SLABS = IN_W // SLAB_W
S_Q, S_K, S_V, S_AG, S_LX_LG, S_HV_HX1, S_HX0_HG = range(N_SLABS)

LANES = 128
F32 = jnp.float32
BF16 = jnp.bfloat16
VMEM_LIMIT = 58 * 1024 * 1024


LOG2E = 1.4426950408889634


def _sigmoid(x):
    return 1.0 / (1.0 + jnp.exp2(x * (-LOG2E)))


def _silu(x):
    return x * _sigmoid(x)


def _dot(a, b):
    return jnp.dot(a, b, preferred_element_type=F32)


def _dot_nt(a, b):
    return lax.dot_general(a, b, (((1,), (1,)), ((), ())), preferred_element_type=F32)


def _dot_hi(a, b):
    return jnp.dot(a, b, precision=lax.Precision.HIGHEST, preferred_element_type=F32)


def _params(*sem):
    return pltpu.CompilerParams(dimension_semantics=sem, vmem_limit_bytes=VMEM_LIMIT)


def _rope_tables():
    t = np.arange(DEC_SEQ)
    pos = np.stack([t // GRID_W, t % GRID_W], axis=1).astype(np.float64)
    nf = HEAD_DIM // 4
    inv = ROPE_BASE ** (-np.arange(nf, dtype=np.float64) / nf)
    lane = np.arange(HEAD_W)
    j = lane % HEAD_DIM
    axis = j // (HEAD_DIM // 2)
    f = j % nf
    upper = (j % (HEAD_DIM // 2)) >= nf
    ang = pos[:, axis] * inv[f][None, :]
    cos = np.cos(ang)
    sin = np.sin(ang) * np.where(upper, 1.0, -1.0)[None, :]
    return jnp.asarray(cos, F32), jnp.asarray(sin, F32)


def _dft_tables(L):
    n = 2 * L
    k = np.arange(L)[:, None]
    t = np.arange(L)[None, :]
    ang = 2.0 * np.pi * ((k * t) % n).astype(np.float64) / n
    fwd = np.concatenate([np.cos(ang), -np.sin(ang)], axis=0)
    fwd[L, :] = (-1.0) ** np.arange(L)
    wk = np.where(np.arange(L) == 0, 1.0, 2.0)[None, :]
    ang_t = ang.T
    inv = np.concatenate([wk * np.cos(ang_t), -2.0 * np.sin(ang_t)], axis=1) / n
    inv[:, L] = ((-1.0) ** np.arange(L)) / n
    return jnp.asarray(fwd, F32), jnp.asarray(inv, F32)


def _hyena_tables(L):
    pos = np.arange(L, dtype=np.float64)
    t = pos / float(max(L - 1, 1))
    bands = np.linspace(1e-4, HY_BANDS - 1, HY_BANDS)
    ang = (2.0 * math.pi / L) * pos[:, None] * bands[None, :]
    z = np.concatenate([t[:, None], np.cos(ang), np.sin(ang)], axis=-1)
    lo = abs(math.log(HY_DECAY_TARGET) / HY_DECAY_SLOW)
    hi = abs(math.log(HY_DECAY_TARGET) / HY_DECAY_FAST)
    deltas = np.linspace(lo, hi, HY_W)
    decay = np.exp(-t[:, None] * deltas[None, :])
    return jnp.asarray(z, F32), jnp.asarray(decay, F32)


def _ada_kernel(c_ref, w_ref, b_ref, o_ref):
    s = _silu(c_ref[...])
    o_ref[...] = _dot(s.astype(BF16), w_ref[...].astype(BF16)) + b_ref[...]


def _ada(cond, w_ada, b_ada):
    tn = 1024
    return pl.pallas_call(
        _ada_kernel,
        grid=(DEPTH, 3 * D_MODEL // tn),
        in_specs=[
            pl.BlockSpec((N_COND, D_MODEL), lambda l, j: (0, 0)),
            pl.BlockSpec((None, D_MODEL, tn), lambda l, j: (l, 0, j)),
            pl.BlockSpec((None, 1, tn), lambda l, j: (l, 0, j)),
        ],
        out_specs=pl.BlockSpec((None, N_COND, tn), lambda l, j: (l, 0, j)),
        out_shape=jax.ShapeDtypeStruct((DEPTH, N_COND, 3 * D_MODEL), F32),
        compiler_params=_params("arbitrary", "arbitrary"),
        name="ada",
    )(cond, w_ada, b_ada.reshape(DEPTH, 1, 3 * D_MODEL))


def _cond_row(seq_len, tm):
    if seq_len is None:
        return lambda i: 0
    return lambda i: 1 + i // (seq_len // tm)


IN_TM = 1024
NORM_ROWS = 32


def _norm_modulate(x_ref, g_ref, shift_ref, scale_ref, h_ref, h_row0=0):
    g = g_ref[...]
    sc = 1.0 + scale_ref[...]
    sh = shift_ref[...]

    def body(r, carry):
        start = pl.multiple_of(r * NORM_ROWS, NORM_ROWS)
        x = x_ref[pl.ds(start, NORM_ROWS), :]
        ms = jnp.mean(x * x, axis=-1, keepdims=True)
        xn = x * lax.rsqrt(ms + EPS)
        h_ref[pl.ds(pl.multiple_of(h_row0 + start, NORM_ROWS), NORM_ROWS), :] = ((xn * g) * sc + sh).astype(BF16)
        return carry

    lax.fori_loop(0, IN_TM // NORM_ROWS, body, 0, unroll=4)


def _inproj_cache_kernel(*refs, first_layer):
    if first_layer:
        x_ref, g_ref, shift_ref, scale_ref, w_ref, o_ref, kc_ref, vc_ref, h_ref = refs
    else:
        x_ref, g_ref, shift_ref, scale_ref, w_ref, _, _, o_ref, kc_ref, vc_ref, h_ref = refs
    j = pl.program_id(1)

    @pl.when(j == 0)
    def _():
        _norm_modulate(x_ref, g_ref, shift_ref, scale_ref, h_ref)

    o_ref[...] = _dot(h_ref[...], w_ref[...])

    def scatter(dst_ref):
        for b in range(IN_TM // SEQ):
            for h in range(N_HEADS):
                dst_ref[b, pl.ds(h, SEQ, stride=N_HEADS), :] = o_ref[b * SEQ:(b + 1) * SEQ, h * HEAD_W:(h + 1) * HEAD_W]

    @pl.when(j == S_K)
    def _():
        scatter(kc_ref)

    @pl.when(j == S_V)
    def _():
        scatter(vc_ref)

    if first_layer:
        @pl.when(j == S_K + 1)
        def _():
            kc_ref[...] = jnp.zeros(kc_ref.shape, F32)

        @pl.when(j == S_V + 1)
        def _():
            vc_ref[...] = jnp.zeros(vc_ref.shape, F32)


def _of_layer(layer, shape):
    return pl.BlockSpec((None,) + tuple(shape), lambda *_: (layer,) + (0,) * len(shape))


def _inproj_specs(tm, layer, cond, w_spec):
    return [
        pl.BlockSpec((tm, D_MODEL), lambda i, j: (i, 0)),
        _of_layer(layer, (1, D_MODEL)),
        pl.BlockSpec((None, None, 1, D_MODEL), lambda i, j: (layer, cond(i), 0, 0)),
        pl.BlockSpec((None, None, 1, D_MODEL), lambda i, j: (layer, cond(i), 0, 1)),
        w_spec,
    ]


W32_K_CHUNK = 512


def _inproj_w32_kernel(x_ref, g_ref, shift_ref, scale_ref, w_ref, o_ref, wbf_ref, h_ref):
    row0 = pl.multiple_of(pl.program_id(1) * IN_TM, IN_TM)

    @pl.when(pl.program_id(0) == 0)
    def _():
        _norm_modulate(x_ref, g_ref, shift_ref, scale_ref, h_ref, row0)

    for k0 in range(0, D_MODEL, W32_K_CHUNK):
        w = w_ref[k0:k0 + W32_K_CHUNK, :].astype(BF16)
        wbf_ref[k0:k0 + W32_K_CHUNK, :] = w
        part = _dot(h_ref[pl.ds(row0, IN_TM), k0:k0 + W32_K_CHUNK], w)
        if k0 == 0:
            o_ref[...] = part
        else:
            o_ref[...] += part


def _inproj_w32(x, norm_g, mod4, w_in, layer, seq_len):
    n_tok = x.shape[0]
    tm = IN_TM
    n_tiles = n_tok // tm
    cond = _cond_row(seq_len, tm)
    x_row = lambda j, i: jnp.where(j == 0, i, n_tiles - 1)
    return pl.pallas_call(
        _inproj_w32_kernel,
        grid=(N_SLABS, n_tiles),
        in_specs=[
            pl.BlockSpec((tm, D_MODEL), lambda j, i: (x_row(j, i), 0), pipeline_mode=pl.Buffered(1)),
            _of_layer(layer, (1, D_MODEL)),
            pl.BlockSpec((None, None, 1, D_MODEL), lambda j, i: (layer, cond(i), 0, 0)),
            pl.BlockSpec((None, None, 1, D_MODEL), lambda j, i: (layer, cond(i), 0, 1)),
            pl.BlockSpec((None, D_MODEL, SLAB_W), lambda j, i: (layer, 0, j)),
        ],
        out_specs=[
            pl.BlockSpec((None, tm, SLAB_W), lambda j, i: (j, i, 0)),
            pl.BlockSpec((D_MODEL, SLAB_W), lambda j, i: (0, j)),
        ],
        out_shape=[jax.ShapeDtypeStruct((N_SLABS, n_tok, SLAB_W), F32), jax.ShapeDtypeStruct((D_MODEL, IN_W), BF16)],
        scratch_shapes=[pltpu.VMEM((n_tok, D_MODEL), BF16)],
        compiler_params=_params("arbitrary", "arbitrary"),
        name="inproj_w32",
    )(x, norm_g, mod4, mod4, w_in)


def _inproj_cache(x, norm_g, mod4, w_in_bf, layer, caches):
    tm = IN_TM
    nb = tm // SEQ
    first = caches is None
    cache_shape = jax.ShapeDtypeStruct((BATCH, DEPTH, SEQ * N_HEADS, HEAD_W), F32)
    if first:
        half = lambda slab: (lambda i, j: (i, jnp.where(j <= slab, layer, layer + 1), 0, 0))
    else:
        half = lambda slab: (lambda i, j: (i, layer, 0, 0))
    cache_spec = lambda slab: pl.BlockSpec((nb, None, SEQ * N_HEADS, HEAD_W), half(slab))
    w_spec = pl.BlockSpec((D_MODEL, SLAB_W), lambda i, j: (0, j))
    in_specs = _inproj_specs(tm, layer, _cond_row(None, tm), w_spec)
    args = (x, norm_g, mod4, mod4, w_in_bf)
    aliases = {}
    if not first:
        in_specs += [pl.BlockSpec(memory_space=pl.ANY)] * 2
        aliases = {len(args): 1, len(args) + 1: 2}
        args += tuple(caches)
    return pl.pallas_call(
        functools.partial(_inproj_cache_kernel, first_layer=first),
        grid=(T_CTX // tm, N_SLABS),
        in_specs=in_specs,
        out_specs=[pl.BlockSpec((None, tm, SLAB_W), lambda i, j: (j, i, 0)), cache_spec(S_K), cache_spec(S_V)],
        out_shape=[jax.ShapeDtypeStruct((N_SLABS, T_CTX, SLAB_W), F32), cache_shape, cache_shape],
        scratch_shapes=[pltpu.VMEM((tm, D_MODEL), BF16)],
        input_output_aliases=aliases,
        compiler_params=_params("arbitrary", "arbitrary"),
        name="inproj_cache",
    )(*args)


def _lam_value(lamp_ref, lam_init):
    lp = lamp_ref[...]
    t1 = jnp.sum(lp[0:1] * lp[1:2], axis=-1, keepdims=True)
    t2 = jnp.sum(lp[2:3] * lp[3:4], axis=-1, keepdims=True)
    return jnp.exp(t1) - jnp.exp(t2) + lam_init


def _diff_attend(items, lam, g, lam_init):
    m = items[0][0].shape[0]
    lane = lax.broadcasted_iota(jnp.int32, (m, HEAD_W), 1)
    ss = []
    for q, k_bf, _ in items:
        q1 = jnp.where(lane < HEAD_DIM, q, 0.0).astype(BF16)
        q2 = jnp.where(lane < HEAD_DIM, 0.0, q).astype(BF16)
        ss.append(_dot_nt(jnp.concatenate([q1, q2], axis=0), k_bf))
    es = [jnp.exp(s - jnp.max(s, axis=-1, keepdims=True)) for s in ss]
    ps = [e * (1.0 / jnp.sum(e, axis=-1, keepdims=True)) for e in es]
    ws = [(p[:m] - lam * p[m:]).astype(BF16) for p in ps]
    os_ = [_dot(w, v_bf) for w, (_, _, v_bf) in zip(ws, items)]
    outs = []
    for o in os_:
        ms = jnp.mean(o * o, axis=-1, keepdims=True)
        outs.append((o * lax.rsqrt(ms + EPS) * g) * (1.0 - lam_init))
    return outs


CTX_BATCHES = 2


def _attn_ctx_kernel(lamp_ref, g_ref, q_ref, k_ref, v_ref, ag_ref, o_ref, *, lam_init):
    lam = _lam_value(lamp_ref, lam_init)
    where = [(slice(b * SEQ, (b + 1) * SEQ), slice(h * HEAD_W, (h + 1) * HEAD_W))
             for b in range(CTX_BATCHES) for h in range(N_HEADS)]
    items = [(q_ref[r, c] * (HEAD_DIM ** -0.5), k_ref[r, c].astype(BF16), v_ref[r, c].astype(BF16)) for r, c in where]
    for (r, c), o in zip(where, _diff_attend(items, lam, g_ref[...], lam_init)):
        o_ref[r, c] = (o * _silu(ag_ref[r, c])).astype(BF16)


def _attn_ctx(proj, lamp, subln_g, layer, lam_init):
    rows = CTX_BATCHES * SEQ
    slab = lambda s: pl.BlockSpec((None, rows, ATT_W), lambda b, s=s: (s, b, 0))
    return pl.pallas_call(
        functools.partial(_attn_ctx_kernel, lam_init=lam_init),
        grid=(BATCH // CTX_BATCHES,),
        in_specs=[
            _of_layer(layer, (4, HEAD_DIM)),
            _of_layer(layer, (1, HEAD_W)),
            slab(S_Q), slab(S_K), slab(S_V), slab(S_AG),
        ],
        out_specs=pl.BlockSpec((rows, ATT_W), lambda b: (b, 0)),
        out_shape=jax.ShapeDtypeStruct((T_CTX, ATT_W), BF16),
        compiler_params=_params("arbitrary"),
        name="attn_ctx",
    )(lamp, subln_g, proj, proj, proj, proj)


Q_CHUNK = 512
Q_PART = 256
N_QC = DEC_SEQ // Q_CHUNK


def _rope(x, cos, sin):
    lane = lax.broadcasted_iota(jnp.int32, x.shape, 1)
    lower = (lane % (HEAD_DIM // 2)) < (HEAD_DIM // 4)
    nf = HEAD_DIM // 4
    partner = jnp.where(lower, pltpu.roll(x, HEAD_W - nf, 1), pltpu.roll(x, nf, 1))
    return x * cos + partner * sin


def _attn_smp_kernel(lamp_ref, g_ref, cosq_ref, sinq_ref, cos_ref, sin_ref, q_ref, k_ref, v_ref,
                     ck_ref, cv_ref, ag_ref, o_ref, ks_ref, vs_ref, *, lam_init):
    @pl.when(pl.program_id(2) == 0)
    def _():
        head_rows = pl.ds(pl.program_id(1), PAST_LEN, stride=N_HEADS)
        ks_ref[0:PAST_LEN, :] = ck_ref[head_rows, :].astype(BF16)
        vs_ref[0:PAST_LEN, :] = cv_ref[head_rows, :].astype(BF16)
        ks_ref[PAST_LEN:, :] = _rope(k_ref[...], cos_ref[...], sin_ref[...]).astype(BF16)
        vs_ref[PAST_LEN:, :] = v_ref[...].astype(BF16)

    lam = _lam_value(lamp_ref, lam_init)
    q = _rope(q_ref[...], cosq_ref[...], sinq_ref[...]) * (HEAD_DIM ** -0.5)
    k_bf, v_bf = ks_ref[...], vs_ref[...]
    parts = [slice(r, r + Q_PART) for r in range(0, Q_CHUNK, Q_PART)]
    items = [(q[p], k_bf, v_bf) for p in parts]
    for p, o in zip(parts, _diff_attend(items, lam, g_ref[...], lam_init)):
        o_ref[p, :] = (o * _silu(ag_ref[p, :])).astype(BF16)


def _attn_smp(proj, cache_k, cache_v, lamp, subln_g, cos, sin, layer, lam_init):
    full = lambda shape: pl.BlockSpec(shape, lambda b, h, c: (0, 0))
    return pl.pallas_call(
        functools.partial(_attn_smp_kernel, lam_init=lam_init),
        grid=(DEC_BATCH, N_HEADS, N_QC),
        in_specs=[
            _of_layer(layer, (4, HEAD_DIM)),
            _of_layer(layer, (1, HEAD_W)),
            pl.BlockSpec((Q_CHUNK, HEAD_W), lambda b, h, c: (c, 0)),
            pl.BlockSpec((Q_CHUNK, HEAD_W), lambda b, h, c: (c, 0)),
            full((DEC_SEQ, HEAD_W)),
            full((DEC_SEQ, HEAD_W)),
            pl.BlockSpec((None, Q_CHUNK, HEAD_W), lambda b, h, c: (S_Q, b * N_QC + c, h)),
            pl.BlockSpec((None, DEC_SEQ, HEAD_W), lambda b, h, c: (S_K, b, h)),
            pl.BlockSpec((None, DEC_SEQ, HEAD_W), lambda b, h, c: (S_V, b, h)),
            pl.BlockSpec((None, None, PAST_LEN * N_HEADS, HEAD_W), lambda b, h, c: (b, layer, 0, 0)),
            pl.BlockSpec((None, None, PAST_LEN * N_HEADS, HEAD_W), lambda b, h, c: (b, layer, 0, 0)),
            pl.BlockSpec((None, Q_CHUNK, HEAD_W), lambda b, h, c: (S_AG, b * N_QC + c, h)),
        ],
        out_specs=pl.BlockSpec((Q_CHUNK, HEAD_W), lambda b, h, c: (b * N_QC + c, h)),
        out_shape=jax.ShapeDtypeStruct((T_SMP, ATT_W), BF16),
        scratch_shapes=[
            pltpu.VMEM((PAST_LEN + DEC_SEQ, HEAD_W), BF16),
            pltpu.VMEM((PAST_LEN + DEC_SEQ, HEAD_W), BF16),
        ],
        compiler_params=_params("arbitrary", "arbitrary", "arbitrary"),
        name="attn_smp",
    )(lamp, subln_g, cos, sin, cos, sin, proj, proj, proj, cache_k, cache_v, proj)


GATE_ROWS = 256
LRU_SLABS = LRU_W // LANES


def _lru_kernel(x_ref, lg_ref, cw_ref, cb_ref, wa_ref, wi_ref, ba_ref, bi_ref, lam_ref, h0_ref, o_ref, hl_ref,
                xc_ref, a_ref, b_ref, h_ref, w_ref, *, L, G):
    @pl.when(pl.program_id(0) == 0)
    def _():
        w_ref[...] = jnp.zeros(w_ref.shape, BF16)
        for d in range(2):
            for gate, src in enumerate((wa_ref, wi_ref)):
                col0 = (2 * d + gate) * LRU_W
                for n in range(LRU_BLOCKS):
                    lo = n * LRU_BW
                    w_ref[lo:lo + LRU_BW, col0 + lo:col0 + lo + LRU_BW] = src[d, n].astype(BF16)

    nl = -lam_ref[...]
    neg_c = (-LRU_C * LOG2E) * (jnp.maximum(nl, 0.0) + jnp.log1p(jnp.exp(-jnp.abs(nl))))
    cw = cw_ref[...]
    cb = cb_ref[...]
    ba = ba_ref[...]
    bi = bi_ref[...]

    t = lax.broadcasted_iota(jnp.int32, (L, LRU_W), 0)

    def gates_of_sequence(g, carry):
        x = x_ref[pl.ds(pl.multiple_of(g * L, L), L), :]
        xc = cb + cw[0:1] * jnp.where(t < 2, 0.0, pltpu.roll(x, 2, 0))
        xc = xc + cw[1:2] * jnp.where(t < 1, 0.0, pltpu.roll(x, 1, 0))
        xc = xc + cw[2:3] * x
        xc_ref[...] = xc + cw[3:4] * jnp.where(t == L - 1, 0.0, pltpu.roll(x, L - 1, 0))
        for c in range(L // GATE_ROWS):
            r0 = c * GATE_ROWS
            xc = xc_ref[r0:r0 + GATE_ROWS, :]
            gates = _dot(xc.astype(BF16), w_ref[...])
            for d in range(2):
                r = _sigmoid(gates[:, (2 * d) * LRU_W:(2 * d + 1) * LRU_W] + ba[d:d + 1])
                i = _sigmoid(gates[:, (2 * d + 1) * LRU_W:(2 * d + 2) * LRU_W] + bi[d:d + 1])
                a = jnp.exp2(r * neg_c[d:d + 1])
                a2 = 1.0 - a * a
                b = jnp.where(a2 > 0.0, a2 * lax.rsqrt(a2), 0.0) * (i * xc)
                rows = pl.ds(r0 * G + g, GATE_ROWS, stride=G)
                for s in range(LRU_SLABS):
                    a_ref[d * LRU_SLABS + s, rows, :] = a[:, s * LANES:(s + 1) * LANES]
                    b_ref[d * LRU_SLABS + s, rows, :] = b[:, s * LANES:(s + 1) * LANES]
        return carry

    lax.fori_loop(0, G, gates_of_sequence, 0)

    def step(t, hs):
        out = []
        for d in range(2):
            tt = t if d == 0 else L - 1 - t
            rows = pl.ds(pl.multiple_of(tt * G, G), G)
            for s in range(LRU_SLABS):
                k = d * LRU_SLABS + s
                h = a_ref[k, rows, :] * hs[k] + b_ref[k, rows, :]
                h_ref[k, rows, :] = h
                out.append(h)
        return tuple(out)

    h0 = tuple(h0_ref[d, :, s * LANES:(s + 1) * LANES] for d in range(2) for s in range(LRU_SLABS))
    hs = lax.fori_loop(0, L, step, h0, unroll=2)
    for d in range(2):
        for s in range(LRU_SLABS):
            hl_ref[d, :, s * LANES:(s + 1) * LANES] = hs[d * LRU_SLABS + s]

    def write_sequence(g, carry):
        rows = pl.ds(g, L, stride=G)
        orow = pl.ds(pl.multiple_of(g * L, L), L)
        for s in range(LRU_SLABS):
            lanes = slice(s * LANES, (s + 1) * LANES)
            h = h_ref[s, rows, :] + h_ref[LRU_SLABS + s, rows, :]
            o_ref[orow, lanes] = (h * _silu(lg_ref[orow, lanes])).astype(BF16)
        return carry

    lax.fori_loop(0, G, write_sequence, 0)


def _lru(proj, conv_w, conv_b, wa, wi, ba, bi, lam, h0, layer, h0_layer, L, G):
    nseq = h0.shape[2]
    return pl.pallas_call(
        functools.partial(_lru_kernel, L=L, G=G),
        grid=(nseq // G,),
        in_specs=[
            pl.BlockSpec((None, G * L, LRU_W), lambda s: (S_LX_LG, s, 0)),
            pl.BlockSpec((None, G * L, LRU_W), lambda s: (S_LX_LG, s, 1)),
            _of_layer(layer, (4, LRU_W)),
            _of_layer(layer, (1, LRU_W)),
            _of_layer(layer, (2, LRU_BLOCKS, LRU_BW, LRU_BW)),
            _of_layer(layer, (2, LRU_BLOCKS, LRU_BW, LRU_BW)),
            _of_layer(layer, (2, LRU_W)),
            _of_layer(layer, (2, LRU_W)),
            _of_layer(layer, (2, LRU_W)),
            pl.BlockSpec((None, 2, G, LRU_W), lambda s: (h0_layer, 0, s, 0)),
        ],
        out_specs=[
            pl.BlockSpec((G * L, LRU_W), lambda s: (s, 0)),
            pl.BlockSpec((2, G, LRU_W), lambda s: (0, s, 0)),
        ],
        out_shape=[
            jax.ShapeDtypeStruct((nseq * L, LRU_W), BF16),
            jax.ShapeDtypeStruct((2, nseq, LRU_W), F32),
        ],
        scratch_shapes=[
            pltpu.VMEM((L, LRU_W), F32),
            pltpu.VMEM((2 * LRU_SLABS, L * G, LANES), F32),
            pltpu.VMEM((2 * LRU_SLABS, L * G, LANES), F32),
            pltpu.VMEM((2 * LRU_SLABS, L * G, LANES), F32),
            pltpu.VMEM((LRU_W, 4 * LRU_W), BF16),
        ],
        compiler_params=_params("arbitrary"),
        name=f"lru_{L}",
    )(proj, proj, conv_w, conv_b, wa, wi, ba, bi, lam, h0)


FILT_ROWS = 256


def _filt_kernel(z_ref, decay_ref, w1_ref, b1_ref, w2_ref, b2_ref, w3_ref, ft_ref, fb_ref, g_ref, taps_ref, *, L):
    i = pl.program_id(0)

    @pl.when(i == 0)
    def _():
        h = jnp.sin(_dot_hi(z_ref[...], w1_ref[...]) + b1_ref[...])
        h = jnp.sin(_dot_hi(h, w2_ref[...]) + b2_ref[...])
        filt = _dot_hi(h, w3_ref[...])
        decay = decay_ref[...]
        row = lax.broadcasted_iota(jnp.int32, (L, HY_W), 0)
        taps_ref[:, :HY_W] = (filt[:, :HY_W] * decay).astype(BF16)
        taps_ref[:, HY_W:] = jnp.where(row == 0, 0.0, filt[:, HY_W:] * decay).astype(BF16)

    taps = taps_ref[...]
    top = _dot(ft_ref[...], taps)
    bot = _dot(fb_ref[...], taps)
    first = (lax.broadcasted_iota(jnp.int32, (FILT_ROWS, HY_W), 0) + i * FILT_ROWS) == 0
    gt = top[:, :HY_W] + top[:, HY_W:]
    g_ref[0] = gt
    g_ref[1] = jnp.where(first, 0.0, bot[:, :HY_W] - bot[:, HY_W:])
    g_ref[2] = jnp.where(first, bot[:, :HY_W] + bot[:, HY_W:], gt)


def _filters(z, decay, w1, b1, w2, b2, w3, fwd_bf, layer, L):
    full = lambda shape: pl.BlockSpec(shape, lambda i: (0,) * len(shape))
    nblk = L // FILT_ROWS
    return pl.pallas_call(
        functools.partial(_filt_kernel, L=L),
        grid=(nblk,),
        in_specs=[
            full((L, HY_POS)), full((L, HY_W)),
            _of_layer(layer, (HY_POS, HY_HIDDEN)), _of_layer(layer, (1, HY_HIDDEN)),
            _of_layer(layer, (HY_HIDDEN, HY_HIDDEN)), _of_layer(layer, (1, HY_HIDDEN)),
            _of_layer(layer, (HY_HIDDEN, 2 * HY_W)),
            pl.BlockSpec((FILT_ROWS, L), lambda i: (i, 0)),
            pl.BlockSpec((FILT_ROWS, L), lambda i: (nblk + i, 0)),
        ],
        out_specs=pl.BlockSpec((3, FILT_ROWS, HY_W), lambda i: (0, i, 0)),
        out_shape=jax.ShapeDtypeStruct((3, L, HY_W), F32),
        scratch_shapes=[pltpu.VMEM((L, 2 * HY_W), BF16)],
        compiler_params=_params("arbitrary"),
        name=f"hyena_filters_{L}",
    )(z, decay, w1, b1, w2, b2, w3, fwd_bf, fwd_bf)


HY_PARAM_ROWS = 16


def _hyena_kernel(hv_ref, hx1_ref, hx0_ref, hg_ref, prm_ref, g_ref, f_ref, fi_ref, o_ref, *, L, ch, seqs):
    prm = prm_ref[...]
    rows = [slice(s * L, (s + 1) * L) for s in range(seqs)]

    t = lax.broadcasted_iota(jnp.int32, (L, ch), 0)

    def conv(ref, stream, s):
        x = ref[rows[s], :]
        prev = jnp.where(t == 0, 0.0, pltpu.roll(x, 1, 0))
        nxt = jnp.where(t == L - 1, 0.0, pltpu.roll(x, L - 1, 0))
        u = prm[9 + stream:10 + stream] + prm[3 * stream:3 * stream + 1] * prev
        u = u + prm[3 * stream + 1:3 * stream + 2] * x
        return u + prm[3 * stream + 2:3 * stream + 3] * nxt

    zs = [conv(hx1_ref, 1, s) * conv(hv_ref, 0, s) for s in range(seqs)]
    specs = [_dot(f_ref[...], z.astype(BF16)) for z in zs]
    gx = g_ref[1]
    prods = []
    for spec in specs:
        top, bot = spec[:L], spec[L:]
        yt = top * g_ref[0] - bot * gx
        yb = top * gx + bot * g_ref[2]
        prods.append(jnp.concatenate([yt, yb], axis=0).astype(BF16))
    ys = [_dot(fi_ref[...], p) for p in prods]
    for s in range(seqs):
        hy = conv(hx0_ref, 2, s) * (ys[s] + zs[s] * prm[12:13])
        o_ref[rows[s], :] = (hy * _silu(hg_ref[rows[s], :])).astype(BF16)


def _hyena(proj, prm, g, fwd, inv, layer, L, nseq, ch, seqs):
    nch = HY_W // ch
    full = lambda shape: pl.BlockSpec(shape, lambda s, j: (0,) * len(shape))
    col = lambda slab, off: pl.BlockSpec((None, seqs * L, ch), lambda s, j, slab=slab, off=off: (slab, s, off * nch + j))
    return pl.pallas_call(
        functools.partial(_hyena_kernel, L=L, ch=ch, seqs=seqs),
        grid=(nseq // seqs, nch),
        in_specs=[
            col(S_HV_HX1, 0), col(S_HV_HX1, 1), col(S_HX0_HG, 0), col(S_HX0_HG, 1),
            pl.BlockSpec((None, HY_PARAM_ROWS, ch), lambda s, j: (layer, 0, j)),
            pl.BlockSpec((3, L, ch), lambda s, j: (0, 0, j)),
            full((2 * L, L)), full((L, 2 * L)),
        ],
        out_specs=pl.BlockSpec((seqs * L, ch), lambda s, j: (s, j)),
        out_shape=jax.ShapeDtypeStruct((nseq * L, HY_W), BF16),
        compiler_params=_params("arbitrary", "arbitrary"),
        name=f"hyena_{L}",
    )(proj, proj, proj, proj, prm, g, fwd, inv)


def _hyena_params(conv_w, conv_b, d):
    rows = [conv_w[:, :, s * HY_W:(s + 1) * HY_W] for s in range(3)]
    rows += [conv_b[:, None, s * HY_W:(s + 1) * HY_W] for s in range(3)]
    rows += [d[:, None, :], jnp.zeros((DEPTH, HY_PARAM_ROWS - 13, HY_W), F32)]
    return jnp.concatenate(rows, axis=1)


OUT_TM = 512


def _outproj_kernel(x_ref, gate_ref, att_ref, lru_ref, hy_ref, w_ref, fg_ref, o_ref, *, final):
    acc = _dot(att_ref[...], w_ref[0:ATT_W, :])
    acc = acc + _dot(lru_ref[...], w_ref[ATT_W:ATT_W + LRU_W, :])
    acc = acc + _dot(hy_ref[...], w_ref[ATT_W + LRU_W:, :])
    y = x_ref[...] + gate_ref[...] * acc
    if final:
        ms = jnp.mean(y * y, axis=-1, keepdims=True)
        y = (y * lax.rsqrt(ms + EPS)) * fg_ref[...]
    o_ref[...] = y


def _outproj(x, mod4, att, lru, hy, w_out_bf, final_g, layer, final, seq_len):
    n_tok = x.shape[0]
    tm = OUT_TM
    cond = _cond_row(seq_len, tm)
    return pl.pallas_call(
        functools.partial(_outproj_kernel, final=final),
        grid=(n_tok // tm,),
        in_specs=[
            pl.BlockSpec((tm, D_MODEL), lambda i: (i, 0)),
            pl.BlockSpec((None, None, 1, D_MODEL), lambda i: (layer, cond(i), 0, 2)),
            pl.BlockSpec((tm, ATT_W), lambda i: (i, 0)),
            pl.BlockSpec((tm, LRU_W), lambda i: (i, 0)),
            pl.BlockSpec((tm, HY_W), lambda i: (i, 0)),
            pl.BlockSpec((None, D_MODEL, D_MODEL), lambda i: (layer, 0, 0), pipeline_mode=pl.Buffered(1)),
            pl.BlockSpec((1, D_MODEL), lambda i: (0, 0)),
        ],
        out_specs=pl.BlockSpec((tm, D_MODEL), lambda i: (i, 0)),
        out_shape=jax.ShapeDtypeStruct((n_tok, D_MODEL), F32),
        compiler_params=_params("arbitrary"),
        name="outproj",
    )(x, mod4, att, lru, hy, w_out_bf, final_g)


def kernel(x_prompt, x_sample, cache_k, cache_v, state_lru, c, c_ctx, norm_g, w_ada, b_ada, w_in, w_out, lam_q1, lam_k1, lam_q2, lam_k2, attn_subln_g, lru_conv_w, lru_conv_b, lru_wa, lru_ba, lru_wi, lru_bi, lru_lam, hy_conv_w, hy_conv_b, hy_w1, hy_b1, hy_w2, hy_b2, hy_w3, hy_d, final_g):
    cos, sin = _rope_tables()
    tables = {}
    for L in (SEQ, DEC_SEQ):
        fwd, inv = _dft_tables(L)
        z, decay = _hyena_tables(L)
        tables[L] = (fwd.astype(BF16), inv.astype(BF16), z, decay)

    cond = jnp.concatenate([c_ctx[None, :], c, jnp.zeros((N_COND - 1 - DEC_BATCH, D_MODEL), F32)], axis=0)
    mod4 = _ada(cond, w_ada, b_ada).reshape(DEPTH, N_COND, 1, 3 * D_MODEL)

    xc = x_prompt.reshape(T_CTX, D_MODEL)
    xs = x_sample.reshape(T_SMP, D_MODEL)
    ck = cache_k.reshape(DEC_BATCH, DEPTH, PAST_LEN * N_HEADS, HEAD_W)
    cv = cache_v.reshape(DEC_BATCH, DEPTH, PAST_LEN * N_HEADS, HEAD_W)
    h0_ctx = jnp.zeros((1, 2, BATCH, LRU_W), F32)
    h0_smp = jnp.transpose(state_lru, (1, 2, 0, 3))
    norm_g3 = norm_g[:, None, :]
    lamp = jnp.stack([lam_q1, lam_k1, lam_q2, lam_k2], axis=1)
    subln = attn_subln_g[:, None, :]
    w_out_bf = w_out.astype(BF16)
    lru_args = (lru_conv_w, lru_conv_b[:, None, :], lru_wa, lru_wi, lru_ba, lru_bi, lru_lam)
    hy_prm = _hyena_params(hy_conv_w, hy_conv_b, hy_d)
    filt_args = (hy_w1, hy_b1[:, None, :], hy_w2, hy_b2[:, None, :], hy_w3)
    fg = final_g[None, :]

    assert DEPTH == 2
    caches, hs = None, []
    for l in range(DEPTH):
        final = l == DEPTH - 1
        lam_init = 0.8 - 0.6 * math.exp(-0.3 * l)

        fwd, inv, z, decay = tables[DEC_SEQ]
        proj, w_in_bf = _inproj_w32(xs, norm_g3, mod4, w_in, l, DEC_SEQ)
        att = _attn_smp(proj, ck, cv, lamp, subln, cos, sin, l, lam_init)
        lru, _ = _lru(proj, *lru_args, h0_smp, l, l, DEC_SEQ, DEC_BATCH)
        g = _filters(z, decay, *filt_args, fwd, l, DEC_SEQ)
        hy = _hyena(proj, hy_prm, g, fwd, inv, l, DEC_SEQ, DEC_BATCH, HY_W // 2, 2)
        xs = _outproj(xs, mod4, att, lru, hy, w_out_bf, fg, l, final, DEC_SEQ)

        fwd, inv, z, decay = tables[SEQ]
        proj, *caches = _inproj_cache(xc, norm_g3, mod4, w_in_bf, l, caches)
        att = _attn_ctx(proj, lamp, subln, l, lam_init)
        lru, h_last = _lru(proj, *lru_args, h0_ctx, l, 0, SEQ, 8)
        hs.append(h_last)
        g = _filters(z, decay, *filt_args, fwd, l, SEQ)
        hy = _hyena(proj, hy_prm, g, fwd, inv, l, SEQ, BATCH, HY_W, 2)
        xc = _outproj(xc, mod4, att, lru, hy, w_out_bf, fg, l, final, None)

    y_prompt = xc.reshape(BATCH, SEQ, D_MODEL)
    y_sample = xs.reshape(DEC_BATCH, DEC_SEQ, D_MODEL)
    new_k, new_v = (a.reshape(BATCH, DEPTH, SEQ, N_HEADS, HEAD_W) for a in caches)
    new_state = jnp.transpose(jnp.stack(hs, axis=0), (2, 0, 1, 3))
    return (y_prompt, y_sample, new_k, new_v, new_state)
```

```python
import functools
import math

import numpy as np
import jax
import jax.numpy as jnp
from jax import lax
from jax.experimental import pallas as pl
from jax.experimental.pallas import tpu as pltpu

D_MODEL = 2048
BATCH = 32
SEQ = 256
DEPTH = 2
DEC_BATCH = 2
DEC_SEQ = 1024
PAST_LEN = 512
GRID_W = 64
ATT_W = 1024
LRU_W = 512
HY_W = 512
HEAD_DIM = 64
N_HEADS = 8
HEAD_W = 2 * HEAD_DIM
LRU_BLOCKS = 8
LRU_BW = LRU_W // LRU_BLOCKS
LRU_C = 8.0
HY_BANDS = 16
HY_POS = 1 + 2 * HY_BANDS
HY_HIDDEN = 64
HY_DECAY_FAST = 0.3
HY_DECAY_SLOW = 1.5
HY_DECAY_TARGET = 1e-2
ROPE_BASE = 10000.0
EPS = 1e-6
IN_W = 4 * ATT_W + 2 * LRU_W + 4 * HY_W

T_CTX = BATCH * SEQ
T_SMP = DEC_BATCH * DEC_SEQ
N_COND = 8

SLAB_W = 1024
N_SLABS = IN_W // SLAB_W
S_Q, S_K, S_V, S_AG, S_LX_LG, S_HV_HX1, S_HX0_HG = range(N_SLABS)

LANES = 128
F32 = jnp.float32
BF16 = jnp.bfloat16
VMEM_LIMIT = 58 * 1024 * 1024


LOG2E = 1.4426950408889634


def _sigmoid(x):
    return 1.0 / (1.0 + jnp.exp2(x * (-LOG2E)))


def _silu(x):
    return x * _sigmoid(x)


def _dot(a, b):
    return jnp.dot(a, b, preferred_element_type=F32)


def _dot_nt(a, b):
    return lax.dot_general(a, b, (((1,), (1,)), ((), ())), preferred_element_type=F32)


def _dot_hi(a, b):
    return jnp.dot(a, b, precision=lax.Precision.HIGHEST, preferred_element_type=F32)


def _params(*sem):
    return pltpu.CompilerParams(dimension_semantics=sem, vmem_limit_bytes=VMEM_LIMIT)


def _rope_tables():
    t = np.arange(DEC_SEQ)
    pos = np.stack([t // GRID_W, t % GRID_W], axis=1).astype(np.float64)
    nf = HEAD_DIM // 4
    inv = ROPE_BASE ** (-np.arange(nf, dtype=np.float64) / nf)
    lane = np.arange(HEAD_W)
    j = lane % HEAD_DIM
    axis = j // (HEAD_DIM // 2)
    f = j % nf
    upper = (j % (HEAD_DIM // 2)) >= nf
    ang = pos[:, axis] * inv[f][None, :]
    cos = np.cos(ang)
    sin = np.sin(ang) * np.where(upper, 1.0, -1.0)[None, :]
    return jnp.asarray(cos, F32), jnp.asarray(sin, F32)


def _dft_tables(L):
    n = 2 * L
    k = np.arange(L)[:, None]
    t = np.arange(L)[None, :]
    ang = 2.0 * np.pi * ((k * t) % n).astype(np.float64) / n
    fwd = np.concatenate([np.cos(ang), -np.sin(ang)], axis=0)
    fwd[L, :] = (-1.0) ** np.arange(L)
    wk = np.where(np.arange(L) == 0, 1.0, 2.0)[None, :]
    ang_t = ang.T
    inv = np.concatenate([wk * np.cos(ang_t), -2.0 * np.sin(ang_t)], axis=1) / n
    inv[:, L] = ((-1.0) ** np.arange(L)) / n
    return jnp.asarray(fwd, F32), jnp.asarray(inv, F32)


def _hyena_tables(L):
    pos = np.arange(L, dtype=np.float64)
    t = pos / float(max(L - 1, 1))
    bands = np.linspace(1e-4, HY_BANDS - 1, HY_BANDS)
    ang = (2.0 * math.pi / L) * pos[:, None] * bands[None, :]
    z = np.concatenate([t[:, None], np.cos(ang), np.sin(ang)], axis=-1)
    lo = abs(math.log(HY_DECAY_TARGET) / HY_DECAY_SLOW)
    hi = abs(math.log(HY_DECAY_TARGET) / HY_DECAY_FAST)
    deltas = np.linspace(lo, hi, HY_W)
    decay = np.exp(-t[:, None] * deltas[None, :])
    return jnp.asarray(z, F32), jnp.asarray(decay, F32)


def _ada_kernel(c_ref, w_ref, b_ref, o_ref):
    s = _silu(c_ref[...])
    o_ref[...] = _dot(s.astype(BF16), w_ref[...].astype(BF16)) + b_ref[...]


def _ada(cond, w_ada, b_ada):
    tn = 1024
    return pl.pallas_call(
        _ada_kernel,
        grid=(DEPTH, 3 * D_MODEL // tn),
        in_specs=[
            pl.BlockSpec((N_COND, D_MODEL), lambda l, j: (0, 0)),
            pl.BlockSpec((None, D_MODEL, tn), lambda l, j: (l, 0, j)),
            pl.BlockSpec((None, 1, tn), lambda l, j: (l, 0, j)),
        ],
        out_specs=pl.BlockSpec((None, N_COND, tn), lambda l, j: (l, 0, j)),
        out_shape=jax.ShapeDtypeStruct((DEPTH, N_COND, 3 * D_MODEL), F32),
        compiler_params=_params("arbitrary", "arbitrary"),
        name="ada",
    )(cond, w_ada, b_ada.reshape(DEPTH, 1, 3 * D_MODEL))


def _cond_row(seq_len, tm):
    if seq_len is None:
        return lambda i: 0
    return lambda i: 1 + i // (seq_len // tm)


IN_TM = 1024
NORM_ROWS = 32


def _norm_modulate(x_ref, g_ref, shift_ref, scale_ref, h_ref, h_row0=0):
    gain = g_ref[...] * (1.0 + scale_ref[...])
    sh = shift_ref[...]

    def body(r, carry):
        start = pl.multiple_of(r * NORM_ROWS, NORM_ROWS)
        x = x_ref[pl.ds(start, NORM_ROWS), :]
        ms = jnp.mean(x * x, axis=-1, keepdims=True)
        xn = x * lax.rsqrt(ms + EPS)
        h_ref[pl.ds(pl.multiple_of(h_row0 + start, NORM_ROWS), NORM_ROWS), :] = (xn * gain + sh).astype(BF16)
        return carry

    lax.fori_loop(0, IN_TM // NORM_ROWS, body, 0, unroll=4)


def _inproj_cache_kernel(*refs, first_layer):
    if first_layer:
        x_ref, g_ref, shift_ref, scale_ref, w_ref, o_ref, kc_ref, vc_ref, h_ref = refs
    else:
        x_ref, g_ref, shift_ref, scale_ref, w_ref, _, _, o_ref, kc_ref, vc_ref, h_ref = refs
    j = pl.program_id(1)

    @pl.when(j == 0)
    def _():
        _norm_modulate(x_ref, g_ref, shift_ref, scale_ref, h_ref)

    o_ref[...] = _dot(h_ref[...], w_ref[...])

    def scatter(dst_ref):
        for b in range(IN_TM // SEQ):
            for h in range(N_HEADS):
                dst_ref[b, pl.ds(h, SEQ, stride=N_HEADS), :] = o_ref[b * SEQ:(b + 1) * SEQ, h * HEAD_W:(h + 1) * HEAD_W]

    @pl.when(j == S_K)
    def _():
        scatter(kc_ref)

    @pl.when(j == S_V)
    def _():
        scatter(vc_ref)

    if first_layer:
        @pl.when(j == S_K + 1)
        def _():
            kc_ref[...] = jnp.zeros(kc_ref.shape, F32)

        @pl.when(j == S_V + 1)
        def _():
            vc_ref[...] = jnp.zeros(vc_ref.shape, F32)


def _of_layer(layer, shape):
    return pl.BlockSpec((None,) + tuple(shape), lambda *_: (layer,) + (0,) * len(shape))


def _inproj_specs(tm, layer, cond, w_spec):
    return [
        pl.BlockSpec((tm, D_MODEL), lambda i, j: (i, 0)),
        _of_layer(layer, (1, D_MODEL)),
        pl.BlockSpec((None, None, 1, D_MODEL), lambda i, j: (layer, cond(i), 0, 0)),
        pl.BlockSpec((None, None, 1, D_MODEL), lambda i, j: (layer, cond(i), 0, 1)),
        w_spec,
    ]


W32_K_CHUNK = 512


def _inproj_w32_kernel(x_ref, g_ref, shift_ref, scale_ref, w_ref, o_ref, wbf_ref, h_ref):
    row0 = pl.multiple_of(pl.program_id(1) * IN_TM, IN_TM)

    @pl.when(pl.program_id(0) == 0)
    def _():
        _norm_modulate(x_ref, g_ref, shift_ref, scale_ref, h_ref, row0)

    for k0 in range(0, D_MODEL, W32_K_CHUNK):
        w = w_ref[k0:k0 + W32_K_CHUNK, :].astype(BF16)
        wbf_ref[k0:k0 + W32_K_CHUNK, :] = w
        part = _dot(h_ref[pl.ds(row0, IN_TM), k0:k0 + W32_K_CHUNK], w)
        if k0 == 0:
            o_ref[...] = part
        else:
            o_ref[...] += part


def _inproj_w32(x, norm_g, mod4, w_in, layer, seq_len):
    n_tok = x.shape[0]
    tm = IN_TM
    n_tiles = n_tok // tm
    cond = _cond_row(seq_len, tm)
    x_row = lambda j, i: jnp.where(j == 0, i, n_tiles - 1)
    return pl.pallas_call(
        _inproj_w32_kernel,
        grid=(N_SLABS, n_tiles),
        in_specs=[
            pl.BlockSpec((tm, D_MODEL), lambda j, i: (x_row(j, i), 0), pipeline_mode=pl.Buffered(1)),
            _of_layer(layer, (1, D_MODEL)),
            pl.BlockSpec((None, None, 1, D_MODEL), lambda j, i: (layer, cond(i), 0, 0)),
            pl.BlockSpec((None, None, 1, D_MODEL), lambda j, i: (layer, cond(i), 0, 1)),
            pl.BlockSpec((None, D_MODEL, SLAB_W), lambda j, i: (layer, 0, j)),
        ],
        out_specs=[
            pl.BlockSpec((None, tm, SLAB_W), lambda j, i: (j, i, 0)),
            pl.BlockSpec((D_MODEL, SLAB_W), lambda j, i: (0, j)),
        ],
        out_shape=[jax.ShapeDtypeStruct((N_SLABS, n_tok, SLAB_W), F32), jax.ShapeDtypeStruct((D_MODEL, IN_W), BF16)],
        scratch_shapes=[pltpu.VMEM((n_tok, D_MODEL), BF16)],
        compiler_params=_params("arbitrary", "arbitrary"),
        name="inproj_w32",
    )(x, norm_g, mod4, mod4, w_in)


def _inproj_cache(x, norm_g, mod4, w_in_bf, layer, caches):
    tm = IN_TM
    nb = tm // SEQ
    first = caches is None
    cache_shape = jax.ShapeDtypeStruct((BATCH, DEPTH, SEQ * N_HEADS, HEAD_W), F32)
    if first:
        half = lambda slab: (lambda i, j: (i, jnp.where(j <= slab, layer, layer + 1), 0, 0))
    else:
        half = lambda slab: (lambda i, j: (i, layer, 0, 0))
    cache_spec = lambda slab: pl.BlockSpec((nb, None, SEQ * N_HEADS, HEAD_W), half(slab))
    w_spec = pl.BlockSpec((D_MODEL, SLAB_W), lambda i, j: (0, j))
    in_specs = _inproj_specs(tm, layer, _cond_row(None, tm), w_spec)
    args = (x, norm_g, mod4, mod4, w_in_bf)
    aliases = {}
    if not first:
        in_specs += [pl.BlockSpec(memory_space=pl.ANY)] * 2
        aliases = {len(args): 1, len(args) + 1: 2}
        args += tuple(caches)
    return pl.pallas_call(
        functools.partial(_inproj_cache_kernel, first_layer=first),
        grid=(T_CTX // tm, N_SLABS),
        in_specs=in_specs,
        out_specs=[pl.BlockSpec((None, tm, SLAB_W), lambda i, j: (j, i, 0)), cache_spec(S_K), cache_spec(S_V)],
        out_shape=[jax.ShapeDtypeStruct((N_SLABS, T_CTX, SLAB_W), F32), cache_shape, cache_shape],
        scratch_shapes=[pltpu.VMEM((tm, D_MODEL), BF16)],
        input_output_aliases=aliases,
        compiler_params=_params("arbitrary", "arbitrary"),
        name="inproj_cache",
    )(*args)


def _lam_value(lamp_ref, lam_init):
    lp = lamp_ref[...]
    t1 = jnp.sum(lp[0:1] * lp[1:2], axis=-1, keepdims=True)
    t2 = jnp.sum(lp[2:3] * lp[3:4], axis=-1, keepdims=True)
    return jnp.exp(t1) - jnp.exp(t2) + lam_init


def _diff_attend(items, lam, g, lam_init):
    m = items[0][0].shape[0]
    lane = lax.broadcasted_iota(jnp.int32, (m, HEAD_W), 1)
    ss = []
    for q, k_bf, _ in items:
        q1 = jnp.where(lane < HEAD_DIM, q, 0.0).astype(BF16)
        q2 = jnp.where(lane < HEAD_DIM, 0.0, q).astype(BF16)
        ss.append(_dot_nt(jnp.concatenate([q1, q2], axis=0), k_bf))
    es = [jnp.exp(s - jnp.max(s, axis=-1, keepdims=True)) for s in ss]
    ps = [e * (1.0 / jnp.sum(e, axis=-1, keepdims=True)) for e in es]
    ws = [(p[:m] - lam * p[m:]).astype(BF16) for p in ps]
    os_ = [_dot(w, v_bf) for w, (_, _, v_bf) in zip(ws, items)]
    outs = []
    for o in os_:
        ms = jnp.mean(o * o, axis=-1, keepdims=True)
        outs.append((o * lax.rsqrt(ms + EPS) * g) * (1.0 - lam_init))
    return outs


CTX_BATCHES = 2


def _attn_ctx_kernel(lamp_ref, g_ref, q_ref, k_ref, v_ref, ag_ref, o_ref, *, lam_init):
    lam = _lam_value(lamp_ref, lam_init)
    where = [(slice(b * SEQ, (b + 1) * SEQ), slice(h * HEAD_W, (h + 1) * HEAD_W))
             for b in range(CTX_BATCHES) for h in range(N_HEADS)]
    items = [(q_ref[r, c] * (HEAD_DIM ** -0.5), k_ref[r, c].astype(BF16), v_ref[r, c].astype(BF16)) for r, c in where]
    for (r, c), o in zip(where, _diff_attend(items, lam, g_ref[...], lam_init)):
        o_ref[r, c] = (o * _silu(ag_ref[r, c])).astype(BF16)


def _attn_ctx(proj, lamp, subln_g, layer, lam_init):
    rows = CTX_BATCHES * SEQ
    slab = lambda s: pl.BlockSpec((None, rows, ATT_W), lambda b, s=s: (s, b, 0))
    return pl.pallas_call(
        functools.partial(_attn_ctx_kernel, lam_init=lam_init),
        grid=(BATCH // CTX_BATCHES,),
        in_specs=[
            _of_layer(layer, (4, HEAD_DIM)),
            _of_layer(layer, (1, HEAD_W)),
            slab(S_Q), slab(S_K), slab(S_V), slab(S_AG),
        ],
        out_specs=pl.BlockSpec((rows, ATT_W), lambda b: (b, 0)),
        out_shape=jax.ShapeDtypeStruct((T_CTX, ATT_W), BF16),
        compiler_params=_params("arbitrary"),
        name="attn_ctx",
    )(lamp, subln_g, proj, proj, proj, proj)


Q_CHUNK = 512
Q_PART = 256
N_QC = DEC_SEQ // Q_CHUNK


def _rope(x, cos, sin):
    lane = lax.broadcasted_iota(jnp.int32, x.shape, 1)
    lower = (lane % (HEAD_DIM // 2)) < (HEAD_DIM // 4)
    nf = HEAD_DIM // 4
    partner = jnp.where(lower, pltpu.roll(x, HEAD_W - nf, 1), pltpu.roll(x, nf, 1))
    return x * cos + partner * sin


def _attn_smp_kernel(lamp_ref, g_ref, cosq_ref, sinq_ref, cos_ref, sin_ref, q_ref, k_ref, v_ref,
                     ck_ref, cv_ref, ag_ref, o_ref, ks_ref, vs_ref, *, lam_init):
    @pl.when(pl.program_id(2) == 0)
    def _():
        head_rows = pl.ds(pl.program_id(1), PAST_LEN, stride=N_HEADS)
        ks_ref[0:PAST_LEN, :] = ck_ref[head_rows, :].astype(BF16)
        vs_ref[0:PAST_LEN, :] = cv_ref[head_rows, :].astype(BF16)
        ks_ref[PAST_LEN:, :] = _rope(k_ref[...], cos_ref[...], sin_ref[...]).astype(BF16)
        vs_ref[PAST_LEN:, :] = v_ref[...].astype(BF16)

    lam = _lam_value(lamp_ref, lam_init)
    q = _rope(q_ref[...], cosq_ref[...], sinq_ref[...]) * (HEAD_DIM ** -0.5)
    k_bf, v_bf = ks_ref[...], vs_ref[...]
    parts = [slice(r, r + Q_PART) for r in range(0, Q_CHUNK, Q_PART)]
    items = [(q[p], k_bf, v_bf) for p in parts]
    for p, o in zip(parts, _diff_attend(items, lam, g_ref[...], lam_init)):
        o_ref[p, :] = (o * _silu(ag_ref[p, :])).astype(BF16)


def _attn_smp(proj, cache_k, cache_v, lamp, subln_g, cos, sin, layer, lam_init):
    full = lambda shape: pl.BlockSpec(shape, lambda b, h, c: (0, 0))
    return pl.pallas_call(
        functools.partial(_attn_smp_kernel, lam_init=lam_init),
        grid=(DEC_BATCH, N_HEADS, N_QC),
        in_specs=[
            _of_layer(layer, (4, HEAD_DIM)),
            _of_layer(layer, (1, HEAD_W)),
            pl.BlockSpec((Q_CHUNK, HEAD_W), lambda b, h, c: (c, 0)),
            pl.BlockSpec((Q_CHUNK, HEAD_W), lambda b, h, c: (c, 0)),
            full((DEC_SEQ, HEAD_W)),
            full((DEC_SEQ, HEAD_W)),
            pl.BlockSpec((None, Q_CHUNK, HEAD_W), lambda b, h, c: (S_Q, b * N_QC + c, h)),
            pl.BlockSpec((None, DEC_SEQ, HEAD_W), lambda b, h, c: (S_K, b, h)),
            pl.BlockSpec((None, DEC_SEQ, HEAD_W), lambda b, h, c: (S_V, b, h)),
            pl.BlockSpec((None, None, PAST_LEN * N_HEADS, HEAD_W), lambda b, h, c: (b, layer, 0, 0)),
            pl.BlockSpec((None, None, PAST_LEN * N_HEADS, HEAD_W), lambda b, h, c: (b, layer, 0, 0)),
            pl.BlockSpec((None, Q_CHUNK, HEAD_W), lambda b, h, c: (S_AG, b * N_QC + c, h)),
        ],
        out_specs=pl.BlockSpec((Q_CHUNK, HEAD_W), lambda b, h, c: (b * N_QC + c, h)),
        out_shape=jax.ShapeDtypeStruct((T_SMP, ATT_W), BF16),
        scratch_shapes=[
            pltpu.VMEM((PAST_LEN + DEC_SEQ, HEAD_W), BF16),
            pltpu.VMEM((PAST_LEN + DEC_SEQ, HEAD_W), BF16),
        ],
        compiler_params=_params("arbitrary", "arbitrary", "arbitrary"),
        name="attn_smp",
    )(lamp, subln_g, cos, sin, cos, sin, proj, proj, proj, cache_k, cache_v, proj)


GATE_ROWS = 256
LRU_SLABS = LRU_W // LANES


def _lru_kernel(x_ref, lg_ref, cw_ref, cb_ref, wa_ref, wi_ref, ba_ref, bi_ref, lam_ref, h0_ref, o_ref, hl_ref,
                xc_ref, a_ref, b_ref, h_ref, w_ref, *, L, G):
    @pl.when(pl.program_id(0) == 0)
    def _():
        w_ref[...] = jnp.zeros(w_ref.shape, BF16)
        for d in range(2):
            for gate, src in enumerate((wa_ref, wi_ref)):
                col0 = (2 * d + gate) * LRU_W
                for n in range(LRU_BLOCKS):
                    lo = n * LRU_BW
                    w_ref[lo:lo + LRU_BW, col0 + lo:col0 + lo + LRU_BW] = src[d, n].astype(BF16)

    nl = -lam_ref[...]
    neg_c = (-LRU_C * LOG2E) * (jnp.maximum(nl, 0.0) + jnp.log1p(jnp.exp(-jnp.abs(nl))))
    cw = cw_ref[...]
    cb = cb_ref[...]
    ba = ba_ref[...]
    bi = bi_ref[...]

    t = lax.broadcasted_iota(jnp.int32, (L, LRU_W), 0)

    def gates_of_sequence(g, carry):
        x = x_ref[pl.ds(pl.multiple_of(g * L, L), L), :]
        xc = cb + cw[0:1] * jnp.where(t < 2, 0.0, pltpu.roll(x, 2, 0))
        xc = xc + cw[1:2] * jnp.where(t < 1, 0.0, pltpu.roll(x, 1, 0))
        xc = xc + cw[2:3] * x
        xc_ref[...] = xc + cw[3:4] * jnp.where(t == L - 1, 0.0, pltpu.roll(x, L - 1, 0))
        for c in range(L // GATE_ROWS):
            r0 = c * GATE_ROWS
            xc = xc_ref[r0:r0 + GATE_ROWS, :]
            gates = _dot(xc.astype(BF16), w_ref[...])
            for d in range(2):
                r = _sigmoid(gates[:, (2 * d) * LRU_W:(2 * d + 1) * LRU_W] + ba[d:d + 1])
                i = _sigmoid(gates[:, (2 * d + 1) * LRU_W:(2 * d + 2) * LRU_W] + bi[d:d + 1])
                a = jnp.exp2(r * neg_c[d:d + 1])
                a2 = 1.0 - a * a
                b = jnp.where(a2 > 0.0, a2 * lax.rsqrt(a2), 0.0) * (i * xc)
                rows = pl.ds(r0 * G + g, GATE_ROWS, stride=G)
                for s in range(LRU_SLABS):
                    a_ref[d * LRU_SLABS + s, rows, :] = a[:, s * LANES:(s + 1) * LANES]
                    b_ref[d * LRU_SLABS + s, rows, :] = b[:, s * LANES:(s + 1) * LANES]
        return carry

    lax.fori_loop(0, G, gates_of_sequence, 0)

    def step(t, hs):
        out = []
        for d in range(2):
            tt = t if d == 0 else L - 1 - t
            rows = pl.ds(pl.multiple_of(tt * G, G), G)
            for s in range(LRU_SLABS):
                k = d * LRU_SLABS + s
                h = a_ref[k, rows, :] * hs[k] + b_ref[k, rows, :]
                h_ref[k, rows, :] = h
                out.append(h)
        return tuple(out)

    h0 = tuple(h0_ref[d, :, s * LANES:(s + 1) * LANES] for d in range(2) for s in range(LRU_SLABS))
    hs = lax.fori_loop(0, L, step, h0, unroll=2)
    for d in range(2):
        for s in range(LRU_SLABS):
            hl_ref[d, :, s * LANES:(s + 1) * LANES] = hs[d * LRU_SLABS + s]

    def write_sequence(g, carry):
        rows = pl.ds(g, L, stride=G)
        orow = pl.ds(pl.multiple_of(g * L, L), L)
        for s in range(LRU_SLABS):
            lanes = slice(s * LANES, (s + 1) * LANES)
            h = h_ref[s, rows, :] + h_ref[LRU_SLABS + s, rows, :]
            o_ref[orow, lanes] = (h * _silu(lg_ref[orow, lanes])).astype(BF16)
        return carry

    lax.fori_loop(0, G, write_sequence, 0)


def _lru(proj, conv_w, conv_b, wa, wi, ba, bi, lam, h0, layer, h0_layer, L, G):
    nseq = h0.shape[2]
    return pl.pallas_call(
        functools.partial(_lru_kernel, L=L, G=G),
        grid=(nseq // G,),
        in_specs=[
            pl.BlockSpec((None, G * L, LRU_W), lambda s: (S_LX_LG, s, 0)),
            pl.BlockSpec((None, G * L, LRU_W), lambda s: (S_LX_LG, s, 1)),
            _of_layer(layer, (4, LRU_W)),
            _of_layer(layer, (1, LRU_W)),
            _of_layer(layer, (2, LRU_BLOCKS, LRU_BW, LRU_BW)),
            _of_layer(layer, (2, LRU_BLOCKS, LRU_BW, LRU_BW)),
            _of_layer(layer, (2, LRU_W)),
            _of_layer(layer, (2, LRU_W)),
            _of_layer(layer, (2, LRU_W)),
            pl.BlockSpec((None, 2, G, LRU_W), lambda s: (h0_layer, 0, s, 0)),
        ],
        out_specs=[
            pl.BlockSpec((G * L, LRU_W), lambda s: (s, 0)),
            pl.BlockSpec((2, G, LRU_W), lambda s: (0, s, 0)),
        ],
        out_shape=[
            jax.ShapeDtypeStruct((nseq * L, LRU_W), BF16),
            jax.ShapeDtypeStruct((2, nseq, LRU_W), F32),
        ],
        scratch_shapes=[
            pltpu.VMEM((L, LRU_W), F32),
            pltpu.VMEM((2 * LRU_SLABS, L * G, LANES), F32),
            pltpu.VMEM((2 * LRU_SLABS, L * G, LANES), F32),
            pltpu.VMEM((2 * LRU_SLABS, L * G, LANES), F32),
            pltpu.VMEM((LRU_W, 4 * LRU_W), BF16),
        ],
        compiler_params=_params("arbitrary"),
        name=f"lru_{L}",
    )(proj, proj, conv_w, conv_b, wa, wi, ba, bi, lam, h0)


FILT_ROWS = 256


def _filt_kernel(z_ref, decay_ref, w1_ref, b1_ref, w2_ref, b2_ref, w3_ref, ft_ref, fb_ref, g_ref, taps_ref, *, L):
    i = pl.program_id(0)

    @pl.when(i == 0)
    def _():
        h = jnp.sin(_dot_hi(z_ref[...], w1_ref[...]) + b1_ref[...])
        h = jnp.sin(_dot_hi(h, w2_ref[...]) + b2_ref[...])
        filt = _dot_hi(h, w3_ref[...])
        decay = decay_ref[...]
        row = lax.broadcasted_iota(jnp.int32, (L, HY_W), 0)
        taps_ref[:, :HY_W] = (filt[:, :HY_W] * decay).astype(BF16)
        taps_ref[:, HY_W:] = jnp.where(row == 0, 0.0, filt[:, HY_W:] * decay).astype(BF16)

    taps = taps_ref[...]
    top = _dot(ft_ref[...], taps)
    bot = _dot(fb_ref[...], taps)
    first = (lax.broadcasted_iota(jnp.int32, (FILT_ROWS, HY_W), 0) + i * FILT_ROWS) == 0
    gt = top[:, :HY_W] + top[:, HY_W:]
    g_ref[0] = gt
    g_ref[1] = jnp.where(first, 0.0, bot[:, :HY_W] - bot[:, HY_W:])
    g_ref[2] = jnp.where(first, bot[:, :HY_W] + bot[:, HY_W:], gt)


def _filters(z, decay, w1, b1, w2, b2, w3, fwd_bf, layer, L):
    full = lambda shape: pl.BlockSpec(shape, lambda i: (0,) * len(shape))
    nblk = L // FILT_ROWS
    return pl.pallas_call(
        functools.partial(_filt_kernel, L=L),
        grid=(nblk,),
        in_specs=[
            full((L, HY_POS)), full((L, HY_W)),
            _of_layer(layer, (HY_POS, HY_HIDDEN)), _of_layer(layer, (1, HY_HIDDEN)),
            _of_layer(layer, (HY_HIDDEN, HY_HIDDEN)), _of_layer(layer, (1, HY_HIDDEN)),
            _of_layer(layer, (HY_HIDDEN, 2 * HY_W)),
            pl.BlockSpec((FILT_ROWS, L), lambda i: (i, 0)),
            pl.BlockSpec((FILT_ROWS, L), lambda i: (nblk + i, 0)),
        ],
        out_specs=pl.BlockSpec((3, FILT_ROWS, HY_W), lambda i: (0, i, 0)),
        out_shape=jax.ShapeDtypeStruct((3, L, HY_W), F32),
        scratch_shapes=[pltpu.VMEM((L, 2 * HY_W), BF16)],
        compiler_params=_params("arbitrary"),
        name=f"hyena_filters_{L}",
    )(z, decay, w1, b1, w2, b2, w3, fwd_bf, fwd_bf)


HY_PARAM_ROWS = 16


def _hyena_kernel(hv_ref, hx1_ref, hx0_ref, hg_ref, prm_ref, g_ref, f_ref, fi_ref, o_ref, *, L, ch, seqs):
    prm = prm_ref[...]
    rows = [slice(s * L, (s + 1) * L) for s in range(seqs)]

    t = lax.broadcasted_iota(jnp.int32, (L, ch), 0)

    def conv(ref, stream, s):
        x = ref[rows[s], :]
        prev = jnp.where(t == 0, 0.0, pltpu.roll(x, 1, 0))
        nxt = jnp.where(t == L - 1, 0.0, pltpu.roll(x, L - 1, 0))
        u = prm[9 + stream:10 + stream] + prm[3 * stream:3 * stream + 1] * prev
        u = u + prm[3 * stream + 1:3 * stream + 2] * x
        return u + prm[3 * stream + 2:3 * stream + 3] * nxt

    zs = [conv(hx1_ref, 1, s) * conv(hv_ref, 0, s) for s in range(seqs)]
    specs = [_dot(f_ref[...], z.astype(BF16)) for z in zs]
    gx = g_ref[1]
    prods = []
    for spec in specs:
        top, bot = spec[:L], spec[L:]
        yt = top * g_ref[0] - bot * gx
        yb = top * gx + bot * g_ref[2]
        prods.append(jnp.concatenate([yt, yb], axis=0).astype(BF16))
    ys = [_dot(fi_ref[...], p) for p in prods]
    for s in range(seqs):
        hy = conv(hx0_ref, 2, s) * (ys[s] + zs[s] * prm[12:13])
        o_ref[rows[s], :] = (hy * _silu(hg_ref[rows[s], :])).astype(BF16)


def _hyena(proj, prm, g, fwd, inv, layer, L, nseq, ch, seqs):
    nch = HY_W // ch
    full = lambda shape: pl.BlockSpec(shape, lambda s, j: (0,) * len(shape))
    col = lambda slab, off: pl.BlockSpec((None, seqs * L, ch), lambda s, j, slab=slab, off=off: (slab, s, off * nch + j))
    return pl.pallas_call(
        functools.partial(_hyena_kernel, L=L, ch=ch, seqs=seqs),
        grid=(nseq // seqs, nch),
        in_specs=[
            col(S_HV_HX1, 0), col(S_HV_HX1, 1), col(S_HX0_HG, 0), col(S_HX0_HG, 1),
            pl.BlockSpec((None, HY_PARAM_ROWS, ch), lambda s, j: (layer, 0, j)),
            pl.BlockSpec((3, L, ch), lambda s, j: (0, 0, j)),
            full((2 * L, L)), full((L, 2 * L)),
        ],
        out_specs=pl.BlockSpec((seqs * L, ch), lambda s, j: (s, j)),
        out_shape=jax.ShapeDtypeStruct((nseq * L, HY_W), BF16),
        compiler_params=_params("arbitrary", "arbitrary"),
        name=f"hyena_{L}",
    )(proj, proj, proj, proj, prm, g, fwd, inv)


def _hyena_params(conv_w, conv_b, d):
    rows = [conv_w[:, :, s * HY_W:(s + 1) * HY_W] for s in range(3)]
    rows += [conv_b[:, None, s * HY_W:(s + 1) * HY_W] for s in range(3)]
    rows += [d[:, None, :], jnp.zeros((DEPTH, HY_PARAM_ROWS - 13, HY_W), F32)]
    return jnp.concatenate(rows, axis=1)


OUT_TM = 512


def _outproj_kernel(x_ref, gate_ref, att_ref, lru_ref, hy_ref, w_ref, fg_ref, o_ref, *, final):
    for r0 in range(0, OUT_TM, OUT_TM // 2):
        rows = slice(r0, r0 + OUT_TM // 2)
        acc = _dot(att_ref[rows, :], w_ref[0:ATT_W, :])
        acc = acc + _dot(lru_ref[rows, :], w_ref[ATT_W:ATT_W + LRU_W, :])
        acc = acc + _dot(hy_ref[rows, :], w_ref[ATT_W + LRU_W:, :])
        y = x_ref[rows, :] + gate_ref[...] * acc
        if final:
            ms = jnp.mean(y * y, axis=-1, keepdims=True)
            y = (y * lax.rsqrt(ms + EPS)) * fg_ref[...]
        o_ref[rows, :] = y


def _outproj(x, mod4, att, lru, hy, w_out_bf, final_g, layer, final, seq_len):
    n_tok = x.shape[0]
    tm = OUT_TM
    cond = _cond_row(seq_len, tm)
    return pl.pallas_call(
        functools.partial(_outproj_kernel, final=final),
        grid=(n_tok // tm,),
        in_specs=[
            pl.BlockSpec((tm, D_MODEL), lambda i: (i, 0)),
            pl.BlockSpec((None, None, 1, D_MODEL), lambda i: (layer, cond(i), 0, 2)),
            pl.BlockSpec((tm, ATT_W), lambda i: (i, 0)),
            pl.BlockSpec((tm, LRU_W), lambda i: (i, 0)),
            pl.BlockSpec((tm, HY_W), lambda i: (i, 0)),
            pl.BlockSpec((None, D_MODEL, D_MODEL), lambda i: (layer, 0, 0), pipeline_mode=pl.Buffered(1)),
            pl.BlockSpec((1, D_MODEL), lambda i: (0, 0)),
        ],
        out_specs=pl.BlockSpec((tm, D_MODEL), lambda i: (i, 0)),
        out_shape=jax.ShapeDtypeStruct((n_tok, D_MODEL), F32),
        compiler_params=_params("arbitrary"),
        name="outproj",
    )(x, mod4, att, lru, hy, w_out_bf, final_g)


def kernel(x_prompt, x_sample, cache_k, cache_v, state_lru, c, c_ctx, norm_g, w_ada, b_ada, w_in, w_out, lam_q1, lam_k1, lam_q2, lam_k2, attn_subln_g, lru_conv_w, lru_conv_b, lru_wa, lru_ba, lru_wi, lru_bi, lru_lam, hy_conv_w, hy_conv_b, hy_w1, hy_b1, hy_w2, hy_b2, hy_w3, hy_d, final_g):
    cos, sin = _rope_tables()
    tables = {}
    for L in (SEQ, DEC_SEQ):
        fwd, inv = _dft_tables(L)
        z, decay = _hyena_tables(L)
        tables[L] = (fwd.astype(BF16), inv.astype(BF16), z, decay)

    cond = jnp.concatenate([c_ctx[None, :], c, jnp.zeros((N_COND - 1 - DEC_BATCH, D_MODEL), F32)], axis=0)
    mod4 = _ada(cond, w_ada, b_ada).reshape(DEPTH, N_COND, 1, 3 * D_MODEL)

    xc = x_prompt.reshape(T_CTX, D_MODEL)
    xs = x_sample.reshape(T_SMP, D_MODEL)
    ck = cache_k.reshape(DEC_BATCH, DEPTH, PAST_LEN * N_HEADS, HEAD_W)
    cv = cache_v.reshape(DEC_BATCH, DEPTH, PAST_LEN * N_HEADS, HEAD_W)
    h0_ctx = jnp.zeros((1, 2, BATCH, LRU_W), F32)
    h0_smp = jnp.transpose(state_lru, (1, 2, 0, 3))
    norm_g3 = norm_g[:, None, :]
    lamp = jnp.stack([lam_q1, lam_k1, lam_q2, lam_k2], axis=1)
    subln = attn_subln_g[:, None, :]
    w_out_bf = w_out.astype(BF16)
    lru_args = (lru_conv_w, lru_conv_b[:, None, :], lru_wa, lru_wi, lru_ba, lru_bi, lru_lam)
    hy_prm = _hyena_params(hy_conv_w, hy_conv_b, hy_d)
    filt_args = (hy_w1, hy_b1[:, None, :], hy_w2, hy_b2[:, None, :], hy_w3)
    fg = final_g[None, :]

    assert DEPTH == 2
    caches, hs = None, []
    for l in range(DEPTH):
        final = l == DEPTH - 1
        lam_init = 0.8 - 0.6 * math.exp(-0.3 * l)

        fwd, inv, z, decay = tables[DEC_SEQ]
        proj, w_in_bf = _inproj_w32(xs, norm_g3, mod4, w_in, l, DEC_SEQ)
        att = _attn_smp(proj, ck, cv, lamp, subln, cos, sin, l, lam_init)
        lru, _ = _lru(proj, *lru_args, h0_smp, l, l, DEC_SEQ, DEC_BATCH)
        g = _filters(z, decay, *filt_args, fwd, l, DEC_SEQ)
        hy = _hyena(proj, hy_prm, g, fwd, inv, l, DEC_SEQ, DEC_BATCH, HY_W // 2, 2)
        xs = _outproj(xs, mod4, att, lru, hy, w_out_bf, fg, l, final, DEC_SEQ)

        fwd, inv, z, decay = tables[SEQ]
        proj, *caches = _inproj_cache(xc, norm_g3, mod4, w_in_bf, l, caches)
        att = _attn_ctx(proj, lamp, subln, l, lam_init)
        lru, h_last = _lru(proj, *lru_args, h0_ctx, l, 0, SEQ, 8)
        hs.append(h_last)
        g = _filters(z, decay, *filt_args, fwd, l, SEQ)
        hy = _hyena(proj, hy_prm, g, fwd, inv, l, SEQ, BATCH, HY_W, 2)
        xc = _outproj(xc, mod4, att, lru, hy, w_out_bf, fg, l, final, None)

    y_prompt = xc.reshape(BATCH, SEQ, D_MODEL)
    y_sample = xs.reshape(DEC_BATCH, DEC_SEQ, D_MODEL)
    new_k, new_v = (a.reshape(BATCH, DEPTH, SEQ, N_HEADS, HEAD_W) for a in caches)
    new_state = jnp.transpose(jnp.stack(hs, axis=0), (2, 0, 1, 3))
    return (y_prompt, y_sample, new_k, new_v, new_state)
```

```python
import functools
import math

import numpy as np
import jax
import jax.numpy as jnp
from jax import lax
from jax.experimental import pallas as pl
from jax.experimental.pallas import tpu as pltpu

D_MODEL = 2048
BATCH = 32
SEQ = 256
DEPTH = 2
DEC_BATCH = 2
DEC_SEQ = 1024
PAST_LEN = 512
GRID_W = 64
ATT_W = 1024
LRU_W = 512
HY_W = 512
HEAD_DIM = 64
N_HEADS = 8
HEAD_W = 2 * HEAD_DIM
LRU_BLOCKS = 8
LRU_BW = LRU_W // LRU_BLOCKS
LRU_C = 8.0
HY_BANDS = 16
HY_POS = 1 + 2 * HY_BANDS
HY_HIDDEN = 64
HY_DECAY_FAST = 0.3
HY_DECAY_SLOW = 1.5
HY_DECAY_TARGET = 1e-2
ROPE_BASE = 10000.0
EPS = 1e-6
IN_W = 4 * ATT_W + 2 * LRU_W + 4 * HY_W

T_CTX = BATCH * SEQ
T_SMP = DEC_BATCH * DEC_SEQ
N_COND = 8

SLAB_W = 1024
N_SLABS = IN_W // SLAB_W
S_Q, S_K, S_V, S_AG, S_LX_LG, S_HV_HX1, S_HX0_HG = range(N_SLABS)

LANES = 128
F32 = jnp.float32
BF16 = jnp.bfloat16
VMEM_LIMIT = 58 * 1024 * 1024


LOG2E = 1.4426950408889634


def _sigmoid(x):
    return 1.0 / (1.0 + jnp.exp2(x * (-LOG2E)))


def _silu(x):
    return x * _sigmoid(x)


def _dot(a, b):
    return jnp.dot(a, b, preferred_element_type=F32)


def _dot_nt(a, b):
    return lax.dot_general(a, b, (((1,), (1,)), ((), ())), preferred_element_type=F32)


def _dot_hi(a, b):
    return jnp.dot(a, b, precision=lax.Precision.HIGHEST, preferred_element_type=F32)


def _params(*sem):
    return pltpu.CompilerParams(dimension_semantics=sem, vmem_limit_bytes=VMEM_LIMIT)


def _rope_tables():
    t = np.arange(DEC_SEQ)
    pos = np.stack([t // GRID_W, t % GRID_W], axis=1).astype(np.float64)
    nf = HEAD_DIM // 4
    inv = ROPE_BASE ** (-np.arange(nf, dtype=np.float64) / nf)
    lane = np.arange(HEAD_W)
    j = lane % HEAD_DIM
    axis = j // (HEAD_DIM // 2)
    f = j % nf
    upper = (j % (HEAD_DIM // 2)) >= nf
    ang = pos[:, axis] * inv[f][None, :]
    cos = np.cos(ang)
    sin = np.sin(ang) * np.where(upper, 1.0, -1.0)[None, :]
    return jnp.asarray(cos, F32), jnp.asarray(sin, F32)


def _dft_tables(L):
    n = 2 * L
    k = np.arange(L)[:, None]
    t = np.arange(L)[None, :]
    ang = 2.0 * np.pi * ((k * t) % n).astype(np.float64) / n
    fwd = np.concatenate([np.cos(ang), -np.sin(ang)], axis=0)
    fwd[L, :] = (-1.0) ** np.arange(L)
    wk = np.where(np.arange(L) == 0, 1.0, 2.0)[None, :]
    ang_t = ang.T
    inv = np.concatenate([wk * np.cos(ang_t), -2.0 * np.sin(ang_t)], axis=1) / n
    inv[:, L] = ((-1.0) ** np.arange(L)) / n
    return jnp.asarray(fwd, F32), jnp.asarray(inv, F32)


def _hyena_tables(L):
    pos = np.arange(L, dtype=np.float64)
    t = pos / float(max(L - 1, 1))
    bands = np.linspace(1e-4, HY_BANDS - 1, HY_BANDS)
    ang = (2.0 * math.pi / L) * pos[:, None] * bands[None, :]
    z = np.concatenate([t[:, None], np.cos(ang), np.sin(ang)], axis=-1)
    lo = abs(math.log(HY_DECAY_TARGET) / HY_DECAY_SLOW)
    hi = abs(math.log(HY_DECAY_TARGET) / HY_DECAY_FAST)
    deltas = np.linspace(lo, hi, HY_W)
    decay = np.exp(-t[:, None] * deltas[None, :])
    return jnp.asarray(z, F32), jnp.asarray(decay, F32)


def _ada_kernel(c_ref, w_ref, b_ref, o_ref):
    s = _silu(c_ref[...])
    o_ref[...] = _dot(s.astype(BF16), w_ref[...].astype(BF16)) + b_ref[...]


def _ada(cond, w_ada, b_ada):
    tn = 1024
    return pl.pallas_call(
        _ada_kernel,
        grid=(DEPTH, 3 * D_MODEL // tn),
        in_specs=[
            pl.BlockSpec((N_COND, D_MODEL), lambda l, j: (0, 0)),
            pl.BlockSpec((None, D_MODEL, tn), lambda l, j: (l, 0, j)),
            pl.BlockSpec((None, 1, tn), lambda l, j: (l, 0, j)),
        ],
        out_specs=pl.BlockSpec((None, N_COND, tn), lambda l, j: (l, 0, j)),
        out_shape=jax.ShapeDtypeStruct((DEPTH, N_COND, 3 * D_MODEL), F32),
        compiler_params=_params("arbitrary", "arbitrary"),
        name="ada",
    )(cond, w_ada, b_ada.reshape(DEPTH, 1, 3 * D_MODEL))


def _cond_row(seq_len, tm):
    if seq_len is None:
        return lambda i: 0
    return lambda i: 1 + i // (seq_len // tm)


IN_TM = 1024
NORM_ROWS = 32


def _norm_modulate(x_ref, g_ref, shift_ref, scale_ref, h_ref, h_row0=0):
    gain = g_ref[...] * (1.0 + scale_ref[...])
    sh = shift_ref[...]

    def body(r, carry):
        start = pl.multiple_of(r * NORM_ROWS, NORM_ROWS)
        x = x_ref[pl.ds(start, NORM_ROWS), :]
        ms = jnp.mean(x * x, axis=-1, keepdims=True)
        xn = x * lax.rsqrt(ms + EPS)
        h_ref[pl.ds(pl.multiple_of(h_row0 + start, NORM_ROWS), NORM_ROWS), :] = (xn * gain + sh).astype(BF16)
        return carry

    lax.fori_loop(0, IN_TM // NORM_ROWS, body, 0, unroll=4)


def _inproj_cache_kernel(*refs, first_layer):
    if first_layer:
        x_ref, g_ref, shift_ref, scale_ref, w_ref, o_ref, kc_ref, vc_ref, h_ref = refs
    else:
        x_ref, g_ref, shift_ref, scale_ref, w_ref, _, _, o_ref, kc_ref, vc_ref, h_ref = refs
    j = pl.program_id(1)

    @pl.when(j == 0)
    def _():
        _norm_modulate(x_ref, g_ref, shift_ref, scale_ref, h_ref)

    o_ref[...] = _dot(h_ref[...], w_ref[...])

    def scatter(dst_ref):
        for b in range(IN_TM // SEQ):
            for h in range(N_HEADS):
                dst_ref[b, pl.ds(h, SEQ, stride=N_HEADS), :] = o_ref[b * SEQ:(b + 1) * SEQ, h * HEAD_W:(h + 1) * HEAD_W]

    @pl.when(j == S_K)
    def _():
        scatter(kc_ref)

    @pl.when(j == S_V)
    def _():
        scatter(vc_ref)

    if first_layer:
        @pl.when(j == S_K + 1)
        def _():
            kc_ref[...] = jnp.zeros(kc_ref.shape, F32)

        @pl.when(j == S_V + 1)
        def _():
            vc_ref[...] = jnp.zeros(vc_ref.shape, F32)


def _of_layer(layer, shape):
    return pl.BlockSpec((None,) + tuple(shape), lambda *_: (layer,) + (0,) * len(shape))


def _inproj_specs(tm, layer, cond, w_spec):
    return [
        pl.BlockSpec((tm, D_MODEL), lambda i, j: (i, 0)),
        _of_layer(layer, (1, D_MODEL)),
        pl.BlockSpec((None, None, 1, D_MODEL), lambda i, j: (layer, cond(i), 0, 0)),
        pl.BlockSpec((None, None, 1, D_MODEL), lambda i, j: (layer, cond(i), 0, 1)),
        w_spec,
    ]


W32_K_CHUNK = 512


def _inproj_w32_kernel(x_ref, g_ref, shift_ref, scale_ref, w_ref, o_ref, wbf_ref, h_ref):
    row0 = pl.multiple_of(pl.program_id(1) * IN_TM, IN_TM)

    @pl.when(pl.program_id(0) == 0)
    def _():
        _norm_modulate(x_ref, g_ref, shift_ref, scale_ref, h_ref, row0)

    for k0 in range(0, D_MODEL, W32_K_CHUNK):
        w = w_ref[k0:k0 + W32_K_CHUNK, :].astype(BF16)
        wbf_ref[k0:k0 + W32_K_CHUNK, :] = w
        part = _dot(h_ref[pl.ds(row0, IN_TM), k0:k0 + W32_K_CHUNK], w)
        if k0 == 0:
            o_ref[...] = part
        else:
            o_ref[...] += part


def _inproj_w32(x, norm_g, mod4, w_in, layer, seq_len):
    n_tok = x.shape[0]
    tm = IN_TM
    n_tiles = n_tok // tm
    cond = _cond_row(seq_len, tm)
    x_row = lambda j, i: jnp.where(j == 0, i, n_tiles - 1)
    return pl.pallas_call(
        _inproj_w32_kernel,
        grid=(N_SLABS, n_tiles),
        in_specs=[
            pl.BlockSpec((tm, D_MODEL), lambda j, i: (x_row(j, i), 0), pipeline_mode=pl.Buffered(1)),
            _of_layer(layer, (1, D_MODEL)),
            pl.BlockSpec((None, None, 1, D_MODEL), lambda j, i: (layer, cond(i), 0, 0)),
            pl.BlockSpec((None, None, 1, D_MODEL), lambda j, i: (layer, cond(i), 0, 1)),
            pl.BlockSpec((None, D_MODEL, SLAB_W), lambda j, i: (layer, 0, j)),
        ],
        out_specs=[
            pl.BlockSpec((None, tm, SLAB_W), lambda j, i: (j, i, 0)),
            pl.BlockSpec((D_MODEL, SLAB_W), lambda j, i: (0, j)),
        ],
        out_shape=[jax.ShapeDtypeStruct((N_SLABS, n_tok, SLAB_W), F32), jax.ShapeDtypeStruct((D_MODEL, IN_W), BF16)],
        scratch_shapes=[pltpu.VMEM((n_tok, D_MODEL), BF16)],
        compiler_params=_params("arbitrary", "arbitrary"),
        name="inproj_w32",
    )(x, norm_g, mod4, mod4, w_in)


def _inproj_cache(x, norm_g, mod4, w_in_bf, layer, caches):
    tm = IN_TM
    nb = tm // SEQ
    first = caches is None
    cache_shape = jax.ShapeDtypeStruct((BATCH, DEPTH, SEQ * N_HEADS, HEAD_W), F32)
    if first:
        half = lambda slab: (lambda i, j: (i, jnp.where(j <= slab, layer, layer + 1), 0, 0))
    else:
        half = lambda slab: (lambda i, j: (i, layer, 0, 0))
    cache_spec = lambda slab: pl.BlockSpec((nb, None, SEQ * N_HEADS, HEAD_W), half(slab))
    w_spec = pl.BlockSpec((D_MODEL, SLAB_W), lambda i, j: (0, j))
    in_specs = _inproj_specs(tm, layer, _cond_row(None, tm), w_spec)
    args = (x, norm_g, mod4, mod4, w_in_bf)
    aliases = {}
    if not first:
        in_specs += [pl.BlockSpec(memory_space=pl.ANY)] * 2
        aliases = {len(args): 1, len(args) + 1: 2}
        args += tuple(caches)
    return pl.pallas_call(
        functools.partial(_inproj_cache_kernel, first_layer=first),
        grid=(T_CTX // tm, N_SLABS),
        in_specs=in_specs,
        out_specs=[pl.BlockSpec((None, tm, SLAB_W), lambda i, j: (j, i, 0)), cache_spec(S_K), cache_spec(S_V)],
        out_shape=[jax.ShapeDtypeStruct((N_SLABS, T_CTX, SLAB_W), F32), cache_shape, cache_shape],
        scratch_shapes=[pltpu.VMEM((tm, D_MODEL), BF16)],
        input_output_aliases=aliases,
        compiler_params=_params("arbitrary", "arbitrary"),
        name="inproj_cache",
    )(*args)


def _lam_value(lamp_ref, lam_init):
    lp = lamp_ref[...]
    t1 = jnp.sum(lp[0:1] * lp[1:2], axis=-1, keepdims=True)
    t2 = jnp.sum(lp[2:3] * lp[3:4], axis=-1, keepdims=True)
    return jnp.exp(t1) - jnp.exp(t2) + lam_init


def _diff_attend(items, lam, g, lam_init):
    m = items[0][0].shape[0]
    lane = lax.broadcasted_iota(jnp.int32, (m, HEAD_W), 1)
    ss = []
    for q, k_bf, _ in items:
        q1 = jnp.where(lane < HEAD_DIM, q, 0.0).astype(BF16)
        q2 = jnp.where(lane < HEAD_DIM, 0.0, q).astype(BF16)
        ss.append(_dot_nt(jnp.concatenate([q1, q2], axis=0), k_bf))
    es = [jnp.exp(s - jnp.max(s, axis=-1, keepdims=True)) for s in ss]
    ps = [e * (1.0 / jnp.sum(e, axis=-1, keepdims=True)) for e in es]
    ws = [(p[:m] - lam * p[m:]).astype(BF16) for p in ps]
    os_ = [_dot(w, v_bf) for w, (_, _, v_bf) in zip(ws, items)]
    outs = []
    for o in os_:
        ms = jnp.mean(o * o, axis=-1, keepdims=True)
        outs.append((o * lax.rsqrt(ms + EPS) * g) * (1.0 - lam_init))
    return outs


CTX_BATCHES = 2


def _attn_ctx_kernel(lamp_ref, g_ref, q_ref, k_ref, v_ref, ag_ref, o_ref, *, lam_init):
    lam = _lam_value(lamp_ref, lam_init)
    where = [(slice(b * SEQ, (b + 1) * SEQ), slice(h * HEAD_W, (h + 1) * HEAD_W))
             for b in range(CTX_BATCHES) for h in range(N_HEADS)]
    items = [(q_ref[r, c] * (HEAD_DIM ** -0.5), k_ref[r, c].astype(BF16), v_ref[r, c].astype(BF16)) for r, c in where]
    for (r, c), o in zip(where, _diff_attend(items, lam, g_ref[...], lam_init)):
        o_ref[r, c] = (o * _silu(ag_ref[r, c])).astype(BF16)


def _attn_ctx(proj, lamp, subln_g, layer, lam_init):
    rows = CTX_BATCHES * SEQ
    slab = lambda s: pl.BlockSpec((None, rows, ATT_W), lambda b, s=s: (s, b, 0))
    return pl.pallas_call(
        functools.partial(_attn_ctx_kernel, lam_init=lam_init),
        grid=(BATCH // CTX_BATCHES,),
        in_specs=[
            _of_layer(layer, (4, HEAD_DIM)),
            _of_layer(layer, (1, HEAD_W)),
            slab(S_Q), slab(S_K), slab(S_V), slab(S_AG),
        ],
        out_specs=pl.BlockSpec((rows, ATT_W), lambda b: (b, 0)),
        out_shape=jax.ShapeDtypeStruct((T_CTX, ATT_W), BF16),
        compiler_params=_params("arbitrary"),
        name="attn_ctx",
    )(lamp, subln_g, proj, proj, proj, proj)


Q_CHUNK = 512
Q_PART = 256
N_QC = DEC_SEQ // Q_CHUNK


def _rope(x, cos, sin):
    lane = lax.broadcasted_iota(jnp.int32, x.shape, 1)
    lower = (lane % (HEAD_DIM // 2)) < (HEAD_DIM // 4)
    nf = HEAD_DIM // 4
    partner = jnp.where(lower, pltpu.roll(x, HEAD_W - nf, 1), pltpu.roll(x, nf, 1))
    return x * cos + partner * sin


def _attn_smp_kernel(lamp_ref, g_ref, cosq_ref, sinq_ref, cos_ref, sin_ref, q_ref, k_ref, v_ref,
                     ck_ref, cv_ref, ag_ref, o_ref, ks_ref, vs_ref, *, lam_init):
    @pl.when(pl.program_id(2) == 0)
    def _():
        head_rows = pl.ds(pl.program_id(1), PAST_LEN, stride=N_HEADS)
        ks_ref[0:PAST_LEN, :] = ck_ref[head_rows, :].astype(BF16)
        vs_ref[0:PAST_LEN, :] = cv_ref[head_rows, :].astype(BF16)
        ks_ref[PAST_LEN:, :] = _rope(k_ref[...], cos_ref[...], sin_ref[...]).astype(BF16)
        vs_ref[PAST_LEN:, :] = v_ref[...].astype(BF16)

    lam = _lam_value(lamp_ref, lam_init)
    q = _rope(q_ref[...], cosq_ref[...], sinq_ref[...]) * (HEAD_DIM ** -0.5)
    k_bf, v_bf = ks_ref[...], vs_ref[...]
    parts = [slice(r, r + Q_PART) for r in range(0, Q_CHUNK, Q_PART)]
    items = [(q[p], k_bf, v_bf) for p in parts]
    for p, o in zip(parts, _diff_attend(items, lam, g_ref[...], lam_init)):
        o_ref[p, :] = (o * _silu(ag_ref[p, :])).astype(BF16)


def _attn_smp(proj, cache_k, cache_v, lamp, subln_g, cos, sin, layer, lam_init):
    full = lambda shape: pl.BlockSpec(shape, lambda b, h, c: (0, 0))
    return pl.pallas_call(
        functools.partial(_attn_smp_kernel, lam_init=lam_init),
        grid=(DEC_BATCH, N_HEADS, N_QC),
        in_specs=[
            _of_layer(layer, (4, HEAD_DIM)),
            _of_layer(layer, (1, HEAD_W)),
            pl.BlockSpec((Q_CHUNK, HEAD_W), lambda b, h, c: (c, 0)),
            pl.BlockSpec((Q_CHUNK, HEAD_W), lambda b, h, c: (c, 0)),
            full((DEC_SEQ, HEAD_W)),
            full((DEC_SEQ, HEAD_W)),
            pl.BlockSpec((None, Q_CHUNK, HEAD_W), lambda b, h, c: (S_Q, b * N_QC + c, h)),
            pl.BlockSpec((None, DEC_SEQ, HEAD_W), lambda b, h, c: (S_K, b, h)),
            pl.BlockSpec((None, DEC_SEQ, HEAD_W), lambda b, h, c: (S_V, b, h)),
            pl.BlockSpec((None, None, PAST_LEN * N_HEADS, HEAD_W), lambda b, h, c: (b, layer, 0, 0)),
            pl.BlockSpec((None, None, PAST_LEN * N_HEADS, HEAD_W), lambda b, h, c: (b, layer, 0, 0)),
            pl.BlockSpec((None, Q_CHUNK, HEAD_W), lambda b, h, c: (S_AG, b * N_QC + c, h)),
        ],
        out_specs=pl.BlockSpec((Q_CHUNK, HEAD_W), lambda b, h, c: (b * N_QC + c, h)),
        out_shape=jax.ShapeDtypeStruct((T_SMP, ATT_W), BF16),
        scratch_shapes=[
            pltpu.VMEM((PAST_LEN + DEC_SEQ, HEAD_W), BF16),
            pltpu.VMEM((PAST_LEN + DEC_SEQ, HEAD_W), BF16),
        ],
        compiler_params=_params("arbitrary", "arbitrary", "arbitrary"),
        name="attn_smp",
    )(lamp, subln_g, cos, sin, cos, sin, proj, proj, proj, cache_k, cache_v, proj)


GATE_ROWS = 256
LRU_SLABS = LRU_W // LANES


def _lru_kernel(x_ref, lg_ref, cw_ref, cb_ref, wa_ref, wi_ref, ba_ref, bi_ref, lam_ref, h0_ref, o_ref, hl_ref,
                xc_ref, a_ref, b_ref, h_ref, w_ref, *, L, G):
    @pl.when(pl.program_id(0) == 0)
    def _():
        w_ref[...] = jnp.zeros(w_ref.shape, BF16)
        for d in range(2):
            for gate, src in enumerate((wa_ref, wi_ref)):
                col0 = (2 * d + gate) * LRU_W
                for n in range(LRU_BLOCKS):
                    lo = n * LRU_BW
                    w_ref[lo:lo + LRU_BW, col0 + lo:col0 + lo + LRU_BW] = src[d, n].astype(BF16)

    nl = -lam_ref[...]
    neg_c = (-LRU_C * LOG2E) * (jnp.maximum(nl, 0.0) + jnp.log1p(jnp.exp(-jnp.abs(nl))))
    cw = cw_ref[...]
    cb = cb_ref[...]
    ba = ba_ref[...]
    bi = bi_ref[...]

    t = lax.broadcasted_iota(jnp.int32, (L, LRU_W), 0)

    def gates_of_sequence(g, carry):
        x = x_ref[pl.ds(pl.multiple_of(g * L, L), L), :]
        xc = cb + cw[0:1] * jnp.where(t < 2, 0.0, pltpu.roll(x, 2, 0))
        xc = xc + cw[1:2] * jnp.where(t < 1, 0.0, pltpu.roll(x, 1, 0))
        xc = xc + cw[2:3] * x
        xc_ref[...] = xc + cw[3:4] * jnp.where(t == L - 1, 0.0, pltpu.roll(x, L - 1, 0))
        for c in range(L // GATE_ROWS):
            r0 = c * GATE_ROWS
            xc = xc_ref[r0:r0 + GATE_ROWS, :]
            gates = _dot(xc.astype(BF16), w_ref[...])
            for d in range(2):
                r = 0.5 * jnp.tanh(0.5 * (gates[:, (2 * d) * LRU_W:(2 * d + 1) * LRU_W] + ba[d:d + 1])) + 0.5
                i = 0.5 * jnp.tanh(0.5 * (gates[:, (2 * d + 1) * LRU_W:(2 * d + 2) * LRU_W] + bi[d:d + 1])) + 0.5
                a = jnp.exp2(r * neg_c[d:d + 1])
                a2 = 1.0 - a * a
                b = jnp.where(a2 > 0.0, a2 * lax.rsqrt(a2), 0.0) * (i * xc)
                rows = pl.ds(r0 * G + g, GATE_ROWS, stride=G)
                for s in range(LRU_SLABS):
                    a_ref[d * LRU_SLABS + s, rows, :] = a[:, s * LANES:(s + 1) * LANES]
                    b_ref[d * LRU_SLABS + s, rows, :] = b[:, s * LANES:(s + 1) * LANES]
        return carry

    lax.fori_loop(0, G, gates_of_sequence, 0)

    def step(t, hs):
        out = []
        for d in range(2):
            tt = t if d == 0 else L - 1 - t
            rows = pl.ds(pl.multiple_of(tt * G, G), G)
            for s in range(LRU_SLABS):
                k = d * LRU_SLABS + s
                h = a_ref[k, rows, :] * hs[k] + b_ref[k, rows, :]
                h_ref[k, rows, :] = h
                out.append(h)
        return tuple(out)

    h0 = tuple(h0_ref[d, :, s * LANES:(s + 1) * LANES] for d in range(2) for s in range(LRU_SLABS))
    hs = lax.fori_loop(0, L, step, h0, unroll=2)
    for d in range(2):
        for s in range(LRU_SLABS):
            hl_ref[d, :, s * LANES:(s + 1) * LANES] = hs[d * LRU_SLABS + s]

    def write_sequence(g, carry):
        rows = pl.ds(g, L, stride=G)
        orow = pl.ds(pl.multiple_of(g * L, L), L)
        for s in range(LRU_SLABS):
            lanes = slice(s * LANES, (s + 1) * LANES)
            h = h_ref[s, rows, :] + h_ref[LRU_SLABS + s, rows, :]
            o_ref[orow, lanes] = (h * _silu(lg_ref[orow, lanes])).astype(BF16)
        return carry

    lax.fori_loop(0, G, write_sequence, 0)


def _lru(proj, conv_w, conv_b, wa, wi, ba, bi, lam, h0, layer, h0_layer, L, G):
    nseq = h0.shape[2]
    return pl.pallas_call(
        functools.partial(_lru_kernel, L=L, G=G),
        grid=(nseq // G,),
        in_specs=[
            pl.BlockSpec((None, G * L, LRU_W), lambda s: (S_LX_LG, s, 0)),
            pl.BlockSpec((None, G * L, LRU_W), lambda s: (S_LX_LG, s, 1)),
            _of_layer(layer, (4, LRU_W)),
            _of_layer(layer, (1, LRU_W)),
            _of_layer(layer, (2, LRU_BLOCKS, LRU_BW, LRU_BW)),
            _of_layer(layer, (2, LRU_BLOCKS, LRU_BW, LRU_BW)),
            _of_layer(layer, (2, LRU_W)),
            _of_layer(layer, (2, LRU_W)),
            _of_layer(layer, (2, LRU_W)),
            pl.BlockSpec((None, 2, G, LRU_W), lambda s: (h0_layer, 0, s, 0)),
        ],
        out_specs=[
            pl.BlockSpec((G * L, LRU_W), lambda s: (s, 0)),
            pl.BlockSpec((2, G, LRU_W), lambda s: (0, s, 0)),
        ],
        out_shape=[
            jax.ShapeDtypeStruct((nseq * L, LRU_W), BF16),
            jax.ShapeDtypeStruct((2, nseq, LRU_W), F32),
        ],
        scratch_shapes=[
            pltpu.VMEM((L, LRU_W), F32),
            pltpu.VMEM((2 * LRU_SLABS, L * G, LANES), F32),
            pltpu.VMEM((2 * LRU_SLABS, L * G, LANES), F32),
            pltpu.VMEM((2 * LRU_SLABS, L * G, LANES), F32),
            pltpu.VMEM((LRU_W, 4 * LRU_W), BF16),
        ],
        compiler_params=_params("arbitrary"),
        name=f"lru_{L}",
    )(proj, proj, conv_w, conv_b, wa, wi, ba, bi, lam, h0)


FILT_ROWS = 256


def _filt_kernel(z_ref, decay_ref, w1_ref, b1_ref, w2_ref, b2_ref, w3_ref, ft_ref, fb_ref, g_ref, taps_ref, *, L):
    i = pl.program_id(0)

    @pl.when(i == 0)
    def _():
        h = jnp.sin(_dot_hi(z_ref[...], w1_ref[...]) + b1_ref[...])
        h = jnp.sin(_dot_hi(h, w2_ref[...]) + b2_ref[...])
        filt = _dot_hi(h, w3_ref[...])
        decay = decay_ref[...]
        row = lax.broadcasted_iota(jnp.int32, (L, HY_W), 0)
        taps_ref[:, :HY_W] = (filt[:, :HY_W] * decay).astype(BF16)
        taps_ref[:, HY_W:] = jnp.where(row == 0, 0.0, filt[:, HY_W:] * decay).astype(BF16)

    taps = taps_ref[...]
    top = _dot(ft_ref[...], taps)
    bot = _dot(fb_ref[...], taps)
    first = (lax.broadcasted_iota(jnp.int32, (FILT_ROWS, HY_W), 0) + i * FILT_ROWS) == 0
    gt = top[:, :HY_W] + top[:, HY_W:]
    g_ref[0] = gt
    g_ref[1] = jnp.where(first, 0.0, bot[:, :HY_W] - bot[:, HY_W:])
    g_ref[2] = jnp.where(first, bot[:, :HY_W] + bot[:, HY_W:], gt)


def _filters(z, decay, w1, b1, w2, b2, w3, fwd_bf, layer, L):
    full = lambda shape: pl.BlockSpec(shape, lambda i: (0,) * len(shape))
    nblk = L // FILT_ROWS
    return pl.pallas_call(
        functools.partial(_filt_kernel, L=L),
        grid=(nblk,),
        in_specs=[
            full((L, HY_POS)), full((L, HY_W)),
            _of_layer(layer, (HY_POS, HY_HIDDEN)), _of_layer(layer, (1, HY_HIDDEN)),
            _of_layer(layer, (HY_HIDDEN, HY_HIDDEN)), _of_layer(layer, (1, HY_HIDDEN)),
            _of_layer(layer, (HY_HIDDEN, 2 * HY_W)),
            pl.BlockSpec((FILT_ROWS, L), lambda i: (i, 0)),
            pl.BlockSpec((FILT_ROWS, L), lambda i: (nblk + i, 0)),
        ],
        out_specs=pl.BlockSpec((3, FILT_ROWS, HY_W), lambda i: (0, i, 0)),
        out_shape=jax.ShapeDtypeStruct((3, L, HY_W), F32),
        scratch_shapes=[pltpu.VMEM((L, 2 * HY_W), BF16)],
        compiler_params=_params("arbitrary"),
        name=f"hyena_filters_{L}",
    )(z, decay, w1, b1, w2, b2, w3, fwd_bf, fwd_bf)


HY_PARAM_ROWS = 16


def _hyena_kernel(hv_ref, hx1_ref, hx0_ref, hg_ref, prm_ref, g_ref, f_ref, fi_ref, o_ref, *, L, ch, seqs):
    prm = prm_ref[...]
    rows = [slice(s * L, (s + 1) * L) for s in range(seqs)]

    t = lax.broadcasted_iota(jnp.int32, (L, ch), 0)

    def conv(ref, stream, s):
        x = ref[rows[s], :]
        prev = jnp.where(t == 0, 0.0, pltpu.roll(x, 1, 0))
        nxt = jnp.where(t == L - 1, 0.0, pltpu.roll(x, L - 1, 0))
        u = prm[9 + stream:10 + stream] + prm[3 * stream:3 * stream + 1] * prev
        u = u + prm[3 * stream + 1:3 * stream + 2] * x
        return u + prm[3 * stream + 2:3 * stream + 3] * nxt

    zs = [conv(hx1_ref, 1, s) * conv(hv_ref, 0, s) for s in range(seqs)]
    specs = [_dot(f_ref[...], z.astype(BF16)) for z in zs]
    gx = g_ref[1]
    prods = []
    for spec in specs:
        top, bot = spec[:L], spec[L:]
        yt = top * g_ref[0] - bot * gx
        yb = top * gx + bot * g_ref[2]
        prods.append(jnp.concatenate([yt, yb], axis=0).astype(BF16))
    ys = [_dot(fi_ref[...], p) for p in prods]
    for s in range(seqs):
        hy = conv(hx0_ref, 2, s) * (ys[s] + zs[s] * prm[12:13])
        o_ref[rows[s], :] = (hy * _silu(hg_ref[rows[s], :])).astype(BF16)


def _hyena(proj, prm, g, fwd, inv, layer, L, nseq, ch, seqs):
    nch = HY_W // ch
    full = lambda shape: pl.BlockSpec(shape, lambda s, j: (0,) * len(shape))
    col = lambda slab, off: pl.BlockSpec((None, seqs * L, ch), lambda s, j, slab=slab, off=off: (slab, s, off * nch + j))
    return pl.pallas_call(
        functools.partial(_hyena_kernel, L=L, ch=ch, seqs=seqs),
        grid=(nseq // seqs, nch),
        in_specs=[
            col(S_HV_HX1, 0), col(S_HV_HX1, 1), col(S_HX0_HG, 0), col(S_HX0_HG, 1),
            pl.BlockSpec((None, HY_PARAM_ROWS, ch), lambda s, j: (layer, 0, j)),
            pl.BlockSpec((3, L, ch), lambda s, j: (0, 0, j)),
            full((2 * L, L)), full((L, 2 * L)),
        ],
        out_specs=pl.BlockSpec((seqs * L, ch), lambda s, j: (s, j)),
        out_shape=jax.ShapeDtypeStruct((nseq * L, HY_W), BF16),
        compiler_params=_params("arbitrary", "arbitrary"),
        name=f"hyena_{L}",
    )(proj, proj, proj, proj, prm, g, fwd, inv)


def _hyena_params(conv_w, conv_b, d):
    rows = [conv_w[:, :, s * HY_W:(s + 1) * HY_W] for s in range(3)]
    rows += [conv_b[:, None, s * HY_W:(s + 1) * HY_W] for s in range(3)]
    rows += [d[:, None, :], jnp.zeros((DEPTH, HY_PARAM_ROWS - 13, HY_W), F32)]
    return jnp.concatenate(rows, axis=1)


OUT_TM = 512


def _outproj_kernel(x_ref, gate_ref, att_ref, lru_ref, hy_ref, w_ref, fg_ref, o_ref, *, final):
    for r0 in range(0, OUT_TM, OUT_TM // 2):
        rows = slice(r0, r0 + OUT_TM // 2)
        acc = _dot(att_ref[rows, :], w_ref[0:ATT_W, :])
        acc = acc + _dot(lru_ref[rows, :], w_ref[ATT_W:ATT_W + LRU_W, :])
        acc = acc + _dot(hy_ref[rows, :], w_ref[ATT_W + LRU_W:, :])
        y = x_ref[rows, :] + gate_ref[...] * acc
        if final:
            ms = jnp.mean(y * y, axis=-1, keepdims=True)
            y = (y * lax.rsqrt(ms + EPS)) * fg_ref[...]
        o_ref[rows, :] = y


def _outproj(x, mod4, att, lru, hy, w_out_bf, final_g, layer, final, seq_len):
    n_tok = x.shape[0]
    tm = OUT_TM
    cond = _cond_row(seq_len, tm)
    return pl.pallas_call(
        functools.partial(_outproj_kernel, final=final),
        grid=(n_tok // tm,),
        in_specs=[
            pl.BlockSpec((tm, D_MODEL), lambda i: (i, 0)),
            pl.BlockSpec((None, None, 1, D_MODEL), lambda i: (layer, cond(i), 0, 2)),
            pl.BlockSpec((tm, ATT_W), lambda i: (i, 0)),
            pl.BlockSpec((tm, LRU_W), lambda i: (i, 0)),
            pl.BlockSpec((tm, HY_W), lambda i: (i, 0)),
            pl.BlockSpec((None, D_MODEL, D_MODEL), lambda i: (layer, 0, 0), pipeline_mode=pl.Buffered(1)),
            pl.BlockSpec((1, D_MODEL), lambda i: (0, 0)),
        ],
        out_specs=pl.BlockSpec((tm, D_MODEL), lambda i: (i, 0)),
        out_shape=jax.ShapeDtypeStruct((n_tok, D_MODEL), F32),
        compiler_params=_params("arbitrary"),
        name="outproj",
    )(x, mod4, att, lru, hy, w_out_bf, final_g)


def kernel(x_prompt, x_sample, cache_k, cache_v, state_lru, c, c_ctx, norm_g, w_ada, b_ada, w_in, w_out, lam_q1, lam_k1, lam_q2, lam_k2, attn_subln_g, lru_conv_w, lru_conv_b, lru_wa, lru_ba, lru_wi, lru_bi, lru_lam, hy_conv_w, hy_conv_b, hy_w1, hy_b1, hy_w2, hy_b2, hy_w3, hy_d, final_g):
    cos, sin = _rope_tables()
    tables = {}
    for L in (SEQ, DEC_SEQ):
        fwd, inv = _dft_tables(L)
        z, decay = _hyena_tables(L)
        tables[L] = (fwd.astype(BF16), inv.astype(BF16), z, decay)

    cond = jnp.concatenate([c_ctx[None, :], c, jnp.zeros((N_COND - 1 - DEC_BATCH, D_MODEL), F32)], axis=0)
    mod4 = _ada(cond, w_ada, b_ada).reshape(DEPTH, N_COND, 1, 3 * D_MODEL)

    xc = x_prompt.reshape(T_CTX, D_MODEL)
    xs = x_sample.reshape(T_SMP, D_MODEL)
    ck = cache_k.reshape(DEC_BATCH, DEPTH, PAST_LEN * N_HEADS, HEAD_W)
    cv = cache_v.reshape(DEC_BATCH, DEPTH, PAST_LEN * N_HEADS, HEAD_W)
    h0_ctx = jnp.zeros((1, 2, BATCH, LRU_W), F32)
    h0_smp = jnp.transpose(state_lru, (1, 2, 0, 3))
    norm_g3 = norm_g[:, None, :]
    lamp = jnp.stack([lam_q1, lam_k1, lam_q2, lam_k2], axis=1)
    subln = attn_subln_g[:, None, :]
    w_out_bf = w_out.astype(BF16)
    lru_args = (lru_conv_w, lru_conv_b[:, None, :], lru_wa, lru_wi, lru_ba, lru_bi, lru_lam)
    hy_prm = _hyena_params(hy_conv_w, hy_conv_b, hy_d)
    filt_args = (hy_w1, hy_b1[:, None, :], hy_w2, hy_b2[:, None, :], hy_w3)
    fg = final_g[None, :]

    assert DEPTH == 2
    caches, hs = None, []
    for l in range(DEPTH):
        final = l == DEPTH - 1
        lam_init = 0.8 - 0.6 * math.exp(-0.3 * l)

        fwd, inv, z, decay = tables[DEC_SEQ]
        proj, w_in_bf = _inproj_w32(xs, norm_g3, mod4, w_in, l, DEC_SEQ)
        att = _attn_smp(proj, ck, cv, lamp, subln, cos, sin, l, lam_init)
        lru, _ = _lru(proj, *lru_args, h0_smp, l, l, DEC_SEQ, DEC_BATCH)
        g = _filters(z, decay, *filt_args, fwd, l, DEC_SEQ)
        hy = _hyena(proj, hy_prm, g, fwd, inv, l, DEC_SEQ, DEC_BATCH, HY_W // 2, 2)
        xs = _outproj(xs, mod4, att, lru, hy, w_out_bf, fg, l, final, DEC_SEQ)

        fwd, inv, z, decay = tables[SEQ]
        proj, *caches = _inproj_cache(xc, norm_g3, mod4, w_in_bf, l, caches)
        att = _attn_ctx(proj, lamp, subln, l, lam_init)
        lru, h_last = _lru(proj, *lru_args, h0_ctx, l, 0, SEQ, 8)
        hs.append(h_last)
        g = _filters(z, decay, *filt_args, fwd, l, SEQ)
        hy = _hyena(proj, hy_prm, g, fwd, inv, l, SEQ, BATCH, HY_W, 2)
        xc = _outproj(xc, mod4, att, lru, hy, w_out_bf, fg, l, final, None)

    y_prompt = xc.reshape(BATCH, SEQ, D_MODEL)
    y_sample = xs.reshape(DEC_BATCH, DEC_SEQ, D_MODEL)
    new_k, new_v = (a.reshape(BATCH, DEPTH, SEQ, N_HEADS, HEAD_W) for a in caches)
    new_state = jnp.transpose(jnp.stack(hs, axis=0), (2, 0, 1, 3))
    return (y_prompt, y_sample, new_k, new_v, new_state)
```

```python
import functools
import math

import numpy as np
import jax
import jax.numpy as jnp
from jax import lax
from jax.experimental import pallas as pl
from jax.experimental.pallas import tpu as pltpu

D_MODEL = 2048
BATCH = 32
SEQ = 256
DEPTH = 2
DEC_BATCH = 2
DEC_SEQ = 1024
PAST_LEN = 512
GRID_W = 64
ATT_W = 1024
LRU_W = 512
HY_W = 512
HEAD_DIM = 64
N_HEADS = 8
HEAD_W = 2 * HEAD_DIM
LRU_BLOCKS = 8
LRU_BW = LRU_W // LRU_BLOCKS
LRU_C = 8.0
HY_BANDS = 16
HY_POS = 1 + 2 * HY_BANDS
HY_HIDDEN = 64
HY_DECAY_FAST = 0.3
HY_DECAY_SLOW = 1.5
HY_DECAY_TARGET = 1e-2
ROPE_BASE = 10000.0
EPS = 1e-6
IN_W = 4 * ATT_W + 2 * LRU_W + 4 * HY_W

T_CTX = BATCH * SEQ
T_SMP = DEC_BATCH * DEC_SEQ
N_COND = 8

SLAB_W = 1024
N_SLABS = IN_W // SLAB_W
S_Q, S_K, S_V, S_AG, S_LX_LG, S_HV_HX1, S_HX0_HG = range(N_SLABS)

LANES = 128
F32 = jnp.float32
BF16 = jnp.bfloat16
VMEM_LIMIT = 58 * 1024 * 1024


LOG2E = 1.4426950408889634


def _sigmoid(x):
    return 1.0 / (1.0 + jnp.exp2(x * (-LOG2E)))


def _silu(x):
    return x * _sigmoid(x)


def _dot(a, b):
    return jnp.dot(a, b, preferred_element_type=F32)


def _dot_nt(a, b):
    return lax.dot_general(a, b, (((1,), (1,)), ((), ())), preferred_element_type=F32)


def _dot_hi(a, b):
    return jnp.dot(a, b, precision=lax.Precision.HIGHEST, preferred_element_type=F32)


def _params(*sem):
    return pltpu.CompilerParams(dimension_semantics=sem, vmem_limit_bytes=VMEM_LIMIT)


def _rope_tables():
    t = np.arange(DEC_SEQ)
    pos = np.stack([t // GRID_W, t % GRID_W], axis=1).astype(np.float64)
    nf = HEAD_DIM // 4
    inv = ROPE_BASE ** (-np.arange(nf, dtype=np.float64) / nf)
    lane = np.arange(HEAD_W)
    j = lane % HEAD_DIM
    axis = j // (HEAD_DIM // 2)
    f = j % nf
    upper = (j % (HEAD_DIM // 2)) >= nf
    ang = pos[:, axis] * inv[f][None, :]
    cos = np.cos(ang)
    sin = np.sin(ang) * np.where(upper, 1.0, -1.0)[None, :]
    return jnp.asarray(cos, F32), jnp.asarray(sin, F32)


def _dft_tables(L):
    n = 2 * L
    k = np.arange(L)[:, None]
    t = np.arange(L)[None, :]
    ang = 2.0 * np.pi * ((k * t) % n).astype(np.float64) / n
    fwd = np.concatenate([np.cos(ang), -np.sin(ang)], axis=0)
    fwd[L, :] = (-1.0) ** np.arange(L)
    wk = np.where(np.arange(L) == 0, 1.0, 2.0)[None, :]
    ang_t = ang.T
    inv = np.concatenate([wk * np.cos(ang_t), -2.0 * np.sin(ang_t)], axis=1) / n
    inv[:, L] = ((-1.0) ** np.arange(L)) / n
    return jnp.asarray(fwd, F32), jnp.asarray(inv, F32)


def _hyena_tables(L):
    pos = np.arange(L, dtype=np.float64)
    t = pos / float(max(L - 1, 1))
    bands = np.linspace(1e-4, HY_BANDS - 1, HY_BANDS)
    ang = (2.0 * math.pi / L) * pos[:, None] * bands[None, :]
    z = np.concatenate([t[:, None], np.cos(ang), np.sin(ang)], axis=-1)
    lo = abs(math.log(HY_DECAY_TARGET) / HY_DECAY_SLOW)
    hi = abs(math.log(HY_DECAY_TARGET) / HY_DECAY_FAST)
    deltas = np.linspace(lo, hi, HY_W)
    decay = np.exp(-t[:, None] * deltas[None, :])
    return jnp.asarray(z, F32), jnp.asarray(decay, F32)


def _ada_kernel(c_ref, w_ref, b_ref, o_ref):
    s = _silu(c_ref[...])
    o_ref[...] = _dot(s.astype(BF16), w_ref[...].astype(BF16)) + b_ref[...]


def _ada(cond, w_ada, b_ada):
    tn = 1024
    return pl.pallas_call(
        _ada_kernel,
        grid=(DEPTH, 3 * D_MODEL // tn),
        in_specs=[
            pl.BlockSpec((N_COND, D_MODEL), lambda l, j: (0, 0)),
            pl.BlockSpec((None, D_MODEL, tn), lambda l, j: (l, 0, j)),
            pl.BlockSpec((None, 1, tn), lambda l, j: (l, 0, j)),
        ],
        out_specs=pl.BlockSpec((None, N_COND, tn), lambda l, j: (l, 0, j)),
        out_shape=jax.ShapeDtypeStruct((DEPTH, N_COND, 3 * D_MODEL), F32),
        compiler_params=_params("arbitrary", "arbitrary"),
        name="ada",
    )(cond, w_ada, b_ada.reshape(DEPTH, 1, 3 * D_MODEL))


def _cond_row(seq_len, tm):
    if seq_len is None:
        return lambda i: 0
    return lambda i: 1 + i // (seq_len // tm)


IN_TM = 1024
NORM_ROWS = 32


def _norm_modulate(x_ref, g_ref, shift_ref, scale_ref, h_ref, h_row0=0):
    gain = g_ref[...] * (1.0 + scale_ref[...])
    sh = shift_ref[...]

    def body(r, carry):
        start = pl.multiple_of(r * NORM_ROWS, NORM_ROWS)
        x = x_ref[pl.ds(start, NORM_ROWS), :]
        ms = jnp.mean(x * x, axis=-1, keepdims=True)
        xn = x * lax.rsqrt(ms + EPS)
        h_ref[pl.ds(pl.multiple_of(h_row0 + start, NORM_ROWS), NORM_ROWS), :] = (xn * gain + sh).astype(BF16)
        return carry

    lax.fori_loop(0, IN_TM // NORM_ROWS, body, 0, unroll=4)


def _inproj_cache_kernel(*refs, first_layer):
    if first_layer:
        x_ref, g_ref, shift_ref, scale_ref, w_ref, o_ref, kc_ref, vc_ref, h_ref = refs
    else:
        x_ref, g_ref, shift_ref, scale_ref, w_ref, _, _, o_ref, kc_ref, vc_ref, h_ref = refs
    j = pl.program_id(1)

    @pl.when(j == 0)
    def _():
        _norm_modulate(x_ref, g_ref, shift_ref, scale_ref, h_ref)

    o_ref[...] = _dot(h_ref[...], w_ref[...])

    def scatter(dst_ref):
        for b in range(IN_TM // SEQ):
            for h in range(N_HEADS):
                dst_ref[b, pl.ds(h, SEQ, stride=N_HEADS), :] = o_ref[b * SEQ:(b + 1) * SEQ, h * HEAD_W:(h + 1) * HEAD_W]

    @pl.when(j == S_K)
    def _():
        scatter(kc_ref)

    @pl.when(j == S_V)
    def _():
        scatter(vc_ref)

    if first_layer:
        @pl.when(j == S_K + 1)
        def _():
            kc_ref[...] = jnp.zeros(kc_ref.shape, F32)

        @pl.when(j == S_V + 1)
        def _():
            vc_ref[...] = jnp.zeros(vc_ref.shape, F32)


def _of_layer(layer, shape):
    return pl.BlockSpec((None,) + tuple(shape), lambda *_: (layer,) + (0,) * len(shape))


def _inproj_specs(tm, layer, cond, w_spec):
    return [
        pl.BlockSpec((tm, D_MODEL), lambda i, j: (i, 0)),
        _of_layer(layer, (1, D_MODEL)),
        pl.BlockSpec((None, None, 1, D_MODEL), lambda i, j: (layer, cond(i), 0, 0)),
        pl.BlockSpec((None, None, 1, D_MODEL), lambda i, j: (layer, cond(i), 0, 1)),
        w_spec,
    ]


W32_K_CHUNK = 512


def _inproj_w32_kernel(x_ref, g_ref, shift_ref, scale_ref, w_ref, o_ref, wbf_ref, h_ref):
    row0 = pl.multiple_of(pl.program_id(1) * IN_TM, IN_TM)

    @pl.when(pl.program_id(0) == 0)
    def _():
        _norm_modulate(x_ref, g_ref, shift_ref, scale_ref, h_ref, row0)

    for k0 in range(0, D_MODEL, W32_K_CHUNK):
        w = w_ref[k0:k0 + W32_K_CHUNK, :].astype(BF16)
        wbf_ref[k0:k0 + W32_K_CHUNK, :] = w
        part = _dot(h_ref[pl.ds(row0, IN_TM), k0:k0 + W32_K_CHUNK], w)
        if k0 == 0:
            o_ref[...] = part
        else:
            o_ref[...] += part


def _inproj_w32(x, norm_g, mod4, w_in, layer, seq_len):
    n_tok = x.shape[0]
    tm = IN_TM
    n_tiles = n_tok // tm
    cond = _cond_row(seq_len, tm)
    x_row = lambda j, i: jnp.where(j == 0, i, n_tiles - 1)
    return pl.pallas_call(
        _inproj_w32_kernel,
        grid=(N_SLABS, n_tiles),
        in_specs=[
            pl.BlockSpec((tm, D_MODEL), lambda j, i: (x_row(j, i), 0), pipeline_mode=pl.Buffered(1)),
            _of_layer(layer, (1, D_MODEL)),
            pl.BlockSpec((None, None, 1, D_MODEL), lambda j, i: (layer, cond(i), 0, 0)),
            pl.BlockSpec((None, None, 1, D_MODEL), lambda j, i: (layer, cond(i), 0, 1)),
            pl.BlockSpec((None, D_MODEL, SLAB_W), lambda j, i: (layer, 0, j)),
        ],
        out_specs=[
            pl.BlockSpec((None, tm, SLAB_W), lambda j, i: (j, i, 0)),
            pl.BlockSpec((D_MODEL, SLAB_W), lambda j, i: (0, j)),
        ],
        out_shape=[jax.ShapeDtypeStruct((N_SLABS, n_tok, SLAB_W), F32), jax.ShapeDtypeStruct((D_MODEL, IN_W), BF16)],
        scratch_shapes=[pltpu.VMEM((n_tok, D_MODEL), BF16)],
        compiler_params=_params("arbitrary", "arbitrary"),
        name="inproj_w32",
    )(x, norm_g, mod4, mod4, w_in)


def _inproj_cache(x, norm_g, mod4, w_in_bf, layer, caches):
    tm = IN_TM
    nb = tm // SEQ
    first = caches is None
    cache_shape = jax.ShapeDtypeStruct((BATCH, DEPTH, SEQ * N_HEADS, HEAD_W), F32)
    if first:
        half = lambda slab: (lambda i, j: (i, jnp.where(j <= slab, layer, layer + 1), 0, 0))
    else:
        half = lambda slab: (lambda i, j: (i, layer, 0, 0))
    cache_spec = lambda slab: pl.BlockSpec((nb, None, SEQ * N_HEADS, HEAD_W), half(slab))
    w_spec = pl.BlockSpec((D_MODEL, SLAB_W), lambda i, j: (0, j))
    in_specs = _inproj_specs(tm, layer, _cond_row(None, tm), w_spec)
    args = (x, norm_g, mod4, mod4, w_in_bf)
    aliases = {}
    if not first:
        in_specs += [pl.BlockSpec(memory_space=pl.ANY)] * 2
        aliases = {len(args): 1, len(args) + 1: 2}
        args += tuple(caches)
    return pl.pallas_call(
        functools.partial(_inproj_cache_kernel, first_layer=first),
        grid=(T_CTX // tm, N_SLABS),
        in_specs=in_specs,
        out_specs=[pl.BlockSpec((None, tm, SLAB_W), lambda i, j: (j, i, 0)), cache_spec(S_K), cache_spec(S_V)],
        out_shape=[jax.ShapeDtypeStruct((N_SLABS, T_CTX, SLAB_W), F32), cache_shape, cache_shape],
        scratch_shapes=[pltpu.VMEM((tm, D_MODEL), BF16)],
        input_output_aliases=aliases,
        compiler_params=_params("arbitrary", "arbitrary"),
        name="inproj_cache",
    )(*args)


def _lam_value(lamp_ref, lam_init):
    lp = lamp_ref[...]
    t1 = jnp.sum(lp[0:1] * lp[1:2], axis=-1, keepdims=True)
    t2 = jnp.sum(lp[2:3] * lp[3:4], axis=-1, keepdims=True)
    return jnp.exp(t1) - jnp.exp(t2) + lam_init


def _diff_attend(items, lam, g, lam_init):
    m = items[0][0].shape[0]
    lane = lax.broadcasted_iota(jnp.int32, (m, HEAD_W), 1)
    ss = []
    for q, k_bf, _ in items:
        q1 = jnp.where(lane < HEAD_DIM, q, 0.0).astype(BF16)
        q2 = jnp.where(lane < HEAD_DIM, 0.0, q).astype(BF16)
        ss.append(_dot_nt(jnp.concatenate([q1, q2], axis=0), k_bf))
    es = [jnp.exp(s - jnp.max(s, axis=-1, keepdims=True)) for s in ss]
    ps = [e * (1.0 / jnp.sum(e, axis=-1, keepdims=True)) for e in es]
    ws = [(p[:m] - lam * p[m:]).astype(BF16) for p in ps]
    os_ = [_dot(w, v_bf) for w, (_, _, v_bf) in zip(ws, items)]
    outs = []
    for o in os_:
        ms = jnp.mean(o * o, axis=-1, keepdims=True)
        outs.append((o * lax.rsqrt(ms + EPS) * g) * (1.0 - lam_init))
    return outs


CTX_BATCHES = 2


CTX_ROWS = CTX_BATCHES * SEQ
CTX_SLABS = (S_Q, S_K, S_V, S_AG)
CTX_RING = 3
CTX_AHEAD = CTX_RING - 1


def _attn_ctx_kernel(lamp_ref, g_ref, proj_ref, o_ref, buf_ref, sem_ref, *, lam_init):
    s = pl.program_id(0)
    n = pl.num_programs(0)

    def copies(step, slot):
        rows = pl.ds(pl.multiple_of(step * CTX_ROWS, CTX_ROWS), CTX_ROWS)
        return [pltpu.make_async_copy(proj_ref.at[slab, rows, :], buf_ref.at[slot, k], sem_ref.at[slot, k])
                for k, slab in enumerate(CTX_SLABS)]

    @pl.when(s == 0)
    def _():
        for step in range(CTX_AHEAD):
            for c in copies(step, step):
                c.start()

    @pl.when(s + CTX_AHEAD < n)
    def _():
        for c in copies(s + CTX_AHEAD, lax.rem(s + CTX_AHEAD, CTX_RING)):
            c.start()

    slot = lax.rem(s, CTX_RING)
    for c in copies(s, slot):
        c.wait()

    where = [(slice(b * SEQ, (b + 1) * SEQ), slice(h * HEAD_W, (h + 1) * HEAD_W))
             for b in range(CTX_BATCHES) for h in range(N_HEADS)]

    def attend(k):
        lam = _lam_value(lamp_ref, lam_init)
        items = [(buf_ref[k, 0, r, c] * (HEAD_DIM ** -0.5), buf_ref[k, 1, r, c].astype(BF16),
                  buf_ref[k, 2, r, c].astype(BF16)) for r, c in where]
        for (r, c), o in zip(where, _diff_attend(items, lam, g_ref[...], lam_init)):
            o_ref[r, c] = (o * _silu(buf_ref[k, 3, r, c])).astype(BF16)

    for k in range(CTX_RING):
        @pl.when(slot == k)
        def _(k=k):
            attend(k)


def _attn_ctx(proj, lamp, subln_g, layer, lam_init):
    n_steps = BATCH // CTX_BATCHES
    assert n_steps >= CTX_AHEAD
    return pl.pallas_call(
        functools.partial(_attn_ctx_kernel, lam_init=lam_init),
        grid=(n_steps,),
        in_specs=[
            _of_layer(layer, (4, HEAD_DIM)),
            _of_layer(layer, (1, HEAD_W)),
            pl.BlockSpec(memory_space=pl.ANY),
        ],
        out_specs=pl.BlockSpec((CTX_ROWS, ATT_W), lambda b: (b, 0)),
        out_shape=jax.ShapeDtypeStruct((T_CTX, ATT_W), BF16),
        scratch_shapes=[
            pltpu.VMEM((CTX_RING, len(CTX_SLABS), CTX_ROWS, ATT_W), F32),
            pltpu.SemaphoreType.DMA((CTX_RING, len(CTX_SLABS))),
        ],
        compiler_params=_params("arbitrary"),
        name="attn_ctx",
    )(lamp, subln_g, proj)


Q_CHUNK = 512
Q_PART = 256
N_QC = DEC_SEQ // Q_CHUNK


def _rope(x, cos, sin):
    lane = lax.broadcasted_iota(jnp.int32, x.shape, 1)
    lower = (lane % (HEAD_DIM // 2)) < (HEAD_DIM // 4)
    nf = HEAD_DIM // 4
    partner = jnp.where(lower, pltpu.roll(x, HEAD_W - nf, 1), pltpu.roll(x, nf, 1))
    return x * cos + partner * sin


def _attn_smp_kernel(lamp_ref, g_ref, cosq_ref, sinq_ref, cos_ref, sin_ref, q_ref, k_ref, v_ref,
                     ck_ref, cv_ref, ag_ref, o_ref, ks_ref, vs_ref, *, lam_init):
    @pl.when(pl.program_id(2) == 0)
    def _():
        head_rows = pl.ds(pl.program_id(1), PAST_LEN, stride=N_HEADS)
        ks_ref[0:PAST_LEN, :] = ck_ref[head_rows, :].astype(BF16)
        vs_ref[0:PAST_LEN, :] = cv_ref[head_rows, :].astype(BF16)
        ks_ref[PAST_LEN:, :] = _rope(k_ref[...], cos_ref[...], sin_ref[...]).astype(BF16)
        vs_ref[PAST_LEN:, :] = v_ref[...].astype(BF16)

    lam = _lam_value(lamp_ref, lam_init)
    q = _rope(q_ref[...], cosq_ref[...], sinq_ref[...]) * (HEAD_DIM ** -0.5)
    k_bf, v_bf = ks_ref[...], vs_ref[...]
    parts = [slice(r, r + Q_PART) for r in range(0, Q_CHUNK, Q_PART)]
    items = [(q[p], k_bf, v_bf) for p in parts]
    for p, o in zip(parts, _diff_attend(items, lam, g_ref[...], lam_init)):
        o_ref[p, :] = (o * _silu(ag_ref[p, :])).astype(BF16)


def _attn_smp(proj, cache_k, cache_v, lamp, subln_g, cos, sin, layer, lam_init):
    full = lambda shape: pl.BlockSpec(shape, lambda b, h, c: (0, 0))
    return pl.pallas_call(
        functools.partial(_attn_smp_kernel, lam_init=lam_init),
        grid=(DEC_BATCH, N_HEADS, N_QC),
        in_specs=[
            _of_layer(layer, (4, HEAD_DIM)),
            _of_layer(layer, (1, HEAD_W)),
            pl.BlockSpec((Q_CHUNK, HEAD_W), lambda b, h, c: (c, 0)),
            pl.BlockSpec((Q_CHUNK, HEAD_W), lambda b, h, c: (c, 0)),
            full((DEC_SEQ, HEAD_W)),
            full((DEC_SEQ, HEAD_W)),
            pl.BlockSpec((None, Q_CHUNK, HEAD_W), lambda b, h, c: (S_Q, b * N_QC + c, h)),
            pl.BlockSpec((None, DEC_SEQ, HEAD_W), lambda b, h, c: (S_K, b, h)),
            pl.BlockSpec((None, DEC_SEQ, HEAD_W), lambda b, h, c: (S_V, b, h)),
            pl.BlockSpec((None, None, PAST_LEN * N_HEADS, HEAD_W), lambda b, h, c: (b, layer, 0, 0)),
            pl.BlockSpec((None, None, PAST_LEN * N_HEADS, HEAD_W), lambda b, h, c: (b, layer, 0, 0)),
            pl.BlockSpec((None, Q_CHUNK, HEAD_W), lambda b, h, c: (S_AG, b * N_QC + c, h)),
        ],
        out_specs=pl.BlockSpec((Q_CHUNK, HEAD_W), lambda b, h, c: (b * N_QC + c, h)),
        out_shape=jax.ShapeDtypeStruct((T_SMP, ATT_W), BF16),
        scratch_shapes=[
            pltpu.VMEM((PAST_LEN + DEC_SEQ, HEAD_W), BF16),
            pltpu.VMEM((PAST_LEN + DEC_SEQ, HEAD_W), BF16),
        ],
        compiler_params=_params("arbitrary", "arbitrary", "arbitrary"),
        name="attn_smp",
    )(lamp, subln_g, cos, sin, cos, sin, proj, proj, proj, cache_k, cache_v, proj)


GATE_ROWS = 256
LRU_SLABS = LRU_W // LANES


def _lru_kernel(x_ref, lg_ref, cw_ref, cb_ref, wa_ref, wi_ref, ba_ref, bi_ref, lam_ref, h0_ref, o_ref, hl_ref,
                xc_ref, a_ref, b_ref, h_ref, w_ref, *, L, G):
    @pl.when(pl.program_id(0) == 0)
    def _():
        w_ref[...] = jnp.zeros(w_ref.shape, BF16)
        for d in range(2):
            for gate, src in enumerate((wa_ref, wi_ref)):
                col0 = (2 * d + gate) * LRU_W
                for n in range(LRU_BLOCKS):
                    lo = n * LRU_BW
                    w_ref[lo:lo + LRU_BW, col0 + lo:col0 + lo + LRU_BW] = src[d, n].astype(BF16)

    nl = -lam_ref[...]
    neg_c = (-LRU_C * LOG2E) * (jnp.maximum(nl, 0.0) + jnp.log1p(jnp.exp(-jnp.abs(nl))))
    cw = cw_ref[...]
    cb = cb_ref[...]
    ba = ba_ref[...]
    bi = bi_ref[...]

    t = lax.broadcasted_iota(jnp.int32, (L, LRU_W), 0)

    def gates_of_sequence(g, carry):
        x = x_ref[pl.ds(pl.multiple_of(g * L, L), L), :]
        xc = cb + cw[0:1] * jnp.where(t < 2, 0.0, pltpu.roll(x, 2, 0))
        xc = xc + cw[1:2] * jnp.where(t < 1, 0.0, pltpu.roll(x, 1, 0))
        xc = xc + cw[2:3] * x
        xc_ref[...] = xc + cw[3:4] * jnp.where(t == L - 1, 0.0, pltpu.roll(x, L - 1, 0))
        for c in range(L // GATE_ROWS):
            r0 = c * GATE_ROWS
            xc = xc_ref[r0:r0 + GATE_ROWS, :]
            gates = _dot(xc.astype(BF16), w_ref[...])
            for d in range(2):
                r = 0.5 * jnp.tanh(0.5 * (gates[:, (2 * d) * LRU_W:(2 * d + 1) * LRU_W] + ba[d:d + 1])) + 0.5
                i = 0.5 * jnp.tanh(0.5 * (gates[:, (2 * d + 1) * LRU_W:(2 * d + 2) * LRU_W] + bi[d:d + 1])) + 0.5
                a = jnp.exp2(r * neg_c[d:d + 1])
                a2 = 1.0 - a * a
                b = jnp.where(a2 > 0.0, a2 * lax.rsqrt(a2), 0.0) * (i * xc)
                rows = pl.ds(r0 * G + g, GATE_ROWS, stride=G)
                for s in range(LRU_SLABS):
                    a_ref[d * LRU_SLABS + s, rows, :] = a[:, s * LANES:(s + 1) * LANES]
                    b_ref[d * LRU_SLABS + s, rows, :] = b[:, s * LANES:(s + 1) * LANES]
        return carry

    lax.fori_loop(0, G, gates_of_sequence, 0)

    def step(t, hs):
        out = []
        for d in range(2):
            tt = t if d == 0 else L - 1 - t
            rows = pl.ds(pl.multiple_of(tt * G, G), G)
            for s in range(LRU_SLABS):
                k = d * LRU_SLABS + s
                h = a_ref[k, rows, :] * hs[k] + b_ref[k, rows, :]
                h_ref[k, rows, :] = h
                out.append(h)
        return tuple(out)

    h0 = tuple(h0_ref[d, :, s * LANES:(s + 1) * LANES] for d in range(2) for s in range(LRU_SLABS))
    hs = lax.fori_loop(0, L, step, h0, unroll=2)
    for d in range(2):
        for s in range(LRU_SLABS):
            hl_ref[d, :, s * LANES:(s + 1) * LANES] = hs[d * LRU_SLABS + s]

    def write_sequence(g, carry):
        rows = pl.ds(g, L, stride=G)
        orow = pl.ds(pl.multiple_of(g * L, L), L)
        for s in range(LRU_SLABS):
            lanes = slice(s * LANES, (s + 1) * LANES)
            h = h_ref[s, rows, :] + h_ref[LRU_SLABS + s, rows, :]
            o_ref[orow, lanes] = (h * _silu(lg_ref[orow, lanes])).astype(BF16)
        return carry

    lax.fori_loop(0, G, write_sequence, 0)


def _lru(proj, conv_w, conv_b, wa, wi, ba, bi, lam, h0, layer, h0_layer, L, G):
    nseq = h0.shape[2]
    return pl.pallas_call(
        functools.partial(_lru_kernel, L=L, G=G),
        grid=(nseq // G,),
        in_specs=[
            pl.BlockSpec((None, G * L, LRU_W), lambda s: (S_LX_LG, s, 0)),
            pl.BlockSpec((None, G * L, LRU_W), lambda s: (S_LX_LG, s, 1)),
            _of_layer(layer, (4, LRU_W)),
            _of_layer(layer, (1, LRU_W)),
            _of_layer(layer, (2, LRU_BLOCKS, LRU_BW, LRU_BW)),
            _of_layer(layer, (2, LRU_BLOCKS, LRU_BW, LRU_BW)),
            _of_layer(layer, (2, LRU_W)),
            _of_layer(layer, (2, LRU_W)),
            _of_layer(layer, (2, LRU_W)),
            pl.BlockSpec((None, 2, G, LRU_W), lambda s: (h0_layer, 0, s, 0)),
        ],
        out_specs=[
            pl.BlockSpec((G * L, LRU_W), lambda s: (s, 0)),
            pl.BlockSpec((2, G, LRU_W), lambda s: (0, s, 0)),
        ],
        out_shape=[
            jax.ShapeDtypeStruct((nseq * L, LRU_W), BF16),
            jax.ShapeDtypeStruct((2, nseq, LRU_W), F32),
        ],
        scratch_shapes=[
            pltpu.VMEM((L, LRU_W), F32),
            pltpu.VMEM((2 * LRU_SLABS, L * G, LANES), F32),
            pltpu.VMEM((2 * LRU_SLABS, L * G, LANES), F32),
            pltpu.VMEM((2 * LRU_SLABS, L * G, LANES), F32),
            pltpu.VMEM((LRU_W, 4 * LRU_W), BF16),
        ],
        compiler_params=_params("arbitrary"),
        name=f"lru_{L}",
    )(proj, proj, conv_w, conv_b, wa, wi, ba, bi, lam, h0)


FILT_ROWS = 256


def _filt_kernel(z_ref, decay_ref, w1_ref, b1_ref, w2_ref, b2_ref, w3_ref, ft_ref, fb_ref, g_ref, taps_ref, *, L):
    i = pl.program_id(0)

    @pl.when(i == 0)
    def _():
        h = jnp.sin(_dot_hi(z_ref[...], w1_ref[...]) + b1_ref[...])
        h = jnp.sin(_dot_hi(h, w2_ref[...]) + b2_ref[...])
        filt = _dot_hi(h, w3_ref[...])
        decay = decay_ref[...]
        row = lax.broadcasted_iota(jnp.int32, (L, HY_W), 0)
        taps_ref[:, :HY_W] = (filt[:, :HY_W] * decay).astype(BF16)
        taps_ref[:, HY_W:] = jnp.where(row == 0, 0.0, filt[:, HY_W:] * decay).astype(BF16)

    taps = taps_ref[...]
    top = _dot(ft_ref[...], taps)
    bot = _dot(fb_ref[...], taps)
    first = (lax.broadcasted_iota(jnp.int32, (FILT_ROWS, HY_W), 0) + i * FILT_ROWS) == 0
    gt = top[:, :HY_W] + top[:, HY_W:]
    g_ref[0] = gt
    g_ref[1] = jnp.where(first, 0.0, bot[:, :HY_W] - bot[:, HY_W:])
    g_ref[2] = jnp.where(first, bot[:, :HY_W] + bot[:, HY_W:], gt)


def _filters(z, decay, w1, b1, w2, b2, w3, fwd_bf, layer, L):
    full = lambda shape: pl.BlockSpec(shape, lambda i: (0,) * len(shape))
    nblk = L // FILT_ROWS
    return pl.pallas_call(
        functools.partial(_filt_kernel, L=L),
        grid=(nblk,),
        in_specs=[
            full((L, HY_POS)), full((L, HY_W)),
            _of_layer(layer, (HY_POS, HY_HIDDEN)), _of_layer(layer, (1, HY_HIDDEN)),
            _of_layer(layer, (HY_HIDDEN, HY_HIDDEN)), _of_layer(layer, (1, HY_HIDDEN)),
            _of_layer(layer, (HY_HIDDEN, 2 * HY_W)),
            pl.BlockSpec((FILT_ROWS, L), lambda i: (i, 0)),
            pl.BlockSpec((FILT_ROWS, L), lambda i: (nblk + i, 0)),
        ],
        out_specs=pl.BlockSpec((3, FILT_ROWS, HY_W), lambda i: (0, i, 0)),
        out_shape=jax.ShapeDtypeStruct((3, L, HY_W), F32),
        scratch_shapes=[pltpu.VMEM((L, 2 * HY_W), BF16)],
        compiler_params=_params("arbitrary"),
        name=f"hyena_filters_{L}",
    )(z, decay, w1, b1, w2, b2, w3, fwd_bf, fwd_bf)


HY_PARAM_ROWS = 16


def _hyena_kernel(hv_ref, hx1_ref, hx0_ref, hg_ref, prm_ref, g_ref, f_ref, fi_ref, o_ref, *, L, ch, seqs):
    prm = prm_ref[...]
    rows = [slice(s * L, (s + 1) * L) for s in range(seqs)]

    t = lax.broadcasted_iota(jnp.int32, (L, ch), 0)

    def conv(ref, stream, s):
        x = ref[rows[s], :]
        prev = jnp.where(t == 0, 0.0, pltpu.roll(x, 1, 0))
        nxt = jnp.where(t == L - 1, 0.0, pltpu.roll(x, L - 1, 0))
        u = prm[9 + stream:10 + stream] + prm[3 * stream:3 * stream + 1] * prev
        u = u + prm[3 * stream + 1:3 * stream + 2] * x
        return u + prm[3 * stream + 2:3 * stream + 3] * nxt

    zs = [conv(hx1_ref, 1, s) * conv(hv_ref, 0, s) for s in range(seqs)]
    specs = [_dot(f_ref[...], z.astype(BF16)) for z in zs]
    gx = g_ref[1]
    prods = []
    for spec in specs:
        top, bot = spec[:L], spec[L:]
        yt = top * g_ref[0] - bot * gx
        yb = top * gx + bot * g_ref[2]
        prods.append(jnp.concatenate([yt, yb], axis=0).astype(BF16))
    ys = [_dot(fi_ref[...], p) for p in prods]
    for s in range(seqs):
        hy = conv(hx0_ref, 2, s) * (ys[s] + zs[s] * prm[12:13])
        o_ref[rows[s], :] = (hy * _silu(hg_ref[rows[s], :])).astype(BF16)


def _hyena(proj, prm, g, fwd, inv, layer, L, nseq, ch, seqs):
    nch = HY_W // ch
    full = lambda shape: pl.BlockSpec(shape, lambda s, j: (0,) * len(shape))
    col = lambda slab, off: pl.BlockSpec((None, seqs * L, ch), lambda s, j, slab=slab, off=off: (slab, s, off * nch + j))
    return pl.pallas_call(
        functools.partial(_hyena_kernel, L=L, ch=ch, seqs=seqs),
        grid=(nseq // seqs, nch),
        in_specs=[
            col(S_HV_HX1, 0), col(S_HV_HX1, 1), col(S_HX0_HG, 0), col(S_HX0_HG, 1),
            pl.BlockSpec((None, HY_PARAM_ROWS, ch), lambda s, j: (layer, 0, j)),
            pl.BlockSpec((3, L, ch), lambda s, j: (0, 0, j)),
            full((2 * L, L)), full((L, 2 * L)),
        ],
        out_specs=pl.BlockSpec((seqs * L, ch), lambda s, j: (s, j)),
        out_shape=jax.ShapeDtypeStruct((nseq * L, HY_W), BF16),
        compiler_params=_params("arbitrary", "arbitrary"),
        name=f"hyena_{L}",
    )(proj, proj, proj, proj, prm, g, fwd, inv)


def _hyena_params(conv_w, conv_b, d):
    rows = [conv_w[:, :, s * HY_W:(s + 1) * HY_W] for s in range(3)]
    rows += [conv_b[:, None, s * HY_W:(s + 1) * HY_W] for s in range(3)]
    rows += [d[:, None, :], jnp.zeros((DEPTH, HY_PARAM_ROWS - 13, HY_W), F32)]
    return jnp.concatenate(rows, axis=1)


OUT_TM = 512


def _outproj_kernel(x_ref, gate_ref, att_ref, lru_ref, hy_ref, w_ref, fg_ref, o_ref, *, final):
    for r0 in range(0, OUT_TM, OUT_TM // 2):
        rows = slice(r0, r0 + OUT_TM // 2)
        acc = _dot(att_ref[rows, :], w_ref[0:ATT_W, :])
        acc = acc + _dot(lru_ref[rows, :], w_ref[ATT_W:ATT_W + LRU_W, :])
        acc = acc + _dot(hy_ref[rows, :], w_ref[ATT_W + LRU_W:, :])
        y = x_ref[rows, :] + gate_ref[...] * acc
        if final:
            ms = jnp.mean(y * y, axis=-1, keepdims=True)
            y = (y * lax.rsqrt(ms + EPS)) * fg_ref[...]
        o_ref[rows, :] = y


def _outproj(x, mod4, att, lru, hy, w_out_bf, final_g, layer, final, seq_len):
    n_tok = x.shape[0]
    tm = OUT_TM
    cond = _cond_row(seq_len, tm)
    return pl.pallas_call(
        functools.partial(_outproj_kernel, final=final),
        grid=(n_tok // tm,),
        in_specs=[
            pl.BlockSpec((tm, D_MODEL), lambda i: (i, 0)),
            pl.BlockSpec((None, None, 1, D_MODEL), lambda i: (layer, cond(i), 0, 2)),
            pl.BlockSpec((tm, ATT_W), lambda i: (i, 0)),
            pl.BlockSpec((tm, LRU_W), lambda i: (i, 0)),
            pl.BlockSpec((tm, HY_W), lambda i: (i, 0)),
            pl.BlockSpec((None, D_MODEL, D_MODEL), lambda i: (layer, 0, 0), pipeline_mode=pl.Buffered(1)),
            pl.BlockSpec((1, D_MODEL), lambda i: (0, 0)),
        ],
        out_specs=pl.BlockSpec((tm, D_MODEL), lambda i: (i, 0)),
        out_shape=jax.ShapeDtypeStruct((n_tok, D_MODEL), F32),
        compiler_params=_params("arbitrary"),
        name="outproj",
    )(x, mod4, att, lru, hy, w_out_bf, final_g)


def kernel(x_prompt, x_sample, cache_k, cache_v, state_lru, c, c_ctx, norm_g, w_ada, b_ada, w_in, w_out, lam_q1, lam_k1, lam_q2, lam_k2, attn_subln_g, lru_conv_w, lru_conv_b, lru_wa, lru_ba, lru_wi, lru_bi, lru_lam, hy_conv_w, hy_conv_b, hy_w1, hy_b1, hy_w2, hy_b2, hy_w3, hy_d, final_g):
    cos, sin = _rope_tables()
    tables = {}
    for L in (SEQ, DEC_SEQ):
        fwd, inv = _dft_tables(L)
        z, decay = _hyena_tables(L)
        tables[L] = (fwd.astype(BF16), inv.astype(BF16), z, decay)

    cond = jnp.concatenate([c_ctx[None, :], c, jnp.zeros((N_COND - 1 - DEC_BATCH, D_MODEL), F32)], axis=0)
    mod4 = _ada(cond, w_ada, b_ada).reshape(DEPTH, N_COND, 1, 3 * D_MODEL)

    xc = x_prompt.reshape(T_CTX, D_MODEL)
    xs = x_sample.reshape(T_SMP, D_MODEL)
    ck = cache_k.reshape(DEC_BATCH, DEPTH, PAST_LEN * N_HEADS, HEAD_W)
    cv = cache_v.reshape(DEC_BATCH, DEPTH, PAST_LEN * N_HEADS, HEAD_W)
    h0_ctx = jnp.zeros((1, 2, BATCH, LRU_W), F32)
    h0_smp = jnp.transpose(state_lru, (1, 2, 0, 3))
    norm_g3 = norm_g[:, None, :]
    lamp = jnp.stack([lam_q1, lam_k1, lam_q2, lam_k2], axis=1)
    subln = attn_subln_g[:, None, :]
    w_out_bf = w_out.astype(BF16)
    lru_args = (lru_conv_w, lru_conv_b[:, None, :], lru_wa, lru_wi, lru_ba, lru_bi, lru_lam)
    hy_prm = _hyena_params(hy_conv_w, hy_conv_b, hy_d)
    filt_args = (hy_w1, hy_b1[:, None, :], hy_w2, hy_b2[:, None, :], hy_w3)
    fg = final_g[None, :]

    assert DEPTH == 2
    caches, hs = None, []
    for l in range(DEPTH):
        final = l == DEPTH - 1
        lam_init = 0.8 - 0.6 * math.exp(-0.3 * l)

        fwd, inv, z, decay = tables[DEC_SEQ]
        proj, w_in_bf = _inproj_w32(xs, norm_g3, mod4, w_in, l, DEC_SEQ)
        att = _attn_smp(proj, ck, cv, lamp, subln, cos, sin, l, lam_init)
        lru, _ = _lru(proj, *lru_args, h0_smp, l, l, DEC_SEQ, DEC_BATCH)
        g = _filters(z, decay, *filt_args, fwd, l, DEC_SEQ)
        hy = _hyena(proj, hy_prm, g, fwd, inv, l, DEC_SEQ, DEC_BATCH, HY_W // 2, 2)
        xs = _outproj(xs, mod4, att, lru, hy, w_out_bf, fg, l, final, DEC_SEQ)

        fwd, inv, z, decay = tables[SEQ]
        proj, *caches = _inproj_cache(xc, norm_g3, mod4, w_in_bf, l, caches)
        att = _attn_ctx(proj, lamp, subln, l, lam_init)
        lru, h_last = _lru(proj, *lru_args, h0_ctx, l, 0, SEQ, 8)
        hs.append(h_last)
        g = _filters(z, decay, *filt_args, fwd, l, SEQ)
        hy = _hyena(proj, hy_prm, g, fwd, inv, l, SEQ, BATCH, HY_W, 2)
        xc = _outproj(xc, mod4, att, lru, hy, w_out_bf, fg, l, final, None)

    y_prompt = xc.reshape(BATCH, SEQ, D_MODEL)
    y_sample = xs.reshape(DEC_BATCH, DEC_SEQ, D_MODEL)
    new_k, new_v = (a.reshape(BATCH, DEPTH, SEQ, N_HEADS, HEAD_W) for a in caches)
    new_state = jnp.transpose(jnp.stack(hs, axis=0), (2, 0, 1, 3))
    return (y_prompt, y_sample, new_k, new_v, new_state)
```

```python
import functools
import math

import numpy as np
import jax
import jax.numpy as jnp
from jax import lax
from jax.experimental import pallas as pl
from jax.experimental.pallas import tpu as pltpu

D_MODEL = 2048
BATCH = 32
SEQ = 256
DEPTH = 2
DEC_BATCH = 2
DEC_SEQ = 1024
PAST_LEN = 512
GRID_W = 64
ATT_W = 1024
LRU_W = 512
HY_W = 512
HEAD_DIM = 64
N_HEADS = 8
HEAD_W = 2 * HEAD_DIM
LRU_BLOCKS = 8
LRU_BW = LRU_W // LRU_BLOCKS
LRU_C = 8.0
HY_BANDS = 16
HY_POS = 1 + 2 * HY_BANDS
HY_HIDDEN = 64
HY_DECAY_FAST = 0.3
HY_DECAY_SLOW = 1.5
HY_DECAY_TARGET = 1e-2
ROPE_BASE = 10000.0
EPS = 1e-6
IN_W = 4 * ATT_W + 2 * LRU_W + 4 * HY_W

T_CTX = BATCH * SEQ
T_SMP = DEC_BATCH * DEC_SEQ
N_COND = 8

SLAB_W = 1024
N_SLABS = IN_W // SLAB_W
S_Q, S_K, S_V, S_AG, S_LX_LG, S_HV_HX1, S_HX0_HG = range(N_SLABS)

LANES = 128
F32 = jnp.float32
BF16 = jnp.bfloat16
VMEM_LIMIT = 58 * 1024 * 1024


LOG2E = 1.4426950408889634


def _sigmoid(x):
    return 1.0 / (1.0 + jnp.exp2(x * (-LOG2E)))


def _silu(x):
    return x * _sigmoid(x)


def _dot(a, b):
    return jnp.dot(a, b, preferred_element_type=F32)


def _dot_nt(a, b):
    return lax.dot_general(a, b, (((1,), (1,)), ((), ())), preferred_element_type=F32)


def _dot_hi(a, b):
    return jnp.dot(a, b, precision=lax.Precision.HIGHEST, preferred_element_type=F32)


def _params(*sem):
    return pltpu.CompilerParams(dimension_semantics=sem, vmem_limit_bytes=VMEM_LIMIT)


def _rope_tables():
    t = np.arange(DEC_SEQ)
    pos = np.stack([t // GRID_W, t % GRID_W], axis=1).astype(np.float64)
    nf = HEAD_DIM // 4
    inv = ROPE_BASE ** (-np.arange(nf, dtype=np.float64) / nf)
    lane = np.arange(HEAD_W)
    j = lane % HEAD_DIM
    axis = j // (HEAD_DIM // 2)
    f = j % nf
    upper = (j % (HEAD_DIM // 2)) >= nf
    ang = pos[:, axis] * inv[f][None, :]
    cos = np.cos(ang)
    sin = np.sin(ang) * np.where(upper, 1.0, -1.0)[None, :]
    return jnp.asarray(cos, F32), jnp.asarray(sin, F32)


def _dft_tables(L):
    n = 2 * L
    k = np.arange(L)[:, None]
    t = np.arange(L)[None, :]
    ang = 2.0 * np.pi * ((k * t) % n).astype(np.float64) / n
    fwd = np.concatenate([np.cos(ang), -np.sin(ang)], axis=0)
    fwd[L, :] = (-1.0) ** np.arange(L)
    wk = np.where(np.arange(L) == 0, 1.0, 2.0)[None, :]
    ang_t = ang.T
    inv = np.concatenate([wk * np.cos(ang_t), -2.0 * np.sin(ang_t)], axis=1) / n
    inv[:, L] = ((-1.0) ** np.arange(L)) / n
    return jnp.asarray(fwd, F32), jnp.asarray(inv, F32)


def _hyena_tables(L):
    pos = np.arange(L, dtype=np.float64)
    t = pos / float(max(L - 1, 1))
    bands = np.linspace(1e-4, HY_BANDS - 1, HY_BANDS)
    ang = (2.0 * math.pi / L) * pos[:, None] * bands[None, :]
    z = np.concatenate([t[:, None], np.cos(ang), np.sin(ang)], axis=-1)
    lo = abs(math.log(HY_DECAY_TARGET) / HY_DECAY_SLOW)
    hi = abs(math.log(HY_DECAY_TARGET) / HY_DECAY_FAST)
    deltas = np.linspace(lo, hi, HY_W)
    decay = np.exp(-t[:, None] * deltas[None, :])
    return jnp.asarray(z, F32), jnp.asarray(decay, F32)


def _ada_kernel(c_ref, w_ref, b_ref, o_ref):
    s = _silu(c_ref[...])
    o_ref[...] = _dot(s.astype(BF16), w_ref[...].astype(BF16)) + b_ref[...]


def _ada(cond, w_ada, b_ada):
    tn = 1024
    return pl.pallas_call(
        _ada_kernel,
        grid=(DEPTH, 3 * D_MODEL // tn),
        in_specs=[
            pl.BlockSpec((N_COND, D_MODEL), lambda l, j: (0, 0)),
            pl.BlockSpec((None, D_MODEL, tn), lambda l, j: (l, 0, j)),
            pl.BlockSpec((None, 1, tn), lambda l, j: (l, 0, j)),
        ],
        out_specs=pl.BlockSpec((None, N_COND, tn), lambda l, j: (l, 0, j)),
        out_shape=jax.ShapeDtypeStruct((DEPTH, N_COND, 3 * D_MODEL), F32),
        compiler_params=_params("arbitrary", "arbitrary"),
        name="ada",
    )(cond, w_ada, b_ada.reshape(DEPTH, 1, 3 * D_MODEL))


def _cond_row(seq_len, tm):
    if seq_len is None:
        return lambda i: 0
    return lambda i: 1 + i // (seq_len // tm)


IN_TM = 1024
NORM_ROWS = 32


def _norm_modulate(x_ref, g_ref, shift_ref, scale_ref, h_ref, h_row0=0):
    gain = g_ref[...] * (1.0 + scale_ref[...])
    sh = shift_ref[...]

    def body(r, carry):
        start = pl.multiple_of(r * NORM_ROWS, NORM_ROWS)
        x = x_ref[pl.ds(start, NORM_ROWS), :]
        ms = jnp.mean(x * x, axis=-1, keepdims=True)
        xn = x * lax.rsqrt(ms + EPS)
        h_ref[pl.ds(pl.multiple_of(h_row0 + start, NORM_ROWS), NORM_ROWS), :] = (xn * gain + sh).astype(BF16)
        return carry

    lax.fori_loop(0, IN_TM // NORM_ROWS, body, 0, unroll=4)


def _inproj_cache_kernel(*refs, first_layer):
    if first_layer:
        x_ref, g_ref, shift_ref, scale_ref, w_ref, o_ref, kc_ref, vc_ref, h_ref = refs
    else:
        x_ref, g_ref, shift_ref, scale_ref, w_ref, _, _, o_ref, kc_ref, vc_ref, h_ref = refs
    j = pl.program_id(1)

    @pl.when(j == 0)
    def _():
        _norm_modulate(x_ref, g_ref, shift_ref, scale_ref, h_ref)

    o_ref[...] = _dot(h_ref[...], w_ref[...])

    def scatter(dst_ref):
        for b in range(IN_TM // SEQ):
            for h in range(N_HEADS):
                dst_ref[b, pl.ds(h, SEQ, stride=N_HEADS), :] = o_ref[b * SEQ:(b + 1) * SEQ, h * HEAD_W:(h + 1) * HEAD_W]

    @pl.when(j == S_K)
    def _():
        scatter(kc_ref)

    @pl.when(j == S_V)
    def _():
        scatter(vc_ref)

    if first_layer:
        @pl.when(j == S_K + 1)
        def _():
            kc_ref[...] = jnp.zeros(kc_ref.shape, F32)

        @pl.when(j == S_V + 1)
        def _():
            vc_ref[...] = jnp.zeros(vc_ref.shape, F32)


def _of_layer(layer, shape):
    return pl.BlockSpec((None,) + tuple(shape), lambda *_: (layer,) + (0,) * len(shape))


def _inproj_specs(tm, layer, cond, w_spec):
    return [
        pl.BlockSpec((tm, D_MODEL), lambda i, j: (i, 0)),
        _of_layer(layer, (1, D_MODEL)),
        pl.BlockSpec((None, None, 1, D_MODEL), lambda i, j: (layer, cond(i), 0, 0)),
        pl.BlockSpec((None, None, 1, D_MODEL), lambda i, j: (layer, cond(i), 0, 1)),
        w_spec,
    ]


W32_K_CHUNK = 512


def _inproj_w32_kernel(x_ref, g_ref, shift_ref, scale_ref, w_ref, o_ref, wbf_ref, h_ref):
    row0 = pl.multiple_of(pl.program_id(1) * IN_TM, IN_TM)

    @pl.when(pl.program_id(0) == 0)
    def _():
        _norm_modulate(x_ref, g_ref, shift_ref, scale_ref, h_ref, row0)

    for k0 in range(0, D_MODEL, W32_K_CHUNK):
        w = w_ref[k0:k0 + W32_K_CHUNK, :].astype(BF16)
        wbf_ref[k0:k0 + W32_K_CHUNK, :] = w
        part = _dot(h_ref[pl.ds(row0, IN_TM), k0:k0 + W32_K_CHUNK], w)
        if k0 == 0:
            o_ref[...] = part
        else:
            o_ref[...] += part


def _inproj_w32(x, norm_g, mod4, w_in, layer, seq_len):
    n_tok = x.shape[0]
    tm = IN_TM
    n_tiles = n_tok // tm
    cond = _cond_row(seq_len, tm)
    x_row = lambda j, i: jnp.where(j == 0, i, n_tiles - 1)
    return pl.pallas_call(
        _inproj_w32_kernel,
        grid=(N_SLABS, n_tiles),
        in_specs=[
            pl.BlockSpec((tm, D_MODEL), lambda j, i: (x_row(j, i), 0), pipeline_mode=pl.Buffered(1)),
            _of_layer(layer, (1, D_MODEL)),
            pl.BlockSpec((None, None, 1, D_MODEL), lambda j, i: (layer, cond(i), 0, 0)),
            pl.BlockSpec((None, None, 1, D_MODEL), lambda j, i: (layer, cond(i), 0, 1)),
            pl.BlockSpec((None, D_MODEL, SLAB_W), lambda j, i: (layer, 0, j)),
        ],
        out_specs=[
            pl.BlockSpec((None, tm, SLAB_W), lambda j, i: (j, i, 0)),
            pl.BlockSpec((D_MODEL, SLAB_W), lambda j, i: (0, j)),
        ],
        out_shape=[jax.ShapeDtypeStruct((N_SLABS, n_tok, SLAB_W), F32), jax.ShapeDtypeStruct((D_MODEL, IN_W), BF16)],
        scratch_shapes=[pltpu.VMEM((n_tok, D_MODEL), BF16)],
        compiler_params=_params("arbitrary", "arbitrary"),
        name="inproj_w32",
    )(x, norm_g, mod4, mod4, w_in)


def _inproj_cache(x, norm_g, mod4, w_in_bf, layer, caches):
    tm = IN_TM
    nb = tm // SEQ
    first = caches is None
    cache_shape = jax.ShapeDtypeStruct((BATCH, DEPTH, SEQ * N_HEADS, HEAD_W), F32)
    if first:
        half = lambda slab: (lambda i, j: (i, jnp.where(j <= slab, layer, layer + 1), 0, 0))
    else:
        half = lambda slab: (lambda i, j: (i, layer, 0, 0))
    cache_spec = lambda slab: pl.BlockSpec((nb, None, SEQ * N_HEADS, HEAD_W), half(slab))
    w_spec = pl.BlockSpec((D_MODEL, SLAB_W), lambda i, j: (0, j))
    in_specs = _inproj_specs(tm, layer, _cond_row(None, tm), w_spec)
    args = (x, norm_g, mod4, mod4, w_in_bf)
    aliases = {}
    if not first:
        in_specs += [pl.BlockSpec(memory_space=pl.ANY)] * 2
        aliases = {len(args): 1, len(args) + 1: 2}
        args += tuple(caches)
    return pl.pallas_call(
        functools.partial(_inproj_cache_kernel, first_layer=first),
        grid=(T_CTX // tm, N_SLABS),
        in_specs=in_specs,
        out_specs=[pl.BlockSpec((None, tm, SLAB_W), lambda i, j: (j, i, 0)), cache_spec(S_K), cache_spec(S_V)],
        out_shape=[jax.ShapeDtypeStruct((N_SLABS, T_CTX, SLAB_W), F32), cache_shape, cache_shape],
        scratch_shapes=[pltpu.VMEM((tm, D_MODEL), BF16)],
        input_output_aliases=aliases,
        compiler_params=_params("arbitrary", "arbitrary"),
        name="inproj_cache",
    )(*args)


def _lam_value(lamp_ref, lam_init):
    lp = lamp_ref[...]
    t1 = jnp.sum(lp[0:1] * lp[1:2], axis=-1, keepdims=True)
    t2 = jnp.sum(lp[2:3] * lp[3:4], axis=-1, keepdims=True)
    return jnp.exp(t1) - jnp.exp(t2) + lam_init


def _diff_attend(items, lam, g, lam_init):
    m = items[0][0].shape[0]
    lane = lax.broadcasted_iota(jnp.int32, (m, HEAD_W), 1)
    ss = []
    for q, k_bf, _ in items:
        q1 = jnp.where(lane < HEAD_DIM, q, 0.0).astype(BF16)
        q2 = jnp.where(lane < HEAD_DIM, 0.0, q).astype(BF16)
        ss.append(_dot_nt(jnp.concatenate([q1, q2], axis=0), k_bf))
    es = [jnp.exp(s - jnp.max(s, axis=-1, keepdims=True)) for s in ss]
    ps = [e * (1.0 / jnp.sum(e, axis=-1, keepdims=True)) for e in es]
    ws = [(p[:m] - lam * p[m:]).astype(BF16) for p in ps]
    os_ = [_dot(w, v_bf) for w, (_, _, v_bf) in zip(ws, items)]
    outs = []
    for o in os_:
        ms = jnp.mean(o * o, axis=-1, keepdims=True)
        outs.append((o * lax.rsqrt(ms + EPS) * g) * (1.0 - lam_init))
    return outs


CTX_BATCHES = 2


def _attn_ctx_kernel(lamp_ref, g_ref, q_ref, k_ref, v_ref, ag_ref, o_ref, *, lam_init):
    lam = _lam_value(lamp_ref, lam_init)
    where = [(slice(b * SEQ, (b + 1) * SEQ), slice(h * HEAD_W, (h + 1) * HEAD_W))
             for b in range(CTX_BATCHES) for h in range(N_HEADS)]
    items = [(q_ref[r, c] * (HEAD_DIM ** -0.5), k_ref[r, c].astype(BF16), v_ref[r, c].astype(BF16)) for r, c in where]
    for (r, c), o in zip(where, _diff_attend(items, lam, g_ref[...], lam_init)):
        o_ref[r, c] = (o * _silu(ag_ref[r, c])).astype(BF16)


def _attn_ctx(proj, lamp, subln_g, layer, lam_init):
    rows = CTX_BATCHES * SEQ
    slab = lambda s: pl.BlockSpec((None, rows, ATT_W), lambda b, s=s: (s, b, 0))
    return pl.pallas_call(
        functools.partial(_attn_ctx_kernel, lam_init=lam_init),
        grid=(BATCH // CTX_BATCHES,),
        in_specs=[
            _of_layer(layer, (4, HEAD_DIM)),
            _of_layer(layer, (1, HEAD_W)),
            slab(S_Q), slab(S_K), slab(S_V), slab(S_AG),
        ],
        out_specs=pl.BlockSpec((rows, ATT_W), lambda b: (b, 0)),
        out_shape=jax.ShapeDtypeStruct((T_CTX, ATT_W), BF16),
        compiler_params=_params("arbitrary"),
        name="attn_ctx",
    )(lamp, subln_g, proj, proj, proj, proj)


Q_CHUNK = 512
Q_PART = 256
N_QC = DEC_SEQ // Q_CHUNK


def _rope(x, cos, sin):
    lane = lax.broadcasted_iota(jnp.int32, x.shape, 1)
    lower = (lane % (HEAD_DIM // 2)) < (HEAD_DIM // 4)
    nf = HEAD_DIM // 4
    partner = jnp.where(lower, pltpu.roll(x, HEAD_W - nf, 1), pltpu.roll(x, nf, 1))
    return x * cos + partner * sin


def _attn_smp_kernel(lamp_ref, g_ref, cosq_ref, sinq_ref, cos_ref, sin_ref, q_ref, k_ref, v_ref,
                     ck_ref, cv_ref, ag_ref, o_ref, ks_ref, vs_ref, *, lam_init):
    @pl.when(pl.program_id(2) == 0)
    def _():
        head_rows = pl.ds(pl.program_id(1), PAST_LEN, stride=N_HEADS)
        ks_ref[0:PAST_LEN, :] = ck_ref[head_rows, :].astype(BF16)
        vs_ref[0:PAST_LEN, :] = cv_ref[head_rows, :].astype(BF16)
        ks_ref[PAST_LEN:, :] = _rope(k_ref[...], cos_ref[...], sin_ref[...]).astype(BF16)
        vs_ref[PAST_LEN:, :] = v_ref[...].astype(BF16)

    lam = _lam_value(lamp_ref, lam_init)
    q = _rope(q_ref[...], cosq_ref[...], sinq_ref[...]) * (HEAD_DIM ** -0.5)
    k_bf, v_bf = ks_ref[...], vs_ref[...]
    parts = [slice(r, r + Q_PART) for r in range(0, Q_CHUNK, Q_PART)]
    items = [(q[p], k_bf, v_bf) for p in parts]
    for p, o in zip(parts, _diff_attend(items, lam, g_ref[...], lam_init)):
        o_ref[p, :] = (o * _silu(ag_ref[p, :])).astype(BF16)


def _attn_smp(proj, cache_k, cache_v, lamp, subln_g, cos, sin, layer, lam_init):
    full = lambda shape: pl.BlockSpec(shape, lambda b, h, c: (0, 0))
    return pl.pallas_call(
        functools.partial(_attn_smp_kernel, lam_init=lam_init),
        grid=(DEC_BATCH, N_HEADS, N_QC),
        in_specs=[
            _of_layer(layer, (4, HEAD_DIM)),
            _of_layer(layer, (1, HEAD_W)),
            pl.BlockSpec((Q_CHUNK, HEAD_W), lambda b, h, c: (c, 0)),
            pl.BlockSpec((Q_CHUNK, HEAD_W), lambda b, h, c: (c, 0)),
            full((DEC_SEQ, HEAD_W)),
            full((DEC_SEQ, HEAD_W)),
            pl.BlockSpec((None, Q_CHUNK, HEAD_W), lambda b, h, c: (S_Q, b * N_QC + c, h)),
            pl.BlockSpec((None, DEC_SEQ, HEAD_W), lambda b, h, c: (S_K, b, h)),
            pl.BlockSpec((None, DEC_SEQ, HEAD_W), lambda b, h, c: (S_V, b, h)),
            pl.BlockSpec((None, None, PAST_LEN * N_HEADS, HEAD_W), lambda b, h, c: (b, layer, 0, 0)),
            pl.BlockSpec((None, None, PAST_LEN * N_HEADS, HEAD_W), lambda b, h, c: (b, layer, 0, 0)),
            pl.BlockSpec((None, Q_CHUNK, HEAD_W), lambda b, h, c: (S_AG, b * N_QC + c, h)),
        ],
        out_specs=pl.BlockSpec((Q_CHUNK, HEAD_W), lambda b, h, c: (b * N_QC + c, h)),
        out_shape=jax.ShapeDtypeStruct((T_SMP, ATT_W), BF16),
        scratch_shapes=[
            pltpu.VMEM((PAST_LEN + DEC_SEQ, HEAD_W), BF16),
            pltpu.VMEM((PAST_LEN + DEC_SEQ, HEAD_W), BF16),
        ],
        compiler_params=_params("arbitrary", "arbitrary", "arbitrary"),
        name="attn_smp",
    )(lamp, subln_g, cos, sin, cos, sin, proj, proj, proj, cache_k, cache_v, proj)


GATE_ROWS = 256
LRU_SLABS = LRU_W // LANES


def _lru_kernel(x_ref, lg_ref, cw_ref, cb_ref, wa_ref, wi_ref, ba_ref, bi_ref, lam_ref, h0_ref, o_ref, hl_ref,
                xc_ref, a_ref, b_ref, h_ref, w_ref, *, L, G):
    @pl.when(pl.program_id(0) == 0)
    def _():
        w_ref[...] = jnp.zeros(w_ref.shape, BF16)
        for d in range(2):
            for gate, src in enumerate((wa_ref, wi_ref)):
                col0 = (2 * d + gate) * LRU_W
                for n in range(LRU_BLOCKS):
                    lo = n * LRU_BW
                    w_ref[lo:lo + LRU_BW, col0 + lo:col0 + lo + LRU_BW] = src[d, n].astype(BF16)

    nl = -lam_ref[...]
    neg_c = (-LRU_C * LOG2E) * (jnp.maximum(nl, 0.0) + jnp.log1p(jnp.exp(-jnp.abs(nl))))
    half_c = 0.5 * neg_c
    cw = cw_ref[...]
    cb = cb_ref[...]
    ba = ba_ref[...]
    bi = bi_ref[...]

    t = lax.broadcasted_iota(jnp.int32, (L, LRU_W), 0)

    def gates_of_sequence(g, carry):
        x = x_ref[pl.ds(pl.multiple_of(g * L, L), L), :]
        xc = cb + cw[0:1] * jnp.where(t < 2, 0.0, pltpu.roll(x, 2, 0))
        xc = xc + cw[1:2] * jnp.where(t < 1, 0.0, pltpu.roll(x, 1, 0))
        xc = xc + cw[2:3] * x
        xc_ref[...] = xc + cw[3:4] * jnp.where(t == L - 1, 0.0, pltpu.roll(x, L - 1, 0))
        for c in range(L // GATE_ROWS):
            r0 = c * GATE_ROWS
            xc = xc_ref[r0:r0 + GATE_ROWS, :]
            half_xc = 0.5 * xc
            gates = _dot(xc.astype(BF16), w_ref[...])
            for d in range(2):
                ta = jnp.tanh(0.5 * (gates[:, (2 * d) * LRU_W:(2 * d + 1) * LRU_W] + ba[d:d + 1]))
                ti = jnp.tanh(0.5 * (gates[:, (2 * d + 1) * LRU_W:(2 * d + 2) * LRU_W] + bi[d:d + 1]))
                a = jnp.exp2(ta * half_c[d:d + 1] + half_c[d:d + 1])
                a2 = 1.0 - a * a
                b = jnp.where(a2 > 0.0, a2 * lax.rsqrt(a2), 0.0) * ((ti + 1.0) * half_xc)
                rows = pl.ds(r0 * G + g, GATE_ROWS, stride=G)
                for s in range(LRU_SLABS):
                    a_ref[d * LRU_SLABS + s, rows, :] = a[:, s * LANES:(s + 1) * LANES]
                    b_ref[d * LRU_SLABS + s, rows, :] = b[:, s * LANES:(s + 1) * LANES]
        return carry

    lax.fori_loop(0, G, gates_of_sequence, 0)

    def step(t, hs):
        out = []
        for d in range(2):
            tt = t if d == 0 else L - 1 - t
            rows = pl.ds(pl.multiple_of(tt * G, G), G)
            for s in range(LRU_SLABS):
                k = d * LRU_SLABS + s
                h = a_ref[k, rows, :] * hs[k] + b_ref[k, rows, :]
                h_ref[k, rows, :] = h
                out.append(h)
        return tuple(out)

    h0 = tuple(h0_ref[d, :, s * LANES:(s + 1) * LANES] for d in range(2) for s in range(LRU_SLABS))
    hs = lax.fori_loop(0, L, step, h0, unroll=2)
    for d in range(2):
        for s in range(LRU_SLABS):
            hl_ref[d, :, s * LANES:(s + 1) * LANES] = hs[d * LRU_SLABS + s]

    def write_sequence(g, carry):
        rows = pl.ds(g, L, stride=G)
        orow = pl.ds(pl.multiple_of(g * L, L), L)
        for s in range(LRU_SLABS):
            lanes = slice(s * LANES, (s + 1) * LANES)
            h = h_ref[s, rows, :] + h_ref[LRU_SLABS + s, rows, :]
            o_ref[orow, lanes] = (h * _silu(lg_ref[orow, lanes])).astype(BF16)
        return carry

    lax.fori_loop(0, G, write_sequence, 0)


def _lru(proj, conv_w, conv_b, wa, wi, ba, bi, lam, h0, layer, h0_layer, L, G):
    nseq = h0.shape[2]
    return pl.pallas_call(
        functools.partial(_lru_kernel, L=L, G=G),
        grid=(nseq // G,),
        in_specs=[
            pl.BlockSpec((None, G * L, LRU_W), lambda s: (S_LX_LG, s, 0)),
            pl.BlockSpec((None, G * L, LRU_W), lambda s: (S_LX_LG, s, 1)),
            _of_layer(layer, (4, LRU_W)),
            _of_layer(layer, (1, LRU_W)),
            _of_layer(layer, (2, LRU_BLOCKS, LRU_BW, LRU_BW)),
            _of_layer(layer, (2, LRU_BLOCKS, LRU_BW, LRU_BW)),
            _of_layer(layer, (2, LRU_W)),
            _of_layer(layer, (2, LRU_W)),
            _of_layer(layer, (2, LRU_W)),
            pl.BlockSpec((None, 2, G, LRU_W), lambda s: (h0_layer, 0, s, 0)),
        ],
        out_specs=[
            pl.BlockSpec((G * L, LRU_W), lambda s: (s, 0)),
            pl.BlockSpec((2, G, LRU_W), lambda s: (0, s, 0)),
        ],
        out_shape=[
            jax.ShapeDtypeStruct((nseq * L, LRU_W), BF16),
            jax.ShapeDtypeStruct((2, nseq, LRU_W), F32),
        ],
        scratch_shapes=[
            pltpu.VMEM((L, LRU_W), F32),
            pltpu.VMEM((2 * LRU_SLABS, L * G, LANES), F32),
            pltpu.VMEM((2 * LRU_SLABS, L * G, LANES), F32),
            pltpu.VMEM((2 * LRU_SLABS, L * G, LANES), F32),
            pltpu.VMEM((LRU_W, 4 * LRU_W), BF16),
        ],
        compiler_params=_params("arbitrary"),
        name=f"lru_{L}",
    )(proj, proj, conv_w, conv_b, wa, wi, ba, bi, lam, h0)


FILT_ROWS = 256


def _filt_kernel(z_ref, decay_ref, w1_ref, b1_ref, w2_ref, b2_ref, w3_ref, ft_ref, fb_ref, g_ref, taps_ref, *, L):
    i = pl.program_id(0)

    @pl.when(i == 0)
    def _():
        h = jnp.sin(_dot_hi(z_ref[...], w1_ref[...]) + b1_ref[...])
        h = jnp.sin(_dot_hi(h, w2_ref[...]) + b2_ref[...])
        filt = _dot_hi(h, w3_ref[...])
        decay = decay_ref[...]
        row = lax.broadcasted_iota(jnp.int32, (L, HY_W), 0)
        taps_ref[:, :HY_W] = (filt[:, :HY_W] * decay).astype(BF16)
        taps_ref[:, HY_W:] = jnp.where(row == 0, 0.0, filt[:, HY_W:] * decay).astype(BF16)

    taps = taps_ref[...]
    top = _dot(ft_ref[...], taps)
    bot = _dot(fb_ref[...], taps)
    first = (lax.broadcasted_iota(jnp.int32, (FILT_ROWS, HY_W), 0) + i * FILT_ROWS) == 0
    gt = top[:, :HY_W] + top[:, HY_W:]
    g_ref[0] = gt
    g_ref[1] = jnp.where(first, 0.0, bot[:, :HY_W] - bot[:, HY_W:])
    g_ref[2] = jnp.where(first, bot[:, :HY_W] + bot[:, HY_W:], gt)


def _filters(z, decay, w1, b1, w2, b2, w3, fwd_bf, layer, L):
    full = lambda shape: pl.BlockSpec(shape, lambda i: (0,) * len(shape))
    nblk = L // FILT_ROWS
    return pl.pallas_call(
        functools.partial(_filt_kernel, L=L),
        grid=(nblk,),
        in_specs=[
            full((L, HY_POS)), full((L, HY_W)),
            _of_layer(layer, (HY_POS, HY_HIDDEN)), _of_layer(layer, (1, HY_HIDDEN)),
            _of_layer(layer, (HY_HIDDEN, HY_HIDDEN)), _of_layer(layer, (1, HY_HIDDEN)),
            _of_layer(layer, (HY_HIDDEN, 2 * HY_W)),
            pl.BlockSpec((FILT_ROWS, L), lambda i: (i, 0)),
            pl.BlockSpec((FILT_ROWS, L), lambda i: (nblk + i, 0)),
        ],
        out_specs=pl.BlockSpec((3, FILT_ROWS, HY_W), lambda i: (0, i, 0)),
        out_shape=jax.ShapeDtypeStruct((3, L, HY_W), F32),
        scratch_shapes=[pltpu.VMEM((L, 2 * HY_W), BF16)],
        compiler_params=_params("arbitrary"),
        name=f"hyena_filters_{L}",
    )(z, decay, w1, b1, w2, b2, w3, fwd_bf, fwd_bf)


HY_PARAM_ROWS = 16


def _hyena_kernel(hv_ref, hx1_ref, hx0_ref, hg_ref, prm_ref, g_ref, f_ref, fi_ref, o_ref, *, L, ch, seqs):
    prm = prm_ref[...]
    rows = [slice(s * L, (s + 1) * L) for s in range(seqs)]

    t = lax.broadcasted_iota(jnp.int32, (L, ch), 0)

    def conv(ref, stream, s):
        x = ref[rows[s], :]
        prev = jnp.where(t == 0, 0.0, pltpu.roll(x, 1, 0))
        nxt = jnp.where(t == L - 1, 0.0, pltpu.roll(x, L - 1, 0))
        u = prm[9 + stream:10 + stream] + prm[3 * stream:3 * stream + 1] * prev
        u = u + prm[3 * stream + 1:3 * stream + 2] * x
        return u + prm[3 * stream + 2:3 * stream + 3] * nxt

    zs = [conv(hx1_ref, 1, s) * conv(hv_ref, 0, s) for s in range(seqs)]
    specs = [_dot(f_ref[...], z.astype(BF16)) for z in zs]
    gx = g_ref[1]
    prods = []
    for spec in specs:
        top, bot = spec[:L], spec[L:]
        yt = top * g_ref[0] - bot * gx
        yb = top * gx + bot * g_ref[2]
        prods.append(jnp.concatenate([yt, yb], axis=0).astype(BF16))
    ys = [_dot(fi_ref[...], p) for p in prods]
    for s in range(seqs):
        hy = conv(hx0_ref, 2, s) * (ys[s] + zs[s] * prm[12:13])
        o_ref[rows[s], :] = (hy * _silu(hg_ref[rows[s], :])).astype(BF16)


def _hyena(proj, prm, g, fwd, inv, layer, L, nseq, ch, seqs):
    nch = HY_W // ch
    full = lambda shape: pl.BlockSpec(shape, lambda s, j: (0,) * len(shape))
    col = lambda slab, off: pl.BlockSpec((None, seqs * L, ch), lambda s, j, slab=slab, off=off: (slab, s, off * nch + j))
    return pl.pallas_call(
        functools.partial(_hyena_kernel, L=L, ch=ch, seqs=seqs),
        grid=(nseq // seqs, nch),
        in_specs=[
            col(S_HV_HX1, 0), col(S_HV_HX1, 1), col(S_HX0_HG, 0), col(S_HX0_HG, 1),
            pl.BlockSpec((None, HY_PARAM_ROWS, ch), lambda s, j: (layer, 0, j)),
            pl.BlockSpec((3, L, ch), lambda s, j: (0, 0, j)),
            full((2 * L, L)), full((L, 2 * L)),
        ],
        out_specs=pl.BlockSpec((seqs * L, ch), lambda s, j: (s, j)),
        out_shape=jax.ShapeDtypeStruct((nseq * L, HY_W), BF16),
        compiler_params=_params("arbitrary", "arbitrary"),
        name=f"hyena_{L}",
    )(proj, proj, proj, proj, prm, g, fwd, inv)


def _hyena_params(conv_w, conv_b, d):
    rows = [conv_w[:, :, s * HY_W:(s + 1) * HY_W] for s in range(3)]
    rows += [conv_b[:, None, s * HY_W:(s + 1) * HY_W] for s in range(3)]
    rows += [d[:, None, :], jnp.zeros((DEPTH, HY_PARAM_ROWS - 13, HY_W), F32)]
    return jnp.concatenate(rows, axis=1)


OUT_TM = 512


def _outproj_kernel(x_ref, gate_ref, att_ref, lru_ref, hy_ref, w_ref, fg_ref, o_ref, *, final):
    for r0 in range(0, OUT_TM, OUT_TM // 2):
        rows = slice(r0, r0 + OUT_TM // 2)
        acc = _dot(att_ref[rows, :], w_ref[0:ATT_W, :])
        acc = acc + _dot(lru_ref[rows, :], w_ref[ATT_W:ATT_W + LRU_W, :])
        acc = acc + _dot(hy_ref[rows, :], w_ref[ATT_W + LRU_W:, :])
        y = x_ref[rows, :] + gate_ref[...] * acc
        if final:
            ms = jnp.mean(y * y, axis=-1, keepdims=True)
            y = (y * lax.rsqrt(ms + EPS)) * fg_ref[...]
        o_ref[rows, :] = y


def _outproj(x, mod4, att, lru, hy, w_out_bf, final_g, layer, final, seq_len):
    n_tok = x.shape[0]
    tm = OUT_TM
    cond = _cond_row(seq_len, tm)
    return pl.pallas_call(
        functools.partial(_outproj_kernel, final=final),
        grid=(n_tok // tm,),
        in_specs=[
            pl.BlockSpec((tm, D_MODEL), lambda i: (i, 0)),
            pl.BlockSpec((None, None, 1, D_MODEL), lambda i: (layer, cond(i), 0, 2)),
            pl.BlockSpec((tm, ATT_W), lambda i: (i, 0)),
            pl.BlockSpec((tm, LRU_W), lambda i: (i, 0)),
            pl.BlockSpec((tm, HY_W), lambda i: (i, 0)),
            pl.BlockSpec((None, D_MODEL, D_MODEL), lambda i: (layer, 0, 0), pipeline_mode=pl.Buffered(1)),
            pl.BlockSpec((1, D_MODEL), lambda i: (0, 0)),
        ],
        out_specs=pl.BlockSpec((tm, D_MODEL), lambda i: (i, 0)),
        out_shape=jax.ShapeDtypeStruct((n_tok, D_MODEL), F32),
        compiler_params=_params("arbitrary"),
        name="outproj",
    )(x, mod4, att, lru, hy, w_out_bf, final_g)


def kernel(x_prompt, x_sample, cache_k, cache_v, state_lru, c, c_ctx, norm_g, w_ada, b_ada, w_in, w_out, lam_q1, lam_k1, lam_q2, lam_k2, attn_subln_g, lru_conv_w, lru_conv_b, lru_wa, lru_ba, lru_wi, lru_bi, lru_lam, hy_conv_w, hy_conv_b, hy_w1, hy_b1, hy_w2, hy_b2, hy_w3, hy_d, final_g):
    cos, sin = _rope_tables()
    tables = {}
    for L in (SEQ, DEC_SEQ):
        fwd, inv = _dft_tables(L)
        z, decay = _hyena_tables(L)
        tables[L] = (fwd.astype(BF16), inv.astype(BF16), z, decay)

    cond = jnp.concatenate([c_ctx[None, :], c, jnp.zeros((N_COND - 1 - DEC_BATCH, D_MODEL), F32)], axis=0)
    mod4 = _ada(cond, w_ada, b_ada).reshape(DEPTH, N_COND, 1, 3 * D_MODEL)

    xc = x_prompt.reshape(T_CTX, D_MODEL)
    xs = x_sample.reshape(T_SMP, D_MODEL)
    ck = cache_k.reshape(DEC_BATCH, DEPTH, PAST_LEN * N_HEADS, HEAD_W)
    cv = cache_v.reshape(DEC_BATCH, DEPTH, PAST_LEN * N_HEADS, HEAD_W)
    h0_ctx = jnp.zeros((1, 2, BATCH, LRU_W), F32)
    h0_smp = jnp.transpose(state_lru, (1, 2, 0, 3))
    norm_g3 = norm_g[:, None, :]
    lamp = jnp.stack([lam_q1, lam_k1, lam_q2, lam_k2], axis=1)
    subln = attn_subln_g[:, None, :]
    w_out_bf = w_out.astype(BF16)
    lru_args = (lru_conv_w, lru_conv_b[:, None, :], lru_wa, lru_wi, lru_ba, lru_bi, lru_lam)
    hy_prm = _hyena_params(hy_conv_w, hy_conv_b, hy_d)
    filt_args = (hy_w1, hy_b1[:, None, :], hy_w2, hy_b2[:, None, :], hy_w3)
    fg = final_g[None, :]

    assert DEPTH == 2
    caches, hs = None, []
    for l in range(DEPTH):
        final = l == DEPTH - 1
        lam_init = 0.8 - 0.6 * math.exp(-0.3 * l)

        fwd, inv, z, decay = tables[DEC_SEQ]
        proj, w_in_bf = _inproj_w32(xs, norm_g3, mod4, w_in, l, DEC_SEQ)
        att = _attn_smp(proj, ck, cv, lamp, subln, cos, sin, l, lam_init)
        lru, _ = _lru(proj, *lru_args, h0_smp, l, l, DEC_SEQ, DEC_BATCH)
        g = _filters(z, decay, *filt_args, fwd, l, DEC_SEQ)
        hy = _hyena(proj, hy_prm, g, fwd, inv, l, DEC_SEQ, DEC_BATCH, HY_W // 2, 2)
        xs = _outproj(xs, mod4, att, lru, hy, w_out_bf, fg, l, final, DEC_SEQ)

        fwd, inv, z, decay = tables[SEQ]
        proj, *caches = _inproj_cache(xc, norm_g3, mod4, w_in_bf, l, caches)
        att = _attn_ctx(proj, lamp, subln, l, lam_init)
        lru, h_last = _lru(proj, *lru_args, h0_ctx, l, 0, SEQ, 8)
        hs.append(h_last)
        g = _filters(z, decay, *filt_args, fwd, l, SEQ)
        hy = _hyena(proj, hy_prm, g, fwd, inv, l, SEQ, BATCH, HY_W, 2)
        xc = _outproj(xc, mod4, att, lru, hy, w_out_bf, fg, l, final, None)

    y_prompt = xc.reshape(BATCH, SEQ, D_MODEL)
    y_sample = xs.reshape(DEC_BATCH, DEC_SEQ, D_MODEL)
    new_k, new_v = (a.reshape(BATCH, DEPTH, SEQ, N_HEADS, HEAD_W) for a in caches)
    new_state = jnp.transpose(jnp.stack(hs, axis=0), (2, 0, 1, 3))
    return (y_prompt, y_sample, new_k, new_v, new_state)
```
